```python
import math
import jax, jax.numpy as jnp
from jax import lax
import numpy as np

D_MODEL = 1024
BATCH = 8
SEQ = 2048
DEPTH = 1
DEC_BATCH = 128
DEC_SEQ = 1
PAST_LEN = 16384
PAGE_SIZE = 128

GLA_HEADS = 4
GLA_DK = 128
GLA_DV = 256
GLA_RANK = 16
GLA_TAU = 16.0
HGRN_HEADS = 8
HGRN_EXPAND = 128
HGRN_DV = 128
CHUNK = 64
DEEPNORM_ALPHA = (2.0 * DEPTH) ** 0.25
DEEPNORM_BETA = (8.0 * DEPTH) ** -0.25
NORM_EPS = 1e-5

GLA_K = GLA_HEADS * GLA_DK
GLA_V = GLA_HEADS * GLA_DV
HGRN_K = HGRN_HEADS * HGRN_EXPAND
HGRN_V = HGRN_HEADS * HGRN_DV
SPLITS = (GLA_K, GLA_K, GLA_V, GLA_V, GLA_RANK, HGRN_K, HGRN_K, HGRN_V, HGRN_V, D_MODEL, D_MODEL)
IN_DIM = int(sum(SPLITS))
SPLIT_POINTS = [int(s) for s in np.cumsum(SPLITS)[:-1]]

kernel_name = "gla_hgrn2_parallel_gated_deepnorm_step"


def _chunk_gated_linear(q, k, v, log_decay, s0):
    f32 = jnp.float32
    B, T, H, DK = q.shape
    DV = v.shape[-1]
    C = min(CHUNK, T)
    n = -(-T // C)
    pad = n * C - T
    q, k, v, log_decay = (a.astype(f32) for a in (q, k, v, log_decay))
    if pad > 0:
        cfg = ((0, 0), (0, pad), (0, 0), (0, 0))
        q, k, v, log_decay = (jnp.pad(a, cfg) for a in (q, k, v, log_decay))

    def to_chunks(a):
        d = a.shape[-1]
        return a.reshape(B, n, C, H, d).transpose(1, 0, 3, 2, 4)

    qc, kc, vc, gc = to_chunks(q), to_chunks(k), to_chunks(v), to_chunks(log_decay)
    mask = jnp.tril(jnp.ones((C, C), dtype=bool))

    def step(S, inp):
        qb, kb, vb, gb = inp
        b = jnp.cumsum(gb, axis=2)
        o_inter = jnp.einsum('bhtk,bhkv->bhtv', qb * jnp.exp(b), S)
        diff = b[:, :, :, None, :] - b[:, :, None, :, :]
        decay = jnp.exp(jnp.where(mask[None, None, :, :, None], diff, -jnp.inf))
        scores = jnp.einsum('bhtk,bhsk,bhtsk->bhts', qb, kb, decay)
        o_intra = jnp.einsum('bhts,bhsv->bhtv', scores, vb)
        b_last = b[:, :, -1:, :]
        S_new = jnp.exp(b_last[:, :, 0, :])[..., None] * S + jnp.einsum(
            'bhsk,bhsv->bhkv', kb * jnp.exp(b_last - b), vb)
        return S_new, o_inter + o_intra

    s_final, o = lax.scan(step, s0.astype(f32), (qc, kc, vc, gc))
    o = o.transpose(1, 0, 3, 2, 4).reshape(B, n * C, H, DV)[:, :T]
    return o, s_final


def _head_rmsnorm(o, g):
    o = o * lax.rsqrt(jnp.mean(o * o, axis=-1, keepdims=True) + NORM_EPS)
    return o * g.astype(jnp.float32)


def _mixer_layer(x, s_gla, s_hgrn, w_in, w_gate_lr, b_gate_lr, gla_norm_g, w_br_gla,
                 lb, hgrn_norm_g, w_br_hgrn, w_out, ln_g, ln_b):
    f32 = jnp.float32
    B, T, _ = x.shape
    h = jnp.einsum('btd,de->bte', x, w_in)
    gq, gk, gv, gr, ga, hq, hf, hi, hr, mg_a, mg_b = jnp.split(h, SPLIT_POINTS, axis=-1)

    q = gq.astype(f32).reshape(B, T, GLA_HEADS, GLA_DK) * (GLA_DK ** -0.5)
    k = gk.astype(f32).reshape(B, T, GLA_HEADS, GLA_DK)
    v = gv.astype(f32).reshape(B, T, GLA_HEADS, GLA_DV)
    a_logit = jnp.einsum('btr,rk->btk', ga, w_gate_lr) + b_gate_lr
    log_alpha = jax.nn.log_sigmoid(a_logit.astype(f32)).reshape(B, T, GLA_HEADS, GLA_DK) / GLA_TAU
    o_a, s_gla_new = _chunk_gated_linear(q, k, v, log_alpha, s_gla)
    o_a = _head_rmsnorm(o_a, gla_norm_g).reshape(B, T, GLA_V).astype(x.dtype) * jax.nn.silu(gr)
    p_a = jnp.einsum('btv,vd->btd', o_a, w_br_gla)

    qh = jax.nn.silu(hq.astype(f32)).reshape(B, T, HGRN_HEADS, HGRN_EXPAND)
    lbf = lb.astype(f32)
    hf32 = hf.astype(f32)
    f = lbf + (1.0 - lbf) * jax.nn.sigmoid(hf32)
    log_f = jnp.log(f).reshape(B, T, HGRN_HEADS, HGRN_EXPAND)
    kh = ((1.0 - lbf) * jax.nn.sigmoid(-hf32)).reshape(B, T, HGRN_HEADS, HGRN_EXPAND)
    ih = hi.astype(f32).reshape(B, T, HGRN_HEADS, HGRN_DV)
    o_b, s_hgrn_new = _chunk_gated_linear(qh, kh, ih, log_f, s_hgrn)
    o_b = _head_rmsnorm(o_b, hgrn_norm_g).reshape(B, T, HGRN_V).astype(x.dtype) * jax.nn.silu(hr)
    p_b = jnp.einsum('btv,vd->btd', o_b, w_br_hgrn)

    merged = jax.nn.sigmoid(mg_a) * p_a + jax.nn.sigmoid(mg_b) * p_b
    y = jnp.einsum('btd,de->bte', merged, w_out)

    z = (DEEPNORM_ALPHA * x + y).astype(f32)
    mu = jnp.mean(z, axis=-1, keepdims=True)
    zc = z - mu
    var = jnp.mean(zc * zc, axis=-1, keepdims=True)
    out = zc * lax.rsqrt(var + NORM_EPS) * ln_g.astype(f32) + ln_b.astype(f32)
    return out.astype(x.dtype), s_gla_new.astype(s_gla.dtype), s_hgrn_new.astype(s_hgrn.dtype)


def setup_inputs(seed: int = 0) -> dict:
    key = jax.random.key(seed)
    ks = jax.random.split(key, 16)
    f32 = jnp.float32
    nrm = lambda k, shape, s: (jax.random.normal(k, shape, f32) * s)
    return {
        "x_prompt": nrm(ks[0], (BATCH, SEQ, D_MODEL), 1.0),
        "x_sample": nrm(ks[1], (DEC_BATCH, DEC_SEQ, D_MODEL), 1.0),
        "state_gla": nrm(ks[2], (DEPTH, DEC_BATCH, GLA_HEADS, GLA_DK, GLA_DV), 1.0),
        "state_hgrn": nrm(ks[3], (DEPTH, DEC_BATCH, HGRN_HEADS, HGRN_EXPAND, HGRN_DV), 0.5),
        "w_in": nrm(ks[4], (DEPTH, D_MODEL, IN_DIM), D_MODEL ** -0.5),
        "w_gate_lr": nrm(ks[5], (DEPTH, GLA_RANK, GLA_K), GLA_RANK ** -0.5),
        "b_gate_lr": nrm(ks[6], (DEPTH, GLA_K), 0.1),
        "gla_norm_g": 1.0 + nrm(ks[7], (DEPTH, GLA_HEADS, GLA_DV), 0.02),
        "w_br_gla": nrm(ks[8], (DEPTH, GLA_V, D_MODEL), DEEPNORM_BETA * GLA_V ** -0.5),
        "hgrn_lb_param": nrm(ks[9], (DEPTH + 1, HGRN_K), 0.1),
        "hgrn_norm_g": 1.0 + nrm(ks[10], (DEPTH, HGRN_HEADS, HGRN_DV), 0.02),
        "w_br_hgrn": nrm(ks[11], (DEPTH, HGRN_V, D_MODEL), DEEPNORM_BETA * HGRN_V ** -0.5),
        "w_out": nrm(ks[12], (DEPTH, D_MODEL, D_MODEL), DEEPNORM_BETA * D_MODEL ** -0.5),
        "ln_g": 1.0 + nrm(ks[13], (DEPTH, D_MODEL), 0.02),
        "ln_b": nrm(ks[14], (DEPTH, D_MODEL), 0.02),
    }


def reference(x_prompt, x_sample, state_gla, state_hgrn, w_in, w_gate_lr, b_gate_lr,
              gla_norm_g, w_br_gla, hgrn_lb_param, hgrn_norm_g, w_br_hgrn, w_out, ln_g, ln_b):
    lb_all = jnp.cumsum(jax.nn.softmax(hgrn_lb_param.astype(jnp.float32), axis=0), axis=0)
    yp, ys = x_prompt, x_sample
    gla_p, hgrn_p, gla_s, hgrn_s = [], [], [], []
    for l in range(DEPTH):
        weights = (w_in[l], w_gate_lr[l], b_gate_lr[l], gla_norm_g[l], w_br_gla[l], lb_all[l],
                   hgrn_norm_g[l], w_br_hgrn[l], w_out[l], ln_g[l], ln_b[l])
        s0_gla = jnp.zeros((yp.shape[0], GLA_HEADS, GLA_DK, GLA_DV), state_gla.dtype)
        s0_hgrn = jnp.zeros((yp.shape[0], HGRN_HEADS, HGRN_EXPAND, HGRN_DV), state_hgrn.dtype)
        yp, sg_p, sh_p = _mixer_layer(yp, s0_gla, s0_hgrn, *weights)
        ys, sg_s, sh_s = _mixer_layer(ys, state_gla[l], state_hgrn[l], *weights)
        gla_p.append(sg_p)
        hgrn_p.append(sh_p)
        gla_s.append(sg_s)
        hgrn_s.append(sh_s)
    new_gla_prompt = jnp.stack(gla_p, axis=0)
    new_hgrn_prompt = jnp.stack(hgrn_p, axis=0)
    new_gla_sample = jnp.stack(gla_s, axis=0)
    new_hgrn_sample = jnp.stack(hgrn_s, axis=0)
    return (yp, ys, new_gla_prompt, new_hgrn_prompt, new_gla_sample, new_hgrn_sample)
```

```python
import jax
import jax.numpy as jnp
from jax import lax
from jax.experimental import pallas as pl
from jax.experimental.pallas import tpu as pltpu

F32 = jnp.float32
BF16 = jnp.bfloat16

D_MODEL = 1024
GLA_H, GLA_DK, GLA_DV = 4, 128, 256
HG_H, HG_DK, HG_DV = 8, 128, 128
GLA_RANK = 16
GLA_TAU = 16.0
GLA_K = GLA_H * GLA_DK
GLA_V = GLA_H * GLA_DV
HG_K = HG_H * HG_DK
HG_V = HG_H * HG_DV
CHUNK = 64
SUB = 16
NORM_EPS = 1e-5
DEPTH = 1
DEEPNORM_ALPHA = (2.0 * DEPTH) ** 0.25
LANES = 128
TOK_BLOCK = 256
COL_TILE = 256
SEQ_BLOCK = 8
VMEM_LIMIT = 52 * 1024 * 1024


def _dot(a, b):
    return jnp.dot(a, b, preferred_element_type=F32)


def _dot_nt(a, b):
    return lax.dot_general(a, b, (((1,), (1,)), ((), ())), preferred_element_type=F32)


def _dot_tn(a, b):
    return lax.dot_general(a, b, (((0,), (0,)), ((), ())), preferred_element_type=F32)


def _sigmoid(x):
    return 1.0 / (1.0 + jnp.exp(-x))


def _silu(x):
    return x * _sigmoid(x)


def _log_sigmoid(x):
    return jnp.minimum(x, 0.0) - jnp.log1p(jnp.exp(-jnp.abs(x)))


def _split3(x):
    hi = x.astype(BF16)
    r = x - hi.astype(F32)
    mid = r.astype(BF16)
    lo = (r - mid.astype(F32)).astype(BF16)
    return hi, mid, lo


def _chunk_cumsum(tri, g):
    hi, mid, lo = _split3(g)
    return _dot(tri, hi) + _dot(tri, mid) + _dot(tri, lo)


def _block_tri(n):
    r = lax.broadcasted_iota(jnp.int32, (n, n), 0)
    c = lax.broadcasted_iota(jnp.int32, (n, n), 1)
    shift = CHUNK.bit_length() - 1
    keep = (c <= r) & (jnp.right_shift(r, shift) == jnp.right_shift(c, shift))
    return jnp.where(keep, 1.0, 0.0).astype(BF16)


def _lower_bound(lbp):
    rows = [lbp[i:i + 1, :] for i in range(lbp.shape[0])]
    m = rows[0]
    for r in rows[1:]:
        m = jnp.maximum(m, r)
    es = [jnp.exp(r - m) for r in rows]
    tot = es[0]
    for e in es[1:]:
        tot = tot + e
    return es[0] / tot


def _intra_scores(q, k, b, causal):
    dk = q.shape[1]
    rows = []
    for i in range(CHUNK // SUB):
        lo, hi = i * SUB, (i + 1) * SUB
        bs = b[lo:lo + 1, :]
        qi = (q[lo:hi] * jnp.exp(b[lo:hi] - bs)).astype(BF16)
        kj = (k[0:hi] * jnp.exp(bs - b[0:hi])).astype(BF16)
        if hi < CHUNK:
            kj = jnp.concatenate([kj, jnp.zeros((CHUNK - hi, dk), BF16)], axis=0)
        rows.append(_dot_nt(qi, kj))
    a = jnp.concatenate(rows, axis=0)
    return jnp.where(causal, a, 0.0)


def _chunk_step(q, k, b, v_bf, s, dcol, causal):
    blast = b[CHUNK - 1:CHUNK, :]
    qe = (q * jnp.exp(b)).astype(BF16)
    o = _dot(qe, s.astype(BF16))
    a = _intra_scores(q, k, b, causal)
    o = o + _dot(a.astype(BF16), v_bf)
    kd = (k * jnp.exp(blast - b)).astype(BF16)
    s_new = s * dcol + _dot_tn(kd, v_bf)
    return o, s_new


def _columns(rows):
    r = rows.shape[0]
    if r < LANES:
        rows = jnp.concatenate([rows, jnp.zeros((LANES - r, LANES), F32)], axis=0)
    return rows.T


def _recurrence(q_s, k_s, b_s, v_s, o_s, state_ref, nheads, dk, dv, nchunks):
    r = lax.broadcasted_iota(jnp.int32, (CHUNK, CHUNK), 0)
    c = lax.broadcasted_iota(jnp.int32, (CHUNK, CHUNK), 1)
    causal = c <= r
    half = CHUNK // 2
    for h in range(nheads):
        cols = _columns(jnp.exp(b_s[h, pl.ds(half - 1, 2 * nchunks, stride=half), :]))
        s = state_ref[0, 0, h]
        for ci in range(nchunks):
            rows = slice(ci * CHUNK, (ci + 1) * CHUNK)
            kcols = slice(h * dk, (h + 1) * dk)
            vcols = slice(h * dv, (h + 1) * dv)
            o, s = _chunk_step(q_s[rows, kcols], k_s[rows, kcols], b_s[h, rows, :],
                               v_s[rows, vcols], s, cols[:, 2 * ci + 1:2 * ci + 2], causal)
            o_s[rows, vcols] = o
        state_ref[0, 0, h] = s


def _store_decay(b_s, n, bt):
    per = COL_TILE // LANES
    for j in range(per):
        b_s[n * per + j] = bt[:, j * LANES:(j + 1) * LANES]


def _head_rmsnorm(o, gain, dv):
    outs = []
    for j in range(o.shape[1] // dv):
        oj = o[:, j * dv:(j + 1) * dv]
        ms = jnp.mean(oj * oj, axis=-1, keepdims=True)
        outs.append(oj * lax.rsqrt(ms + NORM_EPS))
    on = outs[0] if len(outs) == 1 else jnp.concatenate(outs, axis=1)
    return on * gain


def _layernorm(z, g, b):
    mu = jnp.mean(z, axis=-1, keepdims=True)
    zc = z - mu
    var = jnp.mean(zc * zc, axis=-1, keepdims=True)
    return zc * lax.rsqrt(var + NORM_EPS) * g + b


def _prompt_kernel(x_ref, wa_ref, wga_ref, wglr_ref, bg_ref, gng_ref, wbg_ref,
                   wb_ref, lbp_ref, hng_ref, wbh_ref, wmg_ref, wout_ref, lng_ref, lnb_ref,
                   y_ref, sg_ref, sh_ref,
                   xb_s, q_s, k_s, b_s, v_s, o_s, g_s, p_s):
    tb = TOK_BLOCK
    nch = tb // CHUNK

    @pl.when(pl.program_id(1) == 0)
    def _():
        sg_ref[...] = jnp.zeros_like(sg_ref)
        sh_ref[...] = jnp.zeros_like(sh_ref)

    xb_s[...] = x_ref[0].astype(BF16)
    tri = _block_tri(tb)

    scale = GLA_DK ** -0.5
    for n in range(GLA_K // COL_TILE):
        cs = slice(n * COL_TILE, (n + 1) * COL_TILE)
        q_s[:, cs] = _dot(xb_s[...], wa_ref[:, cs]) * scale
        k_s[:, cs] = _dot(xb_s[...], wa_ref[:, GLA_K + n * COL_TILE:GLA_K + (n + 1) * COL_TILE])
    for n in range(GLA_V // COL_TILE):
        cs = slice(n * COL_TILE, (n + 1) * COL_TILE)
        v_s[:, cs] = _dot(xb_s[...], wa_ref[:, 2 * GLA_K + n * COL_TILE:2 * GLA_K + (n + 1) * COL_TILE]).astype(BF16)
    ga = _dot(xb_s[...], wga_ref[...]).astype(BF16)
    for n in range(GLA_K // COL_TILE):
        cs = slice(n * COL_TILE, (n + 1) * COL_TILE)
        a_logit = _dot(ga, wglr_ref[:, cs]) + bg_ref[:, cs]
        _store_decay(b_s, n, _chunk_cumsum(tri, _log_sigmoid(a_logit) * (1.0 / GLA_TAU)))
    _recurrence(q_s, k_s, b_s, v_s, o_s, sg_ref, GLA_H, GLA_DK, GLA_DV, nch)
    for n in range(GLA_V // COL_TILE):
        cs = slice(n * COL_TILE, (n + 1) * COL_TILE)
        gr = _dot(xb_s[...], wa_ref[:, 2 * GLA_K + GLA_V + n * COL_TILE:2 * GLA_K + GLA_V + (n + 1) * COL_TILE])
        on = _head_rmsnorm(o_s[:, cs], gng_ref[:, cs], GLA_DV)
        g_s[:, cs] = (on * _silu(gr)).astype(BF16)
    for n in range(D_MODEL // COL_TILE):
        cs = slice(n * COL_TILE, (n + 1) * COL_TILE)
        mg = _dot(xb_s[...], wmg_ref[:, cs])
        p_s[:, cs] = _sigmoid(mg) * _dot(g_s[...], wbg_ref[:, cs])

    lb = _lower_bound(lbp_ref[...])
    for n in range(HG_K // COL_TILE):
        cs = slice(n * COL_TILE, (n + 1) * COL_TILE)
        q_s[:, cs] = _silu(_dot(xb_s[...], wb_ref[:, cs]))
        hf = _dot(xb_s[...], wb_ref[:, HG_K + n * COL_TILE:HG_K + (n + 1) * COL_TILE])
        lbn = lb[:, cs]
        f = lbn + (1.0 - lbn) * _sigmoid(hf)
        k_s[:, cs] = (1.0 - lbn) * _sigmoid(-hf)
        _store_decay(b_s, n, _chunk_cumsum(tri, jnp.log(f)))
    for n in range(HG_V // COL_TILE):
        cs = slice(n * COL_TILE, (n + 1) * COL_TILE)
        v_s[:, cs] = _dot(xb_s[...], wb_ref[:, 2 * HG_K + n * COL_TILE:2 * HG_K + (n + 1) * COL_TILE]).astype(BF16)
    _recurrence(q_s, k_s, b_s, v_s, o_s, sh_ref, HG_H, HG_DK, HG_DV, nch)
    for n in range(HG_V // COL_TILE):
        cs = slice(n * COL_TILE, (n + 1) * COL_TILE)
        hr = _dot(xb_s[...], wb_ref[:, 2 * HG_K + HG_V + n * COL_TILE:2 * HG_K + HG_V + (n + 1) * COL_TILE])
        on = _head_rmsnorm(o_s[:, cs], hng_ref[:, cs], HG_DV)
        g_s[:, cs] = (on * _silu(hr)).astype(BF16)
    for n in range(D_MODEL // COL_TILE):
        cs = slice(n * COL_TILE, (n + 1) * COL_TILE)
        mg = _dot(xb_s[...], wmg_ref[:, D_MODEL + n * COL_TILE:D_MODEL + (n + 1) * COL_TILE])
        merged = p_s[:, cs] + _sigmoid(mg) * _dot(g_s[...], wbh_ref[:, cs])
        v_s[:, cs] = merged.astype(BF16)

    for n in range(D_MODEL // COL_TILE):
        cs = slice(n * COL_TILE, (n + 1) * COL_TILE)
        p_s[:, cs] = DEEPNORM_ALPHA * x_ref[0, :, cs] + _dot(v_s[...], wout_ref[:, cs])
    y_ref[0] = _layernorm(p_s[...], lng_ref[...], lnb_ref[...])


def _whole(shape):
    return pl.BlockSpec(memory_space=pltpu.VMEM)


def _prompt_call(x, wa, wga, wglr, bg, gng, wbg, wb, lbp, hng, wbh, wmg, wout, lng, lnb):
    bsz, seq, d = x.shape
    tb = TOK_BLOCK
    weights = (wa, wga, wglr, bg, gng, wbg, wb, lbp, hng, wbh, wmg, wout, lng, lnb)
    return pl.pallas_call(
        _prompt_kernel,
        grid=(bsz, seq // tb),
        in_specs=[pl.BlockSpec((1, tb, d), lambda b, t: (b, t, 0))] + [_whole(w.shape) for w in weights],
        out_specs=[
            pl.BlockSpec((1, tb, d), lambda b, t: (b, t, 0)),
            pl.BlockSpec((1, 1, GLA_H, GLA_DK, GLA_DV), lambda b, t: (0, b, 0, 0, 0)),
            pl.BlockSpec((1, 1, HG_H, HG_DK, HG_DV), lambda b, t: (0, b, 0, 0, 0)),
        ],
        out_shape=[
            jax.ShapeDtypeStruct((bsz, seq, d), F32),
            jax.ShapeDtypeStruct((DEPTH, bsz, GLA_H, GLA_DK, GLA_DV), F32),
            jax.ShapeDtypeStruct((DEPTH, bsz, HG_H, HG_DK, HG_DV), F32),
        ],
        scratch_shapes=[
            pltpu.VMEM((tb, d), BF16),
            pltpu.VMEM((tb, HG_K), F32),
            pltpu.VMEM((tb, HG_K), F32),
            pltpu.VMEM((HG_H, tb, LANES), F32),
            pltpu.VMEM((tb, HG_V), BF16),
            pltpu.VMEM((tb, HG_V), F32),
            pltpu.VMEM((tb, d), BF16),
            pltpu.VMEM((tb, d), F32),
        ],
        compiler_params=pltpu.CompilerParams(
            dimension_semantics=("arbitrary", "arbitrary"), vmem_limit_bytes=VMEM_LIMIT),
        name="prompt_layer",
    )(x, *weights)


def _sample_proj_kernel(x_ref, wa_ref, wga_ref, wglr_ref, bg_ref, wb_ref, lbp_ref, wmg_ref,
                        eg_ref, kg_ref, qg_ref, vg_ref, eh_ref, kh_ref, qh_ref, vh_ref,
                        sgr_ref, shr_ref, mga_ref, mgb_ref):
    xb = x_ref[...].astype(BF16)
    scale = GLA_DK ** -0.5
    qg_ref[...] = _dot(xb, wa_ref[:, 0:GLA_K]) * scale
    kg_ref[...] = _dot(xb, wa_ref[:, GLA_K:2 * GLA_K])
    vg_ref[...] = _dot(xb, wa_ref[:, 2 * GLA_K:2 * GLA_K + GLA_V])
    sgr_ref[...] = _silu(_dot(xb, wa_ref[:, 2 * GLA_K + GLA_V:2 * GLA_K + 2 * GLA_V]))
    ga = _dot(xb, wga_ref[...]).astype(BF16)
    a_logit = _dot(ga, wglr_ref[...]) + bg_ref[...]
    eg_ref[...] = jnp.exp(_log_sigmoid(a_logit) * (1.0 / GLA_TAU))
    lb = _lower_bound(lbp_ref[...])
    qh_ref[...] = _silu(_dot(xb, wb_ref[:, 0:HG_K]))
    hf = _dot(xb, wb_ref[:, HG_K:2 * HG_K])
    eh_ref[...] = jnp.exp(jnp.log(lb + (1.0 - lb) * _sigmoid(hf)))
    kh_ref[...] = (1.0 - lb) * _sigmoid(-hf)
    vh_ref[...] = _dot(xb, wb_ref[:, 2 * HG_K:2 * HG_K + HG_V])
    shr_ref[...] = _silu(_dot(xb, wb_ref[:, 2 * HG_K + HG_V:2 * HG_K + 2 * HG_V]))
    mga_ref[...] = _sigmoid(_dot(xb, wmg_ref[:, 0:D_MODEL]))
    mgb_ref[...] = _sigmoid(_dot(xb, wmg_ref[:, D_MODEL:2 * D_MODEL]))


def _sample_proj_call(x, wa, wga, wglr, bg, wb, lbp, wmg):
    n = x.shape[0]
    widths = (GLA_K, GLA_K, GLA_K, GLA_V, HG_K, HG_K, HG_K, HG_V, GLA_V, HG_V, D_MODEL, D_MODEL)
    args = (x, wa, wga, wglr, bg, wb, lbp, wmg)
    return pl.pallas_call(
        _sample_proj_kernel,
        in_specs=[_whole(a.shape) for a in args],
        out_specs=[_whole((n, w)) for w in widths],
        out_shape=[jax.ShapeDtypeStruct((n, w), F32) for w in widths],
        compiler_params=pltpu.CompilerParams(vmem_limit_bytes=VMEM_LIMIT),
        name="sample_proj",
    )(*args)


def _state_update(e, k, q, v, s_ref, snew_ref, o_ref, nheads, dk, dv):
    for h in range(nheads):
        ks = slice(h * dk, (h + 1) * dk)
        ec, kc, qc = (_columns(a[:, ks]) for a in (e, k, q))
        for j in range(SEQ_BLOCK):
            s_new = s_ref[j, h] * ec[:, j:j + 1] + kc[:, j:j + 1] * v[j:j + 1, h * dv:(h + 1) * dv]
            snew_ref[j, h] = s_new
            o_ref[j:j + 1, h * dv:(h + 1) * dv] = jnp.sum(s_new * qc[:, j:j + 1], axis=0, keepdims=True)


def _sample_state_kernel(eg_ref, kg_ref, qg_ref, vg_ref, eh_ref, kh_ref, qh_ref, vh_ref, sg_ref, sh_ref,
                         sgn_ref, shn_ref, oa_ref, ob_ref):
    _state_update(eg_ref[...], kg_ref[...], qg_ref[...], vg_ref[...], sg_ref, sgn_ref, oa_ref,
                  GLA_H, GLA_DK, GLA_DV)
    _state_update(eh_ref[...], kh_ref[...], qh_ref[...], vh_ref[...], sh_ref, shn_ref, ob_ref,
                  HG_H, HG_DK, HG_DV)


def _sample_state_call(eg, kg, qg, vg, eh, kh, qh, vh, sg, sh):
    n = sg.shape[0]
    sb = SEQ_BLOCK
    row = lambda w: pl.BlockSpec((sb, w), lambda i: (i, 0))
    gspec = pl.BlockSpec((sb, GLA_H, GLA_DK, GLA_DV), lambda i: (i, 0, 0, 0))
    hspec = pl.BlockSpec((sb, HG_H, HG_DK, HG_DV), lambda i: (i, 0, 0, 0))
    return pl.pallas_call(
        _sample_state_kernel,
        grid=(n // sb,),
        in_specs=[row(GLA_K), row(GLA_K), row(GLA_K), row(GLA_V), row(HG_K), row(HG_K), row(HG_K), row(HG_V),
                  gspec, hspec],
        out_specs=[gspec, hspec, row(GLA_V), row(HG_V)],
        out_shape=[jax.ShapeDtypeStruct(sg.shape, F32), jax.ShapeDtypeStruct(sh.shape, F32),
                   jax.ShapeDtypeStruct((n, GLA_V), F32), jax.ShapeDtypeStruct((n, HG_V), F32)],
        compiler_params=pltpu.CompilerParams(dimension_semantics=("arbitrary",), vmem_limit_bytes=VMEM_LIMIT),
        name="sample_state",
    )(eg, kg, qg, vg, eh, kh, qh, vh, sg, sh)


def _sample_post_kernel(x_ref, oa_ref, ob_ref, sgr_ref, shr_ref, mga_ref, mgb_ref,
                        gng_ref, wbg_ref, hng_ref, wbh_ref, wout_ref, lng_ref, lnb_ref, y_ref):
    ga = (_head_rmsnorm(oa_ref[...], gng_ref[...], GLA_DV) * sgr_ref[...]).astype(BF16)
    gb = (_head_rmsnorm(ob_ref[...], hng_ref[...], HG_DV) * shr_ref[...]).astype(BF16)
    merged = mga_ref[...] * _dot(ga, wbg_ref[...]) + mgb_ref[...] * _dot(gb, wbh_ref[...])
    z = DEEPNORM_ALPHA * x_ref[...] + _dot(merged.astype(BF16), wout_ref[...])
    y_ref[...] = _layernorm(z, lng_ref[...], lnb_ref[...])


def _sample_post_call(*args):
    n = args[0].shape[0]
    return pl.pallas_call(
        _sample_post_kernel,
        in_specs=[_whole(a.shape) for a in args],
        out_specs=_whole((n, D_MODEL)),
        out_shape=jax.ShapeDtypeStruct((n, D_MODEL), F32),
        compiler_params=pltpu.CompilerParams(vmem_limit_bytes=VMEM_LIMIT),
        name="sample_post",
    )(*args)


def kernel(x_prompt, x_sample, state_gla, state_hgrn, w_in, w_gate_lr, b_gate_lr, gla_norm_g, w_br_gla,
           hgrn_lb_param, hgrn_norm_g, w_br_hgrn, w_out, ln_g, ln_b):
    assert w_in.shape[0] == DEPTH and x_sample.shape[1] == 1
    w = w_in[0]
    c_ga = 2 * GLA_K + 2 * GLA_V
    c_hg = c_ga + GLA_RANK
    c_mg = c_hg + 2 * HG_K + 2 * HG_V
    wa = w[:, 0:c_ga].astype(BF16)
    wga = jnp.pad(w[:, c_ga:c_hg], ((0, 0), (0, LANES - GLA_RANK))).astype(BF16)
    wglr = jnp.pad(w_gate_lr[0], ((0, LANES - GLA_RANK), (0, 0))).astype(BF16)
    wb = w[:, c_hg:c_mg].astype(BF16)
    wmg = w[:, c_mg:].astype(BF16)
    bg = b_gate_lr[0].reshape(1, GLA_K)
    gng = gla_norm_g[0].reshape(1, GLA_V)
    hng = hgrn_norm_g[0].reshape(1, HG_V)
    wbg = w_br_gla[0].astype(BF16)
    wbh = w_br_hgrn[0].astype(BF16)
    wout = w_out[0].astype(BF16)
    lng = ln_g[0].reshape(1, D_MODEL)
    lnb = ln_b[0].reshape(1, D_MODEL)
    lbp = hgrn_lb_param

    y_prompt, gla_p, hgrn_p = _prompt_call(x_prompt, wa, wga, wglr, bg, gng, wbg, wb, lbp, hng, wbh, wmg, wout,
                                           lng, lnb)

    xs = x_sample[:, 0, :]
    eg, kg, qg, vg, eh, kh, qh, vh, sgr, shr, mga, mgb = _sample_proj_call(xs, wa, wga, wglr, bg, wb, lbp, wmg)
    gla_s, hgrn_s, oa, ob = _sample_state_call(eg, kg, qg, vg, eh, kh, qh, vh, state_gla[0], state_hgrn[0])
    ys = _sample_post_call(xs, oa, ob, sgr, shr, mga, mgb, gng, wbg, hng, wbh, wout, lng, lnb)
    return (y_prompt, ys[:, None, :], gla_p, hgrn_p, gla_s[None], hgrn_s[None])
```

```python
import jax
import jax.numpy as jnp
from jax import lax
from jax.experimental import pallas as pl
from jax.experimental.pallas import tpu as pltpu

F32 = jnp.float32
BF16 = jnp.bfloat16

D_MODEL = 1024
GLA_H, GLA_DK, GLA_DV = 4, 128, 256
HG_H, HG_DK, HG_DV = 8, 128, 128
GLA_RANK = 16
GLA_TAU = 16.0
GLA_K = GLA_H * GLA_DK
GLA_V = GLA_H * GLA_DV
HG_K = HG_H * HG_DK
HG_V = HG_H * HG_DV
CHUNK = 64
SUB = 16
NORM_EPS = 1e-5
DEPTH = 1
DEEPNORM_ALPHA = (2.0 * DEPTH) ** 0.25
LANES = 128
TOK_BLOCK = 256
COL_TILE = 256
SEQ_BLOCK = 8
VMEM_LIMIT = 52 * 1024 * 1024


def _dot(a, b):
    return jnp.dot(a, b, preferred_element_type=F32)


def _dot_nt(a, b):
    return lax.dot_general(a, b, (((1,), (1,)), ((), ())), preferred_element_type=F32)


def _dot_tn(a, b):
    return lax.dot_general(a, b, (((0,), (0,)), ((), ())), preferred_element_type=F32)


def _sigmoid(x):
    return 1.0 / (1.0 + jnp.exp(-x))


def _silu(x):
    return x * _sigmoid(x)


def _log_sigmoid(x):
    return jnp.minimum(x, 0.0) - jnp.log1p(jnp.exp(-jnp.abs(x)))


def _split3(x):
    hi = x.astype(BF16)
    r = x - hi.astype(F32)
    mid = r.astype(BF16)
    lo = (r - mid.astype(F32)).astype(BF16)
    return hi, mid, lo


def _chunk_cumsum(tri, g):
    hi, mid, lo = _split3(g)
    return _dot(tri, hi) + _dot(tri, mid) + _dot(tri, lo)


def _block_tri(n):
    r = lax.broadcasted_iota(jnp.int32, (n, n), 0)
    c = lax.broadcasted_iota(jnp.int32, (n, n), 1)
    shift = CHUNK.bit_length() - 1
    keep = (c <= r) & (jnp.right_shift(r, shift) == jnp.right_shift(c, shift))
    return jnp.where(keep, 1.0, 0.0).astype(BF16)


def _lower_bound(lbp):
    rows = [lbp[i:i + 1, :] for i in range(lbp.shape[0])]
    m = rows[0]
    for r in rows[1:]:
        m = jnp.maximum(m, r)
    es = [jnp.exp(r - m) for r in rows]
    tot = es[0]
    for e in es[1:]:
        tot = tot + e
    return es[0] / tot


def _intra_scores(q, k, b, causal):
    dk = q.shape[1]
    rows = []
    for i in range(CHUNK // SUB):
        lo, hi = i * SUB, (i + 1) * SUB
        bs = b[lo:lo + 1, :]
        qi = (q[lo:hi] * jnp.exp(b[lo:hi] - bs)).astype(BF16)
        kj = (k[0:hi] * jnp.exp(bs - b[0:hi])).astype(BF16)
        kj = jnp.concatenate([kj, jnp.zeros((LANES - hi, dk), BF16)], axis=0)
        rows.append(_dot_nt(qi, kj))
    a = jnp.concatenate(rows, axis=0)
    return jnp.where(causal, a, 0.0)


def _columns(rows):
    r = rows.shape[0]
    if r < LANES:
        rows = jnp.concatenate([rows, jnp.zeros((LANES - r, LANES), F32)], axis=0)
    return rows.T


def _recurrence(q_s, k_s, b_s, qa_s, kd_s, w_s, u_s, o_s, state_ref, nheads, dk, dv, nchunks):
    r = lax.broadcasted_iota(jnp.int32, (CHUNK, LANES), 0)
    c = lax.broadcasted_iota(jnp.int32, (CHUNK, LANES), 1)
    causal = c <= r
    pairs = [(ci, h) for ci in range(nchunks) for h in range(nheads)]
    rows_of = lambda ci: slice(ci * CHUNK, (ci + 1) * CHUNK)
    for ci, h in pairs:
        rows, kcols = rows_of(ci), slice(h * dk, (h + 1) * dk)
        q, k, b = q_s[rows, kcols], k_s[rows, kcols], b_s[h, rows, :]
        qa_s[h, rows, 0:dk] = (q * jnp.exp(b)).astype(BF16)
        kd_s[h, rows, :] = (k * jnp.exp(b[CHUNK - 1:CHUNK, :] - b)).astype(BF16)
        qa_s[h, rows, dk:dk + LANES] = _intra_scores(q, k, b, causal).astype(BF16)
    for ci, h in pairs:
        hc = h * nchunks + ci
        u_s[hc] = _dot_tn(kd_s[h, rows_of(ci), :], w_s[hc, dk:dk + CHUNK, :])
    half = CHUNK // 2
    cols = [_columns(jnp.exp(b_s[h, pl.ds(half - 1, 2 * nchunks, stride=half), :])) for h in range(nheads)]
    for ci, h in pairs:
        hc = h * nchunks + ci
        s = state_ref[0, 0, h]
        w_s[hc, 0:dk, :] = s.astype(BF16)
        state_ref[0, 0, h] = s * cols[h][:, 2 * ci + 1:2 * ci + 2] + u_s[hc]
    for ci, h in pairs:
        hc = h * nchunks + ci
        o_s[rows_of(ci), h * dv:(h + 1) * dv] = _dot(qa_s[h, rows_of(ci), :], w_s[hc])


def _store_decay(b_s, n, bt):
    per = COL_TILE // LANES
    for j in range(per):
        b_s[n * per + j] = bt[:, j * LANES:(j + 1) * LANES]


def _store_values(w_s, n, vt, dk, dv, nchunks):
    per = COL_TILE // dv
    for j in range(per):
        for ci in range(nchunks):
            w_s[(n * per + j) * nchunks + ci, dk:dk + CHUNK, :] = vt[ci * CHUNK:(ci + 1) * CHUNK, j * dv:(j + 1) * dv]


def _head_rmsnorm(o, gain, dv):
    outs = []
    for j in range(o.shape[1] // dv):
        oj = o[:, j * dv:(j + 1) * dv]
        ms = jnp.mean(oj * oj, axis=-1, keepdims=True)
        outs.append(oj * lax.rsqrt(ms + NORM_EPS))
    on = outs[0] if len(outs) == 1 else jnp.concatenate(outs, axis=1)
    return on * gain


def _layernorm(z, g, b):
    mu = jnp.mean(z, axis=-1, keepdims=True)
    zc = z - mu
    var = jnp.mean(zc * zc, axis=-1, keepdims=True)
    return zc * lax.rsqrt(var + NORM_EPS) * g + b


def _prompt_kernel(x_ref, wa_ref, wga_ref, wglr_ref, bg_ref, gng_ref, wbg_ref,
                   wb_ref, lbp_ref, hng_ref, wbh_ref, wmg_ref, wout_ref, lng_ref, lnb_ref,
                   y_ref, sg_ref, sh_ref,
                   xb_s, q_s, k_s, b_s, qa_s, kd_s, wg_s, ug_s, wh_s, uh_s, o_s, g_s, m_s, p_s):
    tb = TOK_BLOCK
    nch = tb // CHUNK

    @pl.when(pl.program_id(1) == 0)
    def _():
        sg_ref[...] = jnp.zeros_like(sg_ref)
        sh_ref[...] = jnp.zeros_like(sh_ref)
        wg_s[:, GLA_DK + CHUNK:, :] = jnp.zeros((GLA_H * nch, LANES - CHUNK, GLA_DV), BF16)
        wh_s[:, HG_DK + CHUNK:, :] = jnp.zeros((HG_H * nch, LANES - CHUNK, HG_DV), BF16)

    xb_s[...] = x_ref[0].astype(BF16)
    tri = _block_tri(tb)

    scale = GLA_DK ** -0.5
    for n in range(GLA_K // COL_TILE):
        cs = slice(n * COL_TILE, (n + 1) * COL_TILE)
        q_s[:, cs] = _dot(xb_s[...], wa_ref[:, cs]) * scale
        k_s[:, cs] = _dot(xb_s[...], wa_ref[:, GLA_K + n * COL_TILE:GLA_K + (n + 1) * COL_TILE])
    for n in range(GLA_V // COL_TILE):
        cs = slice(n * COL_TILE, (n + 1) * COL_TILE)
        vt = _dot(xb_s[...], wa_ref[:, 2 * GLA_K + n * COL_TILE:2 * GLA_K + (n + 1) * COL_TILE]).astype(BF16)
        _store_values(wg_s, n, vt, GLA_DK, GLA_DV, nch)
    ga =_dot(xb_s[...], wga_ref[...]).astype(BF16)
    for n in range(GLA_K // COL_TILE):
        cs = slice(n * COL_TILE, (n + 1) * COL_TILE)
        a_logit = _dot(ga, wglr_ref[:, cs]) + bg_ref[:, cs]
        _store_decay(b_s, n, _chunk_cumsum(tri, _log_sigmoid(a_logit) * (1.0 / GLA_TAU)))
    _recurrence(q_s, k_s, b_s, qa_s, kd_s, wg_s, ug_s, o_s, sg_ref, GLA_H, GLA_DK, GLA_DV, nch)
    for n in range(GLA_V // COL_TILE):
        cs = slice(n * COL_TILE, (n + 1) * COL_TILE)
        gr = _dot(xb_s[...], wa_ref[:, 2 * GLA_K + GLA_V + n * COL_TILE:2 * GLA_K + GLA_V + (n + 1) * COL_TILE])
        on = _head_rmsnorm(o_s[:, cs], gng_ref[:, cs], GLA_DV)
        g_s[:, cs] = (on * _silu(gr)).astype(BF16)
    for n in range(D_MODEL // COL_TILE):
        cs = slice(n * COL_TILE, (n + 1) * COL_TILE)
        mg = _dot(xb_s[...], wmg_ref[:, cs])
        p_s[:, cs] = _sigmoid(mg) * _dot(g_s[...], wbg_ref[:, cs])

    lb = _lower_bound(lbp_ref[...])
    for n in range(HG_K // COL_TILE):
        cs = slice(n * COL_TILE, (n + 1) * COL_TILE)
        q_s[:, cs] = _silu(_dot(xb_s[...], wb_ref[:, cs]))
        hf = _dot(xb_s[...], wb_ref[:, HG_K + n * COL_TILE:HG_K + (n + 1) * COL_TILE])
        lbn = lb[:, cs]
        f = lbn + (1.0 - lbn) * _sigmoid(hf)
        k_s[:, cs] = (1.0 - lbn) * _sigmoid(-hf)
        _store_decay(b_s, n, _chunk_cumsum(tri, jnp.log(f)))
    for n in range(HG_V // COL_TILE):
        cs = slice(n * COL_TILE, (n + 1) * COL_TILE)
        vt = _dot(xb_s[...], wb_ref[:, 2 * HG_K + n * COL_TILE:2 * HG_K + (n + 1) * COL_TILE]).astype(BF16)
        _store_values(wh_s, n, vt, HG_DK, HG_DV, nch)
    _recurrence(q_s, k_s, b_s, qa_s, kd_s, wh_s, uh_s, o_s, sh_ref, HG_H, HG_DK, HG_DV, nch)
    for n in range(HG_V // COL_TILE):
        cs = slice(n * COL_TILE, (n + 1) * COL_TILE)
        hr = _dot(xb_s[...], wb_ref[:, 2 * HG_K + HG_V + n * COL_TILE:2 * HG_K + HG_V + (n + 1) * COL_TILE])
        on = _head_rmsnorm(o_s[:, cs], hng_ref[:, cs], HG_DV)
        g_s[:, cs] = (on * _silu(hr)).astype(BF16)
    for n in range(D_MODEL // COL_TILE):
        cs = slice(n * COL_TILE, (n + 1) * COL_TILE)
        mg = _dot(xb_s[...], wmg_ref[:, D_MODEL + n * COL_TILE:D_MODEL + (n + 1) * COL_TILE])
        merged = p_s[:, cs] + _sigmoid(mg) * _dot(g_s[...], wbh_ref[:, cs])
        m_s[:, cs] = merged.astype(BF16)

    for n in range(D_MODEL // COL_TILE):
        cs = slice(n * COL_TILE, (n + 1) * COL_TILE)
        p_s[:, cs] = DEEPNORM_ALPHA * x_ref[0, :, cs] + _dot(m_s[...], wout_ref[:, cs])
    y_ref[0] = _layernorm(p_s[...], lng_ref[...], lnb_ref[...])


def _whole(shape):
    return pl.BlockSpec(memory_space=pltpu.VMEM)


def _prompt_call(x, wa, wga, wglr, bg, gng, wbg, wb, lbp, hng, wbh, wmg, wout, lng, lnb):
    bsz, seq, d = x.shape
    tb = TOK_BLOCK
    nch = tb // CHUNK
    weights =(wa, wga, wglr, bg, gng, wbg, wb, lbp, hng, wbh, wmg, wout, lng, lnb)
    return pl.pallas_call(
        _prompt_kernel,
        grid=(bsz, seq // tb),
        in_specs=[pl.BlockSpec((1, tb, d), lambda b, t: (b, t, 0))] + [_whole(w.shape) for w in weights],
        out_specs=[
            pl.BlockSpec((1, tb, d), lambda b, t: (b, t, 0)),
            pl.BlockSpec((1, 1, GLA_H, GLA_DK, GLA_DV), lambda b, t: (0, b, 0, 0, 0)),
            pl.BlockSpec((1, 1, HG_H, HG_DK, HG_DV), lambda b, t: (0, b, 0, 0, 0)),
        ],
        out_shape=[
            jax.ShapeDtypeStruct((bsz, seq, d), F32),
            jax.ShapeDtypeStruct((DEPTH, bsz, GLA_H, GLA_DK, GLA_DV), F32),
            jax.ShapeDtypeStruct((DEPTH, bsz, HG_H, HG_DK, HG_DV), F32),
        ],
        scratch_shapes=[
            pltpu.VMEM((tb, d), BF16),
            pltpu.VMEM((tb, HG_K), F32),
            pltpu.VMEM((tb, HG_K), F32),
            pltpu.VMEM((HG_H, tb, LANES), F32),
            pltpu.VMEM((HG_H, tb, 2 * LANES), BF16),
            pltpu.VMEM((HG_H, tb, LANES), BF16),
            pltpu.VMEM((GLA_H * nch, 2 * LANES, GLA_DV), BF16),
            pltpu.VMEM((GLA_H * nch, GLA_DK, GLA_DV), F32),
            pltpu.VMEM((HG_H * nch, 2 * LANES, HG_DV), BF16),
            pltpu.VMEM((HG_H * nch, HG_DK, HG_DV), F32),
            pltpu.VMEM((tb, HG_V), F32),
            pltpu.VMEM((tb, d), BF16),
            pltpu.VMEM((tb, d), BF16),
            pltpu.VMEM((tb, d), F32),
        ],
        compiler_params=pltpu.CompilerParams(
            dimension_semantics=("arbitrary", "arbitrary"), vmem_limit_bytes=VMEM_LIMIT),
        name="prompt_layer",
    )(x, *weights)


def _sample_proj_kernel(x_ref, wa_ref, wga_ref, wglr_ref, bg_ref, wb_ref, lbp_ref, wmg_ref,
                        eg_ref, kg_ref, qg_ref, vg_ref, eh_ref, kh_ref, qh_ref, vh_ref,
                        sgr_ref, shr_ref, mga_ref, mgb_ref):
    xb = x_ref[...].astype(BF16)
    scale = GLA_DK ** -0.5
    qg_ref[...] = _dot(xb, wa_ref[:, 0:GLA_K]) * scale
    kg_ref[...] = _dot(xb, wa_ref[:, GLA_K:2 * GLA_K])
    vg_ref[...] = _dot(xb, wa_ref[:, 2 * GLA_K:2 * GLA_K + GLA_V])
    sgr_ref[...] = _silu(_dot(xb, wa_ref[:, 2 * GLA_K + GLA_V:2 * GLA_K + 2 * GLA_V]))
    ga = _dot(xb, wga_ref[...]).astype(BF16)
    a_logit = _dot(ga, wglr_ref[...]) + bg_ref[...]
    eg_ref[...] = jnp.exp(_log_sigmoid(a_logit) * (1.0 / GLA_TAU))
    lb = _lower_bound(lbp_ref[...])
    qh_ref[...] = _silu(_dot(xb, wb_ref[:, 0:HG_K]))
    hf = _dot(xb, wb_ref[:, HG_K:2 * HG_K])
    eh_ref[...] = jnp.exp(jnp.log(lb + (1.0 - lb) * _sigmoid(hf)))
    kh_ref[...] = (1.0 - lb) * _sigmoid(-hf)
    vh_ref[...] = _dot(xb, wb_ref[:, 2 * HG_K:2 * HG_K + HG_V])
    shr_ref[...] = _silu(_dot(xb, wb_ref[:, 2 * HG_K + HG_V:2 * HG_K + 2 * HG_V]))
    mga_ref[...] = _sigmoid(_dot(xb, wmg_ref[:, 0:D_MODEL]))
    mgb_ref[...] = _sigmoid(_dot(xb, wmg_ref[:, D_MODEL:2 * D_MODEL]))


def _sample_proj_call(x, wa, wga, wglr, bg, wb, lbp, wmg):
    n = x.shape[0]
    widths = (GLA_K, GLA_K, GLA_K, GLA_V, HG_K, HG_K, HG_K, HG_V, GLA_V, HG_V, D_MODEL, D_MODEL)
    args = (x, wa, wga, wglr, bg, wb, lbp, wmg)
    return pl.pallas_call(
        _sample_proj_kernel,
        in_specs=[_whole(a.shape) for a in args],
        out_specs=[_whole((n, w)) for w in widths],
        out_shape=[jax.ShapeDtypeStruct((n, w), F32) for w in widths],
        compiler_params=pltpu.CompilerParams(vmem_limit_bytes=VMEM_LIMIT),
        name="sample_proj",
    )(*args)


def _state_update(e, k, q, v, s_ref, snew_ref, o_ref, nheads, dk, dv):
    for h in range(nheads):
        ks = slice(h * dk, (h + 1) * dk)
        ec, kc, qc = (_columns(a[:, ks]) for a in (e, k, q))
        for j in range(SEQ_BLOCK):
            s_new = s_ref[j, h] * ec[:, j:j + 1] + kc[:, j:j + 1] * v[j:j + 1, h * dv:(h + 1) * dv]
            snew_ref[j, h] = s_new
            o_ref[j:j + 1, h * dv:(h + 1) * dv] = jnp.sum(s_new * qc[:, j:j + 1], axis=0, keepdims=True)


def _sample_state_kernel(eg_ref, kg_ref, qg_ref, vg_ref, eh_ref, kh_ref, qh_ref, vh_ref, sg_ref, sh_ref,
                         sgn_ref, shn_ref, oa_ref, ob_ref):
    _state_update(eg_ref[...], kg_ref[...], qg_ref[...], vg_ref[...], sg_ref, sgn_ref, oa_ref,
                  GLA_H, GLA_DK, GLA_DV)
    _state_update(eh_ref[...], kh_ref[...], qh_ref[...], vh_ref[...], sh_ref, shn_ref, ob_ref,
                  HG_H, HG_DK, HG_DV)


def _sample_state_call(eg, kg, qg, vg, eh, kh, qh, vh, sg, sh):
    n = sg.shape[0]
    sb = SEQ_BLOCK
    row = lambda w: pl.BlockSpec((sb, w), lambda i: (i, 0))
    gspec = pl.BlockSpec((sb, GLA_H, GLA_DK, GLA_DV), lambda i: (i, 0, 0, 0))
    hspec = pl.BlockSpec((sb, HG_H, HG_DK, HG_DV), lambda i: (i, 0, 0, 0))
    return pl.pallas_call(
        _sample_state_kernel,
        grid=(n // sb,),
        in_specs=[row(GLA_K), row(GLA_K), row(GLA_K), row(GLA_V), row(HG_K), row(HG_K), row(HG_K), row(HG_V),
                  gspec, hspec],
        out_specs=[gspec, hspec, row(GLA_V), row(HG_V)],
        out_shape=[jax.ShapeDtypeStruct(sg.shape, F32), jax.ShapeDtypeStruct(sh.shape, F32),
                   jax.ShapeDtypeStruct((n, GLA_V), F32), jax.ShapeDtypeStruct((n, HG_V), F32)],
        compiler_params=pltpu.CompilerParams(dimension_semantics=("arbitrary",), vmem_limit_bytes=VMEM_LIMIT),
        name="sample_state",
    )(eg, kg, qg, vg, eh, kh, qh, vh, sg, sh)


def _sample_post_kernel(x_ref, oa_ref, ob_ref, sgr_ref, shr_ref, mga_ref, mgb_ref,
                        gng_ref, wbg_ref, hng_ref, wbh_ref, wout_ref, lng_ref, lnb_ref, y_ref):
    ga = (_head_rmsnorm(oa_ref[...], gng_ref[...], GLA_DV) * sgr_ref[...]).astype(BF16)
    gb = (_head_rmsnorm(ob_ref[...], hng_ref[...], HG_DV) * shr_ref[...]).astype(BF16)
    merged = mga_ref[...] * _dot(ga, wbg_ref[...]) + mgb_ref[...] * _dot(gb, wbh_ref[...])
    z = DEEPNORM_ALPHA * x_ref[...] + _dot(merged.astype(BF16), wout_ref[...])
    y_ref[...] = _layernorm(z, lng_ref[...], lnb_ref[...])


def _sample_post_call(*args):
    n = args[0].shape[0]
    return pl.pallas_call(
        _sample_post_kernel,
        in_specs=[_whole(a.shape) for a in args],
        out_specs=_whole((n, D_MODEL)),
        out_shape=jax.ShapeDtypeStruct((n, D_MODEL), F32),
        compiler_params=pltpu.CompilerParams(vmem_limit_bytes=VMEM_LIMIT),
        name="sample_post",
    )(*args)


def kernel(x_prompt, x_sample, state_gla, state_hgrn, w_in, w_gate_lr, b_gate_lr, gla_norm_g, w_br_gla,
           hgrn_lb_param, hgrn_norm_g, w_br_hgrn, w_out, ln_g, ln_b):
    assert w_in.shape[0] == DEPTH and x_sample.shape[1] == 1
    w = w_in[0]
    c_ga = 2 * GLA_K + 2 * GLA_V
    c_hg = c_ga + GLA_RANK
    c_mg = c_hg + 2 * HG_K + 2 * HG_V
    wa = w[:, 0:c_ga].astype(BF16)
    wga = jnp.pad(w[:, c_ga:c_hg], ((0, 0), (0, LANES - GLA_RANK))).astype(BF16)
    wglr = jnp.pad(w_gate_lr[0], ((0, LANES - GLA_RANK), (0, 0))).astype(BF16)
    wb = w[:, c_hg:c_mg].astype(BF16)
    wmg = w[:, c_mg:].astype(BF16)
    bg = b_gate_lr[0].reshape(1, GLA_K)
    gng = gla_norm_g[0].reshape(1, GLA_V)
    hng = hgrn_norm_g[0].reshape(1, HG_V)
    wbg = w_br_gla[0].astype(BF16)
    wbh = w_br_hgrn[0].astype(BF16)
    wout = w_out[0].astype(BF16)
    lng = ln_g[0].reshape(1, D_MODEL)
    lnb = ln_b[0].reshape(1, D_MODEL)
    lbp = hgrn_lb_param

    y_prompt, gla_p, hgrn_p = _prompt_call(x_prompt, wa, wga, wglr, bg, gng, wbg, wb, lbp, hng, wbh, wmg, wout,
                                           lng, lnb)

    xs = x_sample[:, 0, :]
    eg, kg, qg, vg, eh, kh, qh, vh, sgr, shr, mga, mgb = _sample_proj_call(xs, wa, wga, wglr, bg, wb, lbp, wmg)
    gla_s, hgrn_s, oa, ob = _sample_state_call(eg, kg, qg, vg, eh, kh, qh, vh, state_gla[0], state_hgrn[0])
    ys = _sample_post_call(xs, oa, ob, sgr, shr, mga, mgb, gng, wbg, hng, wbh, wout, lng, lnb)
    return (y_prompt, ys[:, None, :], gla_p, hgrn_p, gla_s[None], hgrn_s[None])
```

```python
from typing import Any, NamedTuple

import jax
import jax.numpy as jnp
from jax import lax
from jax.experimental import pallas as pl
from jax.experimental.pallas import tpu as pltpu

F32 = jnp.float32
BF16 = jnp.bfloat16

D_MODEL = 1024
GLA_H, GLA_DK, GLA_DV = 4, 128, 256
HG_H, HG_DK, HG_DV = 8, 128, 128
GLA_RANK = 16
GLA_TAU = 16.0
GLA_K = GLA_H * GLA_DK
GLA_V = GLA_H * GLA_DV
HG_K = HG_H * HG_DK
HG_V = HG_H * HG_DV
CHUNK = 64
SUB = 16
NORM_EPS = 1e-5
DEPTH = 1
DEEPNORM_ALPHA = (2.0 * DEPTH) ** 0.25
LANES = 128
TOK_BLOCK = 256
COL_TILE = 256
ROW_BLOCK = 64
LN_ROW_BLOCK = 16
SEQ_BLOCK = 8
VMEM_LIMIT = 60 * 1024 * 1024


def _dot(a, b):
    return jnp.dot(a, b, preferred_element_type=F32)


def _dot_nt(a, b):
    return lax.dot_general(a, b, (((1,), (1,)), ((), ())), preferred_element_type=F32)


def _dot_tn(a, b):
    return lax.dot_general(a, b, (((0,), (0,)), ((), ())), preferred_element_type=F32)


def _sigmoid(x):
    return 1.0 / (1.0 + jnp.exp(-x))


def _sigmoid_pair(x):
    e = jnp.exp(-jnp.abs(x))
    s = 1.0 / (1.0 + e)
    t = e * s
    pos = x >= 0.0
    return jnp.where(pos, s, t), jnp.where(pos, t, s)


def _silu(x):
    return x * _sigmoid(x)


def _log_sigmoid(x):
    return jnp.minimum(x, 0.0) - jnp.log1p(jnp.exp(-jnp.abs(x)))


def _split3(x):
    hi = x.astype(BF16)
    r = x - hi.astype(F32)
    mid = r.astype(BF16)
    lo = (r - mid.astype(F32)).astype(BF16)
    return hi, mid, lo


def _block_tri(n):
    r = lax.broadcasted_iota(jnp.int32, (n, n), 0)
    c = lax.broadcasted_iota(jnp.int32, (n, n), 1)
    shift = CHUNK.bit_length() - 1
    keep = (c <= r) & (jnp.right_shift(r, shift) == jnp.right_shift(c, shift))
    return jnp.where(keep, 1.0, 0.0).astype(BF16)


def _lower_bound(lbp):
    rows = [lbp[i:i + 1, :] for i in range(lbp.shape[0])]
    m = rows[0]
    for r in rows[1:]:
        m = jnp.maximum(m, r)
    es = [jnp.exp(r - m) for r in rows]
    tot = es[0]
    for e in es[1:]:
        tot = tot + e
    return es[0] / tot


def _intra_scores(q, k, b, causal):
    dk = q.shape[1]
    rows = []
    for i in range(CHUNK // SUB):
        lo, hi = i * SUB, (i + 1) * SUB
        bs = b[lo:lo + 1, :]
        qi = (q[lo:hi] * jnp.exp(b[lo:hi] - bs)).astype(BF16)
        kj = (k[0:hi] * jnp.exp(bs - b[0:hi])).astype(BF16)
        kj = jnp.concatenate([kj, jnp.zeros((LANES - hi, dk), BF16)], axis=0)
        rows.append(_dot_nt(qi, kj))
    a = jnp.concatenate(rows, axis=0)
    return jnp.where(causal, a, 0.0)


def _columns(rows):
    r = rows.shape[0]
    if r < LANES:
        rows = jnp.concatenate([rows, jnp.zeros((LANES - r, LANES), F32)], axis=0)
    return rows.T


def _head_rmsnorm(o, gain, dv):
    outs = []
    for j in range(o.shape[1] // dv):
        oj = o[:, j * dv:(j + 1) * dv]
        ms = jnp.mean(oj * oj, axis=-1, keepdims=True)
        outs.append(oj * lax.rsqrt(ms + NORM_EPS))
    on = outs[0] if len(outs) == 1 else jnp.concatenate(outs, axis=1)
    return on * gain


def _layernorm(z, g, b):
    mu = jnp.mean(z, axis=-1, keepdims=True)
    zc = z - mu
    var = jnp.mean(zc * zc, axis=-1, keepdims=True)
    return zc * lax.rsqrt(var + NORM_EPS) * g + b


class _Task(NamedTuple):
    cost: float
    run: Any


def _merge(*lists):
    totals = [sum(t.cost for t in l) or 1.0 for l in lists]
    pos = [0] * len(lists)
    done = [0.0] * len(lists)
    out = []
    for _ in range(sum(len(l) for l in lists)):
        live = [i for i in range(len(lists)) if pos[i] < len(lists[i])]
        j = min(live, key=lambda i: (done[i] + 0.5 * lists[i][pos[i]].cost) / totals[i])
        out.append(lists[j][pos[j]])
        done[j] += lists[j][pos[j]].cost
        pos[j] += 1
    return out


def _run(tasks):
    for t in tasks:
        t.run()


def _row_blocks(nrows, step=ROW_BLOCK):
    return [slice(r, r + step) for r in range(0, nrows, step)]


class _Branch(NamedTuple):
    heads: int
    dk: int
    dv: int
    scale: float
    q: Any
    k: Any
    b: Any
    o: Any
    state: Any


def _slab(buf, hc, dv):
    per = buf.shape[2] // dv
    return hc // per, slice((hc % per) * dv, (hc % per + 1) * dv)


def _store_decay(b_s, n, bt):
    per = COL_TILE // LANES
    for j in range(per):
        b_s[n * per + j] = bt[:, j * LANES:(j + 1) * LANES]


def _store_values(w_s, n, vt, dk, dv, nchunks):
    per = COL_TILE // dv
    for j in range(per):
        for ci in range(nchunks):
            i, cs = _slab(w_s, (n * per + j) * nchunks + ci, dv)
            w_s[i, dk:dk + CHUNK, cs] = vt[ci * CHUNK:(ci + 1) * CHUNK, j * dv:(j + 1) * dv]


def _recurrence_tasks(br, qa_s, kd_s, w_s, u_s, nchunks):
    nheads, dk, dv = br.heads, br.dk, br.dv
    r = lax.broadcasted_iota(jnp.int32, (CHUNK, LANES), 0)
    c = lax.broadcasted_iota(jnp.int32, (CHUNK, LANES), 1)
    causal = c <= r
    pairs = [(ci, h) for ci in range(nchunks) for h in range(nheads)]
    rows_of = lambda ci: slice(ci * CHUNK, (ci + 1) * CHUNK)
    cols = []

    def operands(ci, h):
        rows, kcols = rows_of(ci), slice(h * dk, (h + 1) * dk)
        q, k, b = br.q[rows, kcols], br.k[rows, kcols], br.b[h, rows, :]
        if br.scale != 1.0:
            q = q * br.scale
        qa_s[h, rows, 0:dk] = (q * jnp.exp(b)).astype(BF16)
        kd_s[h, rows, :] = (k * jnp.exp(b[CHUNK - 1:CHUNK, :] - b)).astype(BF16)
        qa_s[h, rows, dk:dk + LANES] = _intra_scores(q, k, b, causal).astype(BF16)

    def outer(ci, h):
        i, cs = _slab(w_s, h * nchunks + ci, dv)
        u_s[i, :, cs] = _dot_tn(kd_s[h, rows_of(ci), :], w_s[i, dk:dk + CHUNK, cs])

    def decay_columns():
        half = CHUNK // 2
        for h in range(nheads):
            cols.append(_columns(jnp.exp(br.b[h, pl.ds(half - 1, 2 * nchunks, stride=half), :])))

    def chain(ci, h):
        i, cs = _slab(w_s, h * nchunks + ci, dv)
        s = br.state[0, 0, h]
        w_s[i, 0:dk, cs] = s.astype(BF16)
        br.state[0, 0, h] = s * cols[h][:, 2 * ci + 1:2 * ci + 2] + u_s[i, :, cs]

    def output(ci, h):
        i, cs = _slab(w_s, h * nchunks + ci, dv)
        br.o[rows_of(ci), h * dv:(h + 1) * dv] = _dot(qa_s[h, rows_of(ci), :], w_s[i, :, cs])

    bind = lambda cost, fn: [_Task(cost, lambda ci=ci, h=h: fn(ci, h)) for ci, h in pairs]
    return bind(70, operands), (bind(32, outer) + [_Task(16 * nheads, decay_columns)]
                                + bind(8 + dv // 8, chain) + bind(16, output))


def _prompt_kernel(x_ref, wa_ref, wga_ref, wglr_ref, bg_ref, gng_ref, wbg_ref,
                   wb_ref, lbp_ref, hng_ref, wbh_ref, wmg_ref, wout_ref, lng_ref, lnb_ref,
                   y_ref, sg_ref, sh_ref,
                   xb_s, tri_s, ga_s, qg_s, kg_s, bgl_s, qh_s, kh_s, bh_s, h3_s, qa_s, kd_s, w_s, u_s,
                   og_s, oh_s, sgr_s, shr_s, mga_s, mgb_s, g_s, m_s, p_s):
    tb = TOK_BLOCK
    nch = tb // CHUNK
    tiles = lambda width: range(width // COL_TILE)
    col = lambda n, base=0: slice(base + n * COL_TILE, base + (n + 1) * COL_TILE)
    gla = _Branch(GLA_H, GLA_DK, GLA_DV, GLA_DK ** -0.5, qg_s, kg_s, bgl_s, og_s, sg_ref)
    hgrn = _Branch(HG_H, HG_DK, HG_DV, 1.0, qh_s, kh_s, bh_s, oh_s, sh_ref)
    mm_cost = 256.0

    @pl.when(pl.program_id(1) == 0)
    def _():
        sg_ref[...] = jnp.zeros_like(sg_ref)
        sh_ref[...] = jnp.zeros_like(sh_ref)
        tri_s[...] = _block_tri(tb)
        w_s[:, LANES + CHUNK:, :] = jnp.zeros((w_s.shape[0], LANES - CHUNK, w_s.shape[2]), BF16)

    xb_s[...] = x_ref[0].astype(BF16)
    lb = _lower_bound(lbp_ref[...])

    def mm(dst, w_ref, wcols, lhs=xb_s, cast=None, cost=mm_cost):
        def run():
            r = _dot(lhs[...], w_ref[:, wcols])
            dst(r if cast is None else r.astype(cast))
        return _Task(cost, run)

    def store(ref, cols):
        def put(v):
            ref[:, cols] = v
        return put

    def ew(cost, fn, nrows=tb, step=ROW_BLOCK):
        def run():
            for rb in _row_blocks(nrows, step):
                fn(rb)
        return _Task(cost, run)

    def cumsum_task(b_s, n):
        def run():
            t = tri_s[...]
            _store_decay(b_s, n, _dot(t, h3_s[0, :, col(n)]) + _dot(t, h3_s[1, :, col(n)]) + _dot(t, h3_s[2, :, col(n)]))
        return _Task(192.0, run)

    def put_split3(rb, cs, g):
        hi, mid, lo = _split3(g)
        h3_s[0, rb, cs] = hi
        h3_s[1, rb, cs] = mid
        h3_s[2, rb, cs] = lo

    def gla_logit(n):
        def run():
            qg_s[:, col(n)] = _dot(ga_s[...], wglr_ref[:, col(n)]) + bg_ref[:, col(n)]
        return _Task(64.0, run)

    def gla_log(n):
        def fn(rb):
            put_split3(rb, col(n), _log_sigmoid(qg_s[rb, col(n)]) * (1.0 / GLA_TAU))
        return ew(300.0, fn)

    gla_in = [mm(store(ga_s, slice(None)), wga_ref, slice(None), cast=BF16, cost=128.0)]
    gla_in += [mm(store(kg_s, col(n)), wa_ref, col(n, GLA_K)) for n in tiles(GLA_K)]
    gla_in += [gla_logit(n) for n in tiles(GLA_K)]
    gla_in += [t for n in tiles(GLA_K) for t in (
        mm(lambda v, n=n: _store_values(w_s, n, v, GLA_DK, GLA_DV, nch), wa_ref, col(n, 2 * GLA_K), cast=BF16),
        gla_log(n),
        mm(lambda v, n=n: _store_values(w_s, n + GLA_K // COL_TILE, v, GLA_DK, GLA_DV, nch), wa_ref,
           col(n + GLA_K // COL_TILE, 2 * GLA_K), cast=BF16))]
    gla_in += [cumsum_task(bgl_s, n) for n in tiles(GLA_K)]
    gla_in += [mm(store(qg_s, col(n)), wa_ref, col(n)) for n in tiles(GLA_K)]
    _run(gla_in)

    def hq_ew(n):
        def fn(rb):
            qh_s[rb, col(n)] = _silu(qh_s[rb, col(n)])
        return ew(130.0, fn)

    def hf_ew(n):
        def fn(rb):
            sf, sfn = _sigmoid_pair(kh_s[rb, col(n)])
            lbn = lb[:, col(n)]
            kh_s[rb, col(n)] = (1.0 - lbn) * sfn
            put_split3(rb, col(n), jnp.log(lbn + (1.0 - lbn) * sf))
        return ew(450.0, fn)

    def gate_ew(ref, act, n):
        def fn(rb):
            ref[rb, col(n)] = act(ref[rb, col(n)])
        return ew(130.0, fn)

    hg_in = []
    for n in tiles(HG_K):
        hg_in += [mm(store(kh_s, col(n)), wb_ref, col(n, HG_K)),
                  mm(store(qh_s, col(n)), wb_ref, col(n)),
                  hf_ew(n), hq_ew(n), cumsum_task(bh_s, n)]
    gates = []
    for n in tiles(GLA_V):
        gates += [mm(store(sgr_s, col(n)), wa_ref, col(n, 2 * GLA_K + GLA_V)), gate_ew(sgr_s, _silu, n)]
    for n in tiles(D_MODEL):
        gates += [mm(store(mga_s, col(n)), wmg_ref, col(n)), gate_ew(mga_s, _sigmoid, n)]
    gates_b = []
    for n in tiles(HG_V):
        gates_b += [mm(store(shr_s, col(n)), wb_ref, col(n, 2 * HG_K + HG_V)), gate_ew(shr_s, _silu, n)]
    hg_v = [mm(lambda v, n=n: _store_values(w_s, n, v, HG_DK, HG_DV, nch), wb_ref, col(n, 2 * HG_K), cast=BF16)
            for n in tiles(HG_V)]
    mgb = []
    for n in tiles(D_MODEL):
        mgb += [mm(store(mgb_s, col(n)), wmg_ref, col(n, D_MODEL)), gate_ew(mgb_s, _sigmoid, n)]

    ra_ops, ra_rest = _recurrence_tasks(gla, qa_s, kd_s, w_s, u_s, nch)
    _run(_merge(ra_ops + ra_rest, hg_in + gates))

    def gla_norm(n):
        def fn(rb):
            g_s[rb, col(n)] = (_head_rmsnorm(og_s[rb, col(n)], gng_ref[:, col(n)], GLA_DV) * sgr_s[rb, col(n)]).astype(BF16)
        return ew(200.0, fn)

    def gla_scale(n):
        def fn(rb):
            p_s[rb, col(n)] = p_s[rb, col(n)] * mga_s[rb, col(n)]
        return ew(50.0, fn)

    gla_post = [gla_norm(n) for n in tiles(GLA_V)]
    for n in tiles(D_MODEL):
        gla_post += [mm(store(p_s, col(n)), wbg_ref, col(n), lhs=g_s), gla_scale(n)]
    rb_ops, rb_rest = _recurrence_tasks(hgrn, qa_s, kd_s, w_s, u_s, nch)
    _run(_merge(rb_ops, hg_v + gates_b))
    _run(_merge(rb_rest, gla_post))

    def hgrn_norm(n):
        def fn(rb):
            m_s[rb, col(n)] = (_head_rmsnorm(oh_s[rb, col(n)], hng_ref[:, col(n)], HG_DV) * shr_s[rb, col(n)]).astype(BF16)
        return ew(220.0, fn)

    def merge_ew(n):
        def fn(rb):
            g_s[rb, col(n)] = (p_s[rb, col(n)] + mgb_s[rb, col(n)] * sgr_s[rb, col(n)]).astype(BF16)
        return ew(60.0, fn)

    def final_ew():
        def fn(rb):
            z = DEEPNORM_ALPHA * x_ref[0, rb, :] + shr_s[rb, :]
            y_ref[0, rb, :] = _layernorm(z, lng_ref[...], lnb_ref[...])
        return ew(700.0, fn, step=LN_ROW_BLOCK)

    _run(_merge([hgrn_norm(n) for n in tiles(HG_V)], mgb))
    _run([t for n in tiles(D_MODEL) for t in (mm(store(sgr_s, col(n)), wbh_ref, col(n), lhs=m_s),)])
    _run([merge_ew(n) for n in tiles(D_MODEL)])
    _run([mm(store(shr_s, col(n)), wout_ref, col(n), lhs=g_s) for n in tiles(D_MODEL)])
    _run([final_ew()])


def _whole(shape):
    return pl.BlockSpec(memory_space=pltpu.VMEM)


def _prompt_call(x, wa, wga, wglr, bg, gng, wbg, wb, lbp, hng, wbh, wmg, wout, lng, lnb):
    bsz, seq, d = x.shape
    tb = TOK_BLOCK
    nch = tb // CHUNK
    weights = (wa, wga, wglr, bg, gng, wbg, wb, lbp, hng, wbh, wmg, wout, lng, lnb)
    return pl.pallas_call(
        _prompt_kernel,
        grid=(bsz, seq // tb),
        in_specs=[pl.BlockSpec((1, tb, d), lambda b, t: (b, t, 0))] + [_whole(w.shape) for w in weights],
        out_specs=[
            pl.BlockSpec((1, tb, d), lambda b, t: (b, t, 0)),
            pl.BlockSpec((1, 1, GLA_H, GLA_DK, GLA_DV), lambda b, t: (0, b, 0, 0, 0)),
            pl.BlockSpec((1, 1, HG_H, HG_DK, HG_DV), lambda b, t: (0, b, 0, 0, 0)),
        ],
        out_shape=[
            jax.ShapeDtypeStruct((bsz, seq, d), F32),
            jax.ShapeDtypeStruct((DEPTH, bsz, GLA_H, GLA_DK, GLA_DV), F32),
            jax.ShapeDtypeStruct((DEPTH, bsz, HG_H, HG_DK, HG_DV), F32),
        ],
        scratch_shapes=[
            pltpu.VMEM((tb, d), BF16),
            pltpu.VMEM((tb, tb), BF16),
            pltpu.VMEM((tb, LANES), BF16),
            pltpu.VMEM((tb, GLA_K), F32),
            pltpu.VMEM((tb, GLA_K), F32),
            pltpu.VMEM((GLA_H, tb, LANES), F32),
            pltpu.VMEM((tb, HG_K), F32),
            pltpu.VMEM((tb, HG_K), F32),
            pltpu.VMEM((HG_H, tb, LANES), F32),
            pltpu.VMEM((3, tb, HG_K), BF16),
            pltpu.VMEM((HG_H, tb, 2 * LANES), BF16),
            pltpu.VMEM((HG_H, tb, LANES), BF16),
            pltpu.VMEM((GLA_H * nch, 2 * LANES, GLA_DV), BF16),
            pltpu.VMEM((GLA_H * nch, GLA_DK, GLA_DV), F32),
            pltpu.VMEM((tb, GLA_V), F32),
            pltpu.VMEM((tb, HG_V), F32),
            pltpu.VMEM((tb, GLA_V), F32),
            pltpu.VMEM((tb, HG_V), F32),
            pltpu.VMEM((tb, d), F32),
            pltpu.VMEM((tb, d), F32),
            pltpu.VMEM((tb, d), BF16),
            pltpu.VMEM((tb, d), BF16),
            pltpu.VMEM((tb, d), F32),
        ],
        compiler_params=pltpu.CompilerParams(
            dimension_semantics=("arbitrary", "arbitrary"), vmem_limit_bytes=VMEM_LIMIT),
        name="prompt_layer",
    )(x, *weights)


def _sample_proj_kernel(x_ref, wa_ref, wga_ref, wglr_ref, bg_ref, wb_ref, lbp_ref, wmg_ref,
                        eg_ref, kg_ref, qg_ref, vg_ref, eh_ref, kh_ref, qh_ref, vh_ref,
                        sgr_ref, shr_ref, mga_ref, mgb_ref):
    xb = x_ref[...].astype(BF16)
    scale = GLA_DK ** -0.5
    qg_ref[...] = _dot(xb, wa_ref[:, 0:GLA_K]) * scale
    kg_ref[...] = _dot(xb, wa_ref[:, GLA_K:2 * GLA_K])
    vg_ref[...] = _dot(xb, wa_ref[:, 2 * GLA_K:2 * GLA_K + GLA_V])
    sgr_ref[...] = _silu(_dot(xb, wa_ref[:, 2 * GLA_K + GLA_V:2 * GLA_K + 2 * GLA_V]))
    ga = _dot(xb, wga_ref[...]).astype(BF16)
    a_logit = _dot(ga, wglr_ref[...]) + bg_ref[...]
    eg_ref[...] = jnp.exp(_log_sigmoid(a_logit) * (1.0 / GLA_TAU))
    lb = _lower_bound(lbp_ref[...])
    qh_ref[...] = _silu(_dot(xb, wb_ref[:, 0:HG_K]))
    sf, sfn = _sigmoid_pair(_dot(xb, wb_ref[:, HG_K:2 * HG_K]))
    eh_ref[...] = jnp.exp(jnp.log(lb + (1.0 - lb) * sf))
    kh_ref[...] = (1.0 - lb) * sfn
    vh_ref[...] = _dot(xb, wb_ref[:, 2 * HG_K:2 * HG_K + HG_V])
    shr_ref[...] = _silu(_dot(xb, wb_ref[:, 2 * HG_K + HG_V:2 * HG_K + 2 * HG_V]))
    mga_ref[...] = _sigmoid(_dot(xb, wmg_ref[:, 0:D_MODEL]))
    mgb_ref[...] = _sigmoid(_dot(xb, wmg_ref[:, D_MODEL:2 * D_MODEL]))


def _sample_proj_call(x, wa, wga, wglr, bg, wb, lbp, wmg):
    n = x.shape[0]
    widths = (GLA_K, GLA_K, GLA_K, GLA_V, HG_K, HG_K, HG_K, HG_V, GLA_V, HG_V, D_MODEL, D_MODEL)
    args = (x, wa, wga, wglr, bg, wb, lbp, wmg)
    return pl.pallas_call(
        _sample_proj_kernel,
        in_specs=[_whole(a.shape) for a in args],
        out_specs=[_whole((n, w)) for w in widths],
        out_shape=[jax.ShapeDtypeStruct((n, w), F32) for w in widths],
        compiler_params=pltpu.CompilerParams(vmem_limit_bytes=VMEM_LIMIT),
        name="sample_proj",
    )(*args)


def _state_update(e, k, q, v, s_ref, snew_ref, o_ref, nheads, dk, dv):
    for h in range(nheads):
        ks = slice(h * dk, (h + 1) * dk)
        ec, kc, qc = (_columns(a[:, ks]) for a in (e, k, q))
        for j in range(SEQ_BLOCK):
            s_new = s_ref[j, h] * ec[:, j:j + 1] + kc[:, j:j + 1] * v[j:j + 1, h * dv:(h + 1) * dv]
            snew_ref[j, h] = s_new
            o_ref[j:j + 1, h * dv:(h + 1) * dv] = jnp.sum(s_new * qc[:, j:j + 1], axis=0, keepdims=True)


def _sample_state_kernel(eg_ref, kg_ref, qg_ref, vg_ref, eh_ref, kh_ref, qh_ref, vh_ref, sg_ref, sh_ref,
                         sgn_ref, shn_ref, oa_ref, ob_ref):
    _state_update(eg_ref[...], kg_ref[...], qg_ref[...], vg_ref[...], sg_ref, sgn_ref, oa_ref,
                  GLA_H, GLA_DK, GLA_DV)
    _state_update(eh_ref[...], kh_ref[...], qh_ref[...], vh_ref[...], sh_ref, shn_ref, ob_ref,
                  HG_H, HG_DK, HG_DV)


def _sample_state_call(eg, kg, qg, vg, eh, kh, qh, vh, sg, sh):
    n = sg.shape[0]
    sb = SEQ_BLOCK
    row = lambda w: pl.BlockSpec((sb, w), lambda i: (i, 0))
    gspec = pl.BlockSpec((sb, GLA_H, GLA_DK, GLA_DV), lambda i: (i, 0, 0, 0))
    hspec = pl.BlockSpec((sb, HG_H, HG_DK, HG_DV), lambda i: (i, 0, 0, 0))
    return pl.pallas_call(
        _sample_state_kernel,
        grid=(n // sb,),
        in_specs=[row(GLA_K), row(GLA_K), row(GLA_K), row(GLA_V), row(HG_K), row(HG_K), row(HG_K), row(HG_V),
                  gspec, hspec],
        out_specs=[gspec, hspec, row(GLA_V), row(HG_V)],
        out_shape=[jax.ShapeDtypeStruct(sg.shape, F32), jax.ShapeDtypeStruct(sh.shape, F32),
                   jax.ShapeDtypeStruct((n, GLA_V), F32), jax.ShapeDtypeStruct((n, HG_V), F32)],
        compiler_params=pltpu.CompilerParams(dimension_semantics=("arbitrary",), vmem_limit_bytes=VMEM_LIMIT),
        name="sample_state",
    )(eg, kg, qg, vg, eh, kh, qh, vh, sg, sh)


def _sample_post_kernel(x_ref, oa_ref, ob_ref, sgr_ref, shr_ref, mga_ref, mgb_ref,
                        gng_ref, wbg_ref, hng_ref, wbh_ref, wout_ref, lng_ref, lnb_ref, y_ref):
    ga = (_head_rmsnorm(oa_ref[...], gng_ref[...], GLA_DV) * sgr_ref[...]).astype(BF16)
    gb = (_head_rmsnorm(ob_ref[...], hng_ref[...], HG_DV) * shr_ref[...]).astype(BF16)
    merged = mga_ref[...] * _dot(ga, wbg_ref[...]) + mgb_ref[...] * _dot(gb, wbh_ref[...])
    z = DEEPNORM_ALPHA * x_ref[...] + _dot(merged.astype(BF16), wout_ref[...])
    y_ref[...] = _layernorm(z, lng_ref[...], lnb_ref[...])


def _sample_post_call(*args):
    n = args[0].shape[0]
    return pl.pallas_call(
        _sample_post_kernel,
        in_specs=[_whole(a.shape) for a in args],
        out_specs=_whole((n, D_MODEL)),
        out_shape=jax.ShapeDtypeStruct((n, D_MODEL), F32),
        compiler_params=pltpu.CompilerParams(vmem_limit_bytes=VMEM_LIMIT),
        name="sample_post",
    )(*args)


def kernel(x_prompt, x_sample, state_gla, state_hgrn, w_in, w_gate_lr, b_gate_lr, gla_norm_g, w_br_gla,
           hgrn_lb_param, hgrn_norm_g, w_br_hgrn, w_out, ln_g, ln_b):
    assert w_in.shape[0] == DEPTH and x_sample.shape[1] == 1
    w = w_in[0]
    c_ga = 2 * GLA_K + 2 * GLA_V
    c_hg = c_ga + GLA_RANK
    c_mg = c_hg + 2 * HG_K + 2 * HG_V
    wa = w[:, 0:c_ga].astype(BF16)
    wga = jnp.pad(w[:, c_ga:c_hg], ((0, 0), (0, LANES - GLA_RANK))).astype(BF16)
    wglr = jnp.pad(w_gate_lr[0], ((0, LANES - GLA_RANK), (0, 0))).astype(BF16)
    wb = w[:, c_hg:c_mg].astype(BF16)
    wmg = w[:, c_mg:].astype(BF16)
    bg = b_gate_lr[0].reshape(1, GLA_K)
    gng = gla_norm_g[0].reshape(1, GLA_V)
    hng = hgrn_norm_g[0].reshape(1, HG_V)
    wbg = w_br_gla[0].astype(BF16)
    wbh = w_br_hgrn[0].astype(BF16)
    wout = w_out[0].astype(BF16)
    lng = ln_g[0].reshape(1, D_MODEL)
    lnb = ln_b[0].reshape(1, D_MODEL)
    lbp = hgrn_lb_param

    y_prompt, gla_p, hgrn_p = _prompt_call(x_prompt, wa, wga, wglr, bg, gng, wbg, wb, lbp, hng, wbh, wmg, wout,
                                           lng, lnb)

    xs = x_sample[:, 0, :]
    eg, kg, qg, vg, eh, kh, qh, vh, sgr, shr, mga, mgb = _sample_proj_call(xs, wa, wga, wglr, bg, wb, lbp, wmg)
    gla_s, hgrn_s, oa, ob = _sample_state_call(eg, kg, qg, vg, eh, kh, qh, vh, state_gla[0], state_hgrn[0])
    ys = _sample_post_call(xs, oa, ob, sgr, shr, mga, mgb, gng, wbg, hng, wbh, wout, lng, lnb)
    return (y_prompt, ys[:, None, :], gla_p, hgrn_p, gla_s[None], hgrn_s[None])
```

```python
from typing import Any, NamedTuple

import jax
import jax.numpy as jnp
from jax import lax
from jax.experimental import pallas as pl
from jax.experimental.pallas import tpu as pltpu

F32 = jnp.float32
BF16 = jnp.bfloat16

D_MODEL = 1024
GLA_H, GLA_DK, GLA_DV = 4, 128, 256
HG_H, HG_DK, HG_DV = 8, 128, 128
GLA_RANK = 16
GLA_TAU = 16.0
GLA_K = GLA_H * GLA_DK
GLA_V = GLA_H * GLA_DV
HG_K = HG_H * HG_DK
HG_V = HG_H * HG_DV
C_GQ, C_GK, C_GV, C_GR = 0, GLA_K, 2 * GLA_K, 2 * GLA_K + GLA_V
C_HQ = C_GR + GLA_V
C_HF, C_HI, C_HR = C_HQ + HG_K, C_HQ + 2 * HG_K, C_HQ + 2 * HG_K + HG_V
C_MA = C_HR + HG_V
C_MB = C_MA + D_MODEL
CHUNK = 64
SUB = 16
NORM_EPS = 1e-5
DEPTH = 1
DEEPNORM_ALPHA = (2.0 * DEPTH) ** 0.25
LANES = 128
TOK_BLOCK = 256
COL_TILE = 256
ROW_BLOCK = 64
LN_ROW_BLOCK = 16
SEQ_BLOCK = 8
WEIGHT_PREP_ROWS = 128
VMEM_LIMIT = 60 * 1024 * 1024


def _dot(a, b):
    return jnp.dot(a, b, preferred_element_type=F32)


def _dot_nt(a, b):
    return lax.dot_general(a, b, (((1,), (1,)), ((), ())), preferred_element_type=F32)


def _dot_tn(a, b):
    return lax.dot_general(a, b, (((0,), (0,)), ((), ())), preferred_element_type=F32)


def _sigmoid(x):
    return 1.0 / (1.0 + jnp.exp(-x))


def _sigmoid_pair(x):
    e = jnp.exp(-jnp.abs(x))
    s = 1.0 / (1.0 + e)
    t = e * s
    pos = x >= 0.0
    return jnp.where(pos, s, t), jnp.where(pos, t, s)


def _silu(x):
    return x * _sigmoid(x)


def _log_sigmoid(x):
    return jnp.minimum(x, 0.0) - jnp.log1p(jnp.exp(-jnp.abs(x)))


def _split3(x):
    hi = x.astype(BF16)
    r = x - hi.astype(F32)
    mid = r.astype(BF16)
    lo = (r - mid.astype(F32)).astype(BF16)
    return hi, mid, lo


def _block_tri(n):
    r = lax.broadcasted_iota(jnp.int32, (n, n), 0)
    c = lax.broadcasted_iota(jnp.int32, (n, n), 1)
    shift = CHUNK.bit_length() - 1
    keep = (c <= r) & (jnp.right_shift(r, shift) == jnp.right_shift(c, shift))
    return jnp.where(keep, 1.0, 0.0).astype(BF16)


def _lower_bound(lbp):
    rows = [lbp[i:i + 1, :] for i in range(lbp.shape[0])]
    m = rows[0]
    for r in rows[1:]:
        m = jnp.maximum(m, r)
    es = [jnp.exp(r - m) for r in rows]
    tot = es[0]
    for e in es[1:]:
        tot = tot + e
    return es[0] / tot


def _intra_scores(q, k, b, causal):
    dk = q.shape[1]
    rows = []
    for i in range(CHUNK // SUB):
        lo, hi = i * SUB, (i + 1) * SUB
        bs = b[lo:lo + 1, :]
        qi = (q[lo:hi] * jnp.exp(b[lo:hi] - bs)).astype(BF16)
        kj = (k[0:hi] * jnp.exp(bs - b[0:hi])).astype(BF16)
        kj = jnp.concatenate([kj, jnp.zeros((LANES - hi, dk), BF16)], axis=0)
        rows.append(_dot_nt(qi, kj))
    a = jnp.concatenate(rows, axis=0)
    return jnp.where(causal, a, 0.0)


def _columns(rows):
    r = rows.shape[0]
    if r < LANES:
        rows = jnp.concatenate([rows, jnp.zeros((LANES - r, LANES), F32)], axis=0)
    return rows.T


def _head_rmsnorm(o, gain, dv):
    outs = []
    for j in range(o.shape[1] // dv):
        oj = o[:, j * dv:(j + 1) * dv]
        ms = jnp.mean(oj * oj, axis=-1, keepdims=True)
        outs.append(oj * lax.rsqrt(ms + NORM_EPS))
    on = outs[0] if len(outs) == 1 else jnp.concatenate(outs, axis=1)
    return on * gain


def _layernorm(z, g, b):
    mu = jnp.mean(z, axis=-1, keepdims=True)
    zc = z - mu
    var = jnp.mean(zc * zc, axis=-1, keepdims=True)
    return zc * lax.rsqrt(var + NORM_EPS) * g + b


class _Task(NamedTuple):
    cost: float
    run: Any


def _merge(*lists):
    totals = [sum(t.cost for t in l) or 1.0 for l in lists]
    pos = [0] * len(lists)
    done = [0.0] * len(lists)
    out = []
    for _ in range(sum(len(l) for l in lists)):
        live = [i for i in range(len(lists)) if pos[i] < len(lists[i])]
        j = min(live, key=lambda i: (done[i] + 0.5 * lists[i][pos[i]].cost) / totals[i])
        out.append(lists[j][pos[j]])
        done[j] += lists[j][pos[j]].cost
        pos[j] += 1
    return out


def _run(tasks):
    for t in tasks:
        t.run()


def _row_blocks(nrows, step=ROW_BLOCK):
    return [slice(r, r + step) for r in range(0, nrows, step)]


class _Branch(NamedTuple):
    heads: int
    dk: int
    dv: int
    scale: float
    q: Any
    k: Any
    b: Any
    o: Any
    state: Any


def _slab(buf, hc, dv):
    per = buf.shape[2] // dv
    return hc // per, slice((hc % per) * dv, (hc % per + 1) * dv)


def _store_decay(b_s, n, bt):
    per = COL_TILE // LANES
    for j in range(per):
        b_s[n * per + j] = bt[:, j * LANES:(j + 1) * LANES]


def _store_values(w_s, n, vt, dk, dv, nchunks):
    per = COL_TILE // dv
    for j in range(per):
        for ci in range(nchunks):
            i, cs = _slab(w_s, (n * per + j) * nchunks + ci, dv)
            w_s[i, dk:dk + CHUNK, cs] = vt[ci * CHUNK:(ci + 1) * CHUNK, j * dv:(j + 1) * dv]


def _recurrence_tasks(br, qa_s, kd_s, w_s, u_s, nchunks):
    nheads, dk, dv = br.heads, br.dk, br.dv
    r = lax.broadcasted_iota(jnp.int32, (CHUNK, LANES), 0)
    c = lax.broadcasted_iota(jnp.int32, (CHUNK, LANES), 1)
    causal = c <= r
    pairs = [(ci, h) for ci in range(nchunks) for h in range(nheads)]
    rows_of = lambda ci: slice(ci * CHUNK, (ci + 1) * CHUNK)
    cols = []

    def operands(ci, h):
        rows, kcols = rows_of(ci), slice(h * dk, (h + 1) * dk)
        q, k, b = br.q[rows, kcols], br.k[rows, kcols], br.b[h, rows, :]
        if br.scale != 1.0:
            q = q * br.scale
        qa_s[h, rows, 0:dk] = (q * jnp.exp(b)).astype(BF16)
        kd_s[h, rows, :] = (k * jnp.exp(b[CHUNK - 1:CHUNK, :] - b)).astype(BF16)
        qa_s[h, rows, dk:dk + LANES] = _intra_scores(q, k, b, causal).astype(BF16)

    def outer(ci, h):
        i, cs = _slab(w_s, h * nchunks + ci, dv)
        u_s[i, :, cs] = _dot_tn(kd_s[h, rows_of(ci), :], w_s[i, dk:dk + CHUNK, cs])

    def decay_columns():
        half = CHUNK // 2
        for h in range(nheads):
            cols.append(_columns(jnp.exp(br.b[h, pl.ds(half - 1, 2 * nchunks, stride=half), :])))

    def chain(ci, h):
        i, cs = _slab(w_s, h * nchunks + ci, dv)
        s = br.state[0, 0, h]
        w_s[i, 0:dk, cs] = s.astype(BF16)
        br.state[0, 0, h] = s * cols[h][:, 2 * ci + 1:2 * ci + 2] + u_s[i, :, cs]

    def output(ci, h):
        i, cs = _slab(w_s, h * nchunks + ci, dv)
        br.o[rows_of(ci), h * dv:(h + 1) * dv] = _dot(qa_s[h, rows_of(ci), :], w_s[i, :, cs])

    bind = lambda cost, fn: [_Task(cost, lambda ci=ci, h=h: fn(ci, h)) for ci, h in pairs]
    return bind(70, operands), (bind(32, outer) + [_Task(16 * nheads, decay_columns)]
                                + bind(8 + dv // 8, chain) + bind(16, output))


def _prompt_kernel(x_ref, win_ref, wga_ref, wglr_ref, bg_ref, gng_ref, wbg_ref,
                   lbp_ref, hng_ref, wbh_ref, wout_ref, lng_ref, lnb_ref,
                   y_ref, sg_ref, sh_ref,
                   xb_s, tri_s, ga_s, qg_s, kg_s, bgl_s, qh_s, kh_s, bh_s, h3_s, qa_s, kd_s, w_s, u_s,
                   og_s, oh_s, sgr_s, shr_s, mga_s, mgb_s, g_s, m_s, p_s):
    tb = TOK_BLOCK
    nch = tb // CHUNK
    tiles = lambda width: range(width // COL_TILE)
    col = lambda n, base=0: slice(base + n * COL_TILE, base + (n + 1) * COL_TILE)
    gla = _Branch(GLA_H, GLA_DK, GLA_DV, GLA_DK ** -0.5, qg_s, kg_s, bgl_s, og_s, sg_ref)
    hgrn = _Branch(HG_H, HG_DK, HG_DV, 1.0, qh_s, kh_s, bh_s, oh_s, sh_ref)
    mm_cost = 256.0

    @pl.when(pl.program_id(1) == 0)
    def _():
        sg_ref[...] = jnp.zeros_like(sg_ref)
        sh_ref[...] = jnp.zeros_like(sh_ref)
        tri_s[...] = _block_tri(tb)
        w_s[:, LANES + CHUNK:, :] = jnp.zeros((w_s.shape[0], LANES - CHUNK, w_s.shape[2]), BF16)

    xb_s[...] = x_ref[0].astype(BF16)
    lb = _lower_bound(lbp_ref[...])

    def mm(dst, w_ref, wcols, lhs=xb_s, cast=None, cost=mm_cost):
        def run():
            r = _dot(lhs[...], w_ref[:, wcols])
            dst(r if cast is None else r.astype(cast))
        return _Task(cost, run)

    def store(ref, cols):
        def put(v):
            ref[:, cols] = v
        return put

    def ew(cost, fn, nrows=tb, step=ROW_BLOCK):
        def run():
            for rb in _row_blocks(nrows, step):
                fn(rb)
        return _Task(cost, run)

    def cumsum_task(b_s, n):
        def run():
            t = tri_s[...]
            _store_decay(b_s, n, _dot(t, h3_s[0, :, col(n)]) + _dot(t, h3_s[1, :, col(n)]) + _dot(t, h3_s[2, :, col(n)]))
        return _Task(192.0, run)

    def put_split3(rb, cs, g):
        hi, mid, lo = _split3(g)
        h3_s[0, rb, cs] = hi
        h3_s[1, rb, cs] = mid
        h3_s[2, rb, cs] = lo

    def gla_logit(n):
        def run():
            qg_s[:, col(n)] = _dot(ga_s[...], wglr_ref[:, col(n)]) + bg_ref[:, col(n)]
        return _Task(64.0, run)

    def gla_log(n):
        def fn(rb):
            put_split3(rb, col(n), _log_sigmoid(qg_s[rb, col(n)]) * (1.0 / GLA_TAU))
        return ew(300.0, fn)

    gla_in = [mm(store(ga_s, slice(None)), wga_ref, slice(None), cast=BF16, cost=128.0)]
    gla_in += [mm(store(kg_s, col(n)), win_ref, col(n, C_GK)) for n in tiles(GLA_K)]
    gla_in += [gla_logit(n) for n in tiles(GLA_K)]
    gla_in += [t for n in tiles(GLA_K) for t in (
        mm(lambda v, n=n: _store_values(w_s, n, v, GLA_DK, GLA_DV, nch), win_ref, col(n, C_GV), cast=BF16),
        gla_log(n),
        mm(lambda v, n=n: _store_values(w_s, n + GLA_K // COL_TILE, v, GLA_DK, GLA_DV, nch), win_ref,
           col(n + GLA_K // COL_TILE, C_GV), cast=BF16))]
    gla_in += [cumsum_task(bgl_s, n) for n in tiles(GLA_K)]
    gla_in += [mm(store(qg_s, col(n)), win_ref, col(n, C_GQ)) for n in tiles(GLA_K)]
    _run(gla_in)

    def hq_ew(n):
        def fn(rb):
            qh_s[rb, col(n)] = _silu(qh_s[rb, col(n)])
        return ew(130.0, fn)

    def hf_ew(n):
        def fn(rb):
            sf, sfn = _sigmoid_pair(kh_s[rb, col(n)])
            lbn = lb[:, col(n)]
            kh_s[rb, col(n)] = (1.0 - lbn) * sfn
            put_split3(rb, col(n), jnp.log(lbn + (1.0 - lbn) * sf))
        return ew(450.0, fn)

    def gate_ew(ref, act, n):
        def fn(rb):
            ref[rb, col(n)] = act(ref[rb, col(n)])
        return ew(130.0, fn)

    hg_in = []
    for n in tiles(HG_K):
        hg_in += [mm(store(kh_s, col(n)), win_ref, col(n, C_HF)),
                  mm(store(qh_s, col(n)), win_ref, col(n, C_HQ)),
                  hf_ew(n), hq_ew(n), cumsum_task(bh_s, n)]
    gates = []
    for n in tiles(GLA_V):
        gates += [mm(store(sgr_s, col(n)), win_ref, col(n, C_GR)), gate_ew(sgr_s, _silu, n)]
    for n in tiles(D_MODEL):
        gates += [mm(store(mga_s, col(n)), win_ref, col(n, C_MA)), gate_ew(mga_s, _sigmoid, n)]
    gates_b = []
    for n in tiles(HG_V):
        gates_b += [mm(store(shr_s, col(n)), win_ref, col(n, C_HR)), gate_ew(shr_s, _silu, n)]
    hg_v = [mm(lambda v, n=n: _store_values(w_s, n, v, HG_DK, HG_DV, nch), win_ref, col(n, C_HI), cast=BF16)
            for n in tiles(HG_V)]
    mgb = []
    for n in tiles(D_MODEL):
        mgb += [mm(store(mgb_s, col(n)), win_ref, col(n, C_MB)), gate_ew(mgb_s, _sigmoid, n)]

    ra_ops, ra_rest = _recurrence_tasks(gla, qa_s, kd_s, w_s, u_s, nch)
    _run(_merge(ra_ops + ra_rest, hg_in + gates))

    def gla_norm(n):
        def fn(rb):
            g_s[rb, col(n)] = (_head_rmsnorm(og_s[rb, col(n)], gng_ref[:, col(n)], GLA_DV) * sgr_s[rb, col(n)]).astype(BF16)
        return ew(200.0, fn)

    def gla_scale(n):
        def fn(rb):
            p_s[rb, col(n)] = p_s[rb, col(n)] * mga_s[rb, col(n)]
        return ew(50.0, fn)

    gla_post = [gla_norm(n) for n in tiles(GLA_V)]
    for n in tiles(D_MODEL):
        gla_post += [mm(store(p_s, col(n)), wbg_ref, col(n), lhs=g_s), gla_scale(n)]
    rb_ops, rb_rest = _recurrence_tasks(hgrn, qa_s, kd_s, w_s, u_s, nch)
    _run(_merge(rb_ops, hg_v + gates_b))
    _run(_merge(rb_rest, gla_post))

    def hgrn_norm(n):
        def fn(rb):
            m_s[rb, col(n)] = (_head_rmsnorm(oh_s[rb, col(n)], hng_ref[:, col(n)], HG_DV) * shr_s[rb, col(n)]).astype(BF16)
        return ew(220.0, fn)

    def merge_ew(n):
        def fn(rb):
            g_s[rb, col(n)] = (p_s[rb, col(n)] + mgb_s[rb, col(n)] * sgr_s[rb, col(n)]).astype(BF16)
        return ew(60.0, fn)

    def final_ew():
        def fn(rb):
            z = DEEPNORM_ALPHA * x_ref[0, rb, :] + shr_s[rb, :]
            y_ref[0, rb, :] = _layernorm(z, lng_ref[...], lnb_ref[...])
        return ew(700.0, fn, step=LN_ROW_BLOCK)

    _run(_merge([hgrn_norm(n) for n in tiles(HG_V)], mgb))
    _run([t for n in tiles(D_MODEL) for t in (mm(store(sgr_s, col(n)), wbh_ref, col(n), lhs=m_s),)])
    _run([merge_ew(n) for n in tiles(D_MODEL)])
    _run([mm(store(shr_s, col(n)), wout_ref, col(n), lhs=g_s) for n in tiles(D_MODEL)])
    _run([final_ew()])


def _whole(shape):
    return pl.BlockSpec(memory_space=pltpu.VMEM)


def _prompt_call(x, win, wga, wglr, bg, gng, wbg, lbp, hng, wbh, wout, lng, lnb):
    bsz, seq, d = x.shape
    tb = TOK_BLOCK
    nch = tb // CHUNK
    weights = (win, wga, wglr, bg, gng, wbg, lbp, hng, wbh, wout, lng, lnb)
    return pl.pallas_call(
        _prompt_kernel,
        grid=(bsz, seq // tb),
        in_specs=[pl.BlockSpec((1, tb, d), lambda b, t: (b, t, 0))] + [_whole(w.shape) for w in weights],
        out_specs=[
            pl.BlockSpec((1, tb, d), lambda b, t: (b, t, 0)),
            pl.BlockSpec((1, 1, GLA_H, GLA_DK, GLA_DV), lambda b, t: (0, b, 0, 0, 0)),
            pl.BlockSpec((1, 1, HG_H, HG_DK, HG_DV), lambda b, t: (0, b, 0, 0, 0)),
        ],
        out_shape=[
            jax.ShapeDtypeStruct((bsz, seq, d), F32),
            jax.ShapeDtypeStruct((DEPTH, bsz, GLA_H, GLA_DK, GLA_DV), F32),
            jax.ShapeDtypeStruct((DEPTH, bsz, HG_H, HG_DK, HG_DV), F32),
        ],
        scratch_shapes=[
            pltpu.VMEM((tb, d), BF16),
            pltpu.VMEM((tb, tb), BF16),
            pltpu.VMEM((tb, LANES), BF16),
            pltpu.VMEM((tb, GLA_K), F32),
            pltpu.VMEM((tb, GLA_K), F32),
            pltpu.VMEM((GLA_H, tb, LANES), F32),
            pltpu.VMEM((tb, HG_K), F32),
            pltpu.VMEM((tb, HG_K), F32),
            pltpu.VMEM((HG_H, tb, LANES), F32),
            pltpu.VMEM((3, tb, HG_K), BF16),
            pltpu.VMEM((HG_H, tb, 2 * LANES), BF16),
            pltpu.VMEM((HG_H, tb, LANES), BF16),
            pltpu.VMEM((GLA_H * nch, 2 * LANES, GLA_DV), BF16),
            pltpu.VMEM((GLA_H * nch, GLA_DK, GLA_DV), F32),
            pltpu.VMEM((tb, GLA_V), F32),
            pltpu.VMEM((tb, HG_V), F32),
            pltpu.VMEM((tb, GLA_V), F32),
            pltpu.VMEM((tb, HG_V), F32),
            pltpu.VMEM((tb, d), F32),
            pltpu.VMEM((tb, d), F32),
            pltpu.VMEM((tb, d), BF16),
            pltpu.VMEM((tb, d), BF16),
            pltpu.VMEM((tb, d), F32),
        ],
        compiler_params=pltpu.CompilerParams(
            dimension_semantics=("arbitrary", "arbitrary"), vmem_limit_bytes=VMEM_LIMIT),
        name="prompt_layer",
    )(x, *weights)


def _sample_proj_kernel(x_ref, win_ref, wga_ref, wglr_ref, bg_ref, lbp_ref,
                        eg_ref, kg_ref, qg_ref, vg_ref, eh_ref, kh_ref, qh_ref, vh_ref,
                        sgr_ref, shr_ref, mga_ref, mgb_ref):
    xb = x_ref[...].astype(BF16)
    proj = lambda base, width: _dot(xb, win_ref[:, base:base + width])
    qg_ref[...] = proj(C_GQ, GLA_K) * (GLA_DK ** -0.5)
    kg_ref[...] = proj(C_GK, GLA_K)
    vg_ref[...] = proj(C_GV, GLA_V)
    sgr_ref[...] = _silu(proj(C_GR, GLA_V))
    ga = _dot(xb, wga_ref[...]).astype(BF16)
    a_logit = _dot(ga, wglr_ref[...]) + bg_ref[...]
    eg_ref[...] = jnp.exp(_log_sigmoid(a_logit) * (1.0 / GLA_TAU))
    lb = _lower_bound(lbp_ref[...])
    qh_ref[...] = _silu(proj(C_HQ, HG_K))
    sf, sfn = _sigmoid_pair(proj(C_HF, HG_K))
    eh_ref[...] = jnp.exp(jnp.log(lb + (1.0 - lb) * sf))
    kh_ref[...] = (1.0 - lb) * sfn
    vh_ref[...] = proj(C_HI, HG_V)
    shr_ref[...] = _silu(proj(C_HR, HG_V))
    mga_ref[...] = _sigmoid(proj(C_MA, D_MODEL))
    mgb_ref[...] = _sigmoid(proj(C_MB, D_MODEL))


def _sample_proj_call(x, win, wga, wglr, bg, lbp):
    n = x.shape[0]
    widths = (GLA_K, GLA_K, GLA_K, GLA_V, HG_K, HG_K, HG_K, HG_V, GLA_V, HG_V, D_MODEL, D_MODEL)
    args = (x, win, wga, wglr, bg, lbp)
    return pl.pallas_call(
        _sample_proj_kernel,
        in_specs=[_whole(a.shape) for a in args],
        out_specs=[_whole((n, w)) for w in widths],
        out_shape=[jax.ShapeDtypeStruct((n, w), F32) for w in widths],
        compiler_params=pltpu.CompilerParams(vmem_limit_bytes=VMEM_LIMIT),
        name="sample_proj",
    )(*args)


def _state_update(e, k, q, v, s_ref, snew_ref, o_ref, nheads, dk, dv):
    for h in range(nheads):
        ks = slice(h * dk, (h + 1) * dk)
        ec, kc, qc = (_columns(a[:, ks]) for a in (e, k, q))
        for j in range(SEQ_BLOCK):
            s_new = s_ref[j, h] * ec[:, j:j + 1] + kc[:, j:j + 1] * v[j:j + 1, h * dv:(h + 1) * dv]
            snew_ref[j, h] = s_new
            o_ref[j:j + 1, h * dv:(h + 1) * dv] = jnp.sum(s_new * qc[:, j:j + 1], axis=0, keepdims=True)


def _sample_state_kernel(eg_ref, kg_ref, qg_ref, vg_ref, eh_ref, kh_ref, qh_ref, vh_ref, sg_ref, sh_ref,
                         sgn_ref, shn_ref, oa_ref, ob_ref):
    _state_update(eg_ref[...], kg_ref[...], qg_ref[...], vg_ref[...], sg_ref, sgn_ref, oa_ref,
                  GLA_H, GLA_DK, GLA_DV)
    _state_update(eh_ref[...], kh_ref[...], qh_ref[...], vh_ref[...], sh_ref, shn_ref, ob_ref,
                  HG_H, HG_DK, HG_DV)


def _sample_state_call(eg, kg, qg, vg, eh, kh, qh, vh, sg, sh):
    n = sg.shape[0]
    sb = SEQ_BLOCK
    row = lambda w: pl.BlockSpec((sb, w), lambda i: (i, 0))
    gspec = pl.BlockSpec((sb, GLA_H, GLA_DK, GLA_DV), lambda i: (i, 0, 0, 0))
    hspec = pl.BlockSpec((sb, HG_H, HG_DK, HG_DV), lambda i: (i, 0, 0, 0))
    return pl.pallas_call(
        _sample_state_kernel,
        grid=(n // sb,),
        in_specs=[row(GLA_K), row(GLA_K), row(GLA_K), row(GLA_V), row(HG_K), row(HG_K), row(HG_K), row(HG_V),
                  gspec, hspec],
        out_specs=[gspec, hspec, row(GLA_V), row(HG_V)],
        out_shape=[jax.ShapeDtypeStruct(sg.shape, F32), jax.ShapeDtypeStruct(sh.shape, F32),
                   jax.ShapeDtypeStruct((n, GLA_V), F32), jax.ShapeDtypeStruct((n, HG_V), F32)],
        compiler_params=pltpu.CompilerParams(dimension_semantics=("arbitrary",), vmem_limit_bytes=VMEM_LIMIT),
        name="sample_state",
    )(eg, kg, qg, vg, eh, kh, qh, vh, sg, sh)


def _sample_post_kernel(x_ref, oa_ref, ob_ref, sgr_ref, shr_ref, mga_ref, mgb_ref,
                        gng_ref, wbg_ref, hng_ref, wbh_ref, wout_ref, lng_ref, lnb_ref, y_ref):
    ga = (_head_rmsnorm(oa_ref[...], gng_ref[...], GLA_DV) * sgr_ref[...]).astype(BF16)
    gb = (_head_rmsnorm(ob_ref[...], hng_ref[...], HG_DV) * shr_ref[...]).astype(BF16)
    merged = mga_ref[...] * _dot(ga, wbg_ref[...]) + mgb_ref[...] * _dot(gb, wbh_ref[...])
    z = DEEPNORM_ALPHA * x_ref[...] + _dot(merged.astype(BF16), wout_ref[...])
    y_ref[...] = _layernorm(z, lng_ref[...], lnb_ref[...])


def _sample_post_call(*args):
    n = args[0].shape[0]
    return pl.pallas_call(
        _sample_post_kernel,
        in_specs=[_whole(a.shape) for a in args],
        out_specs=_whole((n, D_MODEL)),
        out_shape=jax.ShapeDtypeStruct((n, D_MODEL), F32),
        compiler_params=pltpu.CompilerParams(vmem_limit_bytes=VMEM_LIMIT),
        name="sample_post",
    )(*args)


def _weight_prep_kernel(w_ref, win_ref, wga_ref):
    win_ref[:, 0:C_HQ] = w_ref[0, :, 0:C_HQ].astype(BF16)
    win_ref[:, C_HQ:] = w_ref[0, :, C_HQ + GLA_RANK:].astype(BF16)
    lane = lax.broadcasted_iota(jnp.int32, (w_ref.shape[1], LANES), 1)
    wga_ref[...] = jnp.where(lane < GLA_RANK, w_ref[0, :, C_HQ:C_HQ + LANES], 0.0).astype(BF16)


def _weight_prep_call(w_in):
    _, d, width = w_in.shape
    rows = WEIGHT_PREP_ROWS
    return pl.pallas_call(
        _weight_prep_kernel,
        grid=(d // rows,),
        in_specs=[pl.BlockSpec((1, rows, width), lambda i: (0, i, 0))],
        out_specs=[pl.BlockSpec((rows, width - GLA_RANK), lambda i: (i, 0)),
                   pl.BlockSpec((rows, LANES), lambda i: (i, 0))],
        out_shape=[jax.ShapeDtypeStruct((d, width - GLA_RANK), BF16), jax.ShapeDtypeStruct((d, LANES), BF16)],
        compiler_params=pltpu.CompilerParams(dimension_semantics=("arbitrary",), vmem_limit_bytes=VMEM_LIMIT),
        name="weight_prep",
    )(w_in)


def kernel(x_prompt, x_sample, state_gla, state_hgrn, w_in, w_gate_lr, b_gate_lr, gla_norm_g, w_br_gla,
           hgrn_lb_param, hgrn_norm_g, w_br_hgrn, w_out, ln_g, ln_b):
    assert w_in.shape[0] == DEPTH and x_sample.shape[1] == 1
    win, wga = _weight_prep_call(w_in)
    wglr = jnp.pad(w_gate_lr[0], ((0, LANES - GLA_RANK), (0, 0))).astype(BF16)
    bg = b_gate_lr[0].reshape(1, GLA_K)
    gng = gla_norm_g[0].reshape(1, GLA_V)
    hng = hgrn_norm_g[0].reshape(1, HG_V)
    wbg = w_br_gla[0].astype(BF16)
    wbh = w_br_hgrn[0].astype(BF16)
    wout = w_out[0].astype(BF16)
    lng = ln_g[0].reshape(1, D_MODEL)
    lnb = ln_b[0].reshape(1, D_MODEL)
    lbp = hgrn_lb_param

    y_prompt, gla_p, hgrn_p = _prompt_call(x_prompt, win, wga, wglr, bg, gng, wbg, lbp, hng, wbh, wout, lng, lnb)

    xs = x_sample[:, 0, :]
    eg, kg, qg, vg, eh, kh, qh, vh, sgr, shr, mga, mgb = _sample_proj_call(xs, win, wga, wglr, bg, lbp)
    gla_s, hgrn_s, oa, ob = _sample_state_call(eg, kg, qg, vg, eh, kh, qh, vh, state_gla[0], state_hgrn[0])
    ys = _sample_post_call(xs, oa, ob, sgr, shr, mga, mgb, gng, wbg, hng, wbh, wout, lng, lnb)
    return (y_prompt, ys[:, None, :], gla_p, hgrn_p, gla_s[None], hgrn_s[None])
```

```python
from typing import Any, NamedTuple

import jax
import jax.numpy as jnp
from jax import lax
from jax.experimental import pallas as pl
from jax.experimental.pallas import tpu as pltpu

F32 = jnp.float32
BF16 = jnp.bfloat16

D_MODEL = 1024
GLA_H, GLA_DK, GLA_DV = 4, 128, 256
HG_H, HG_DK, HG_DV = 8, 128, 128
GLA_RANK = 16
GLA_TAU = 16.0
GLA_K = GLA_H * GLA_DK
GLA_V = GLA_H * GLA_DV
HG_K = HG_H * HG_DK
HG_V = HG_H * HG_DV
C_GQ, C_GK, C_GV, C_GR = 0, GLA_K, 2 * GLA_K, 2 * GLA_K + GLA_V
C_HQ = C_GR + GLA_V
C_HF, C_HI, C_HR = C_HQ + HG_K, C_HQ + 2 * HG_K, C_HQ + 2 * HG_K + HG_V
C_MA = C_HR + HG_V
C_MB = C_MA + D_MODEL
CHUNK = 64
SUB = 16
NORM_EPS = 1e-5
DEPTH = 1
DEEPNORM_ALPHA = (2.0 * DEPTH) ** 0.25
LANES = 128
TOK_BLOCK = 256
COL_TILE = 256
ROW_BLOCK = 64
LN_ROW_BLOCK = 16
SEQ_BLOCK = 8
WEIGHT_PREP_ROWS = 128
VMEM_LIMIT = 60 * 1024 * 1024


def _dot(a, b):
    return jnp.dot(a, b, preferred_element_type=F32)


def _dot_nt(a, b):
    return lax.dot_general(a, b, (((1,), (1,)), ((), ())), preferred_element_type=F32)


def _dot_tn(a, b):
    return lax.dot_general(a, b, (((0,), (0,)), ((), ())), preferred_element_type=F32)


def _sigmoid(x):
    return 1.0 / (1.0 + jnp.exp(-x))


def _sigmoid_pair(x):
    e = jnp.exp(-jnp.abs(x))
    s = 1.0 / (1.0 + e)
    t = e * s
    pos = x >= 0.0
    return jnp.where(pos, s, t), jnp.where(pos, t, s)


def _silu(x):
    return x * _sigmoid(x)


def _log_sigmoid(x):
    return jnp.minimum(x, 0.0) - jnp.log1p(jnp.exp(-jnp.abs(x)))


def _split3(x):
    hi = x.astype(BF16)
    r = x - hi.astype(F32)
    mid = r.astype(BF16)
    lo = (r - mid.astype(F32)).astype(BF16)
    return hi, mid, lo


def _block_tri(n):
    r = lax.broadcasted_iota(jnp.int32, (n, n), 0)
    c = lax.broadcasted_iota(jnp.int32, (n, n), 1)
    shift = CHUNK.bit_length() - 1
    keep = (c <= r) & (jnp.right_shift(r, shift) == jnp.right_shift(c, shift))
    return jnp.where(keep, 1.0, 0.0).astype(BF16)


def _lower_bound(lbp):
    rows = [lbp[i:i + 1, :] for i in range(lbp.shape[0])]
    m = rows[0]
    for r in rows[1:]:
        m = jnp.maximum(m, r)
    es = [jnp.exp(r - m) for r in rows]
    tot = es[0]
    for e in es[1:]:
        tot = tot + e
    return es[0] / tot


def _intra_scores(q, k, b, causal):
    dk = q.shape[1]
    rows = []
    for i in range(CHUNK // SUB):
        lo, hi = i * SUB, (i + 1) * SUB
        bs = b[lo:lo + 1, :]
        qi = (q[lo:hi] * jnp.exp(b[lo:hi] - bs)).astype(BF16)
        kj = (k[0:hi] * jnp.exp(bs - b[0:hi])).astype(BF16)
        kj = jnp.concatenate([kj, jnp.zeros((LANES - hi, dk), BF16)], axis=0)
        rows.append(_dot_nt(qi, kj))
    a = jnp.concatenate(rows, axis=0)
    return jnp.where(causal, a, 0.0)


def _columns(rows):
    r = rows.shape[0]
    if r < LANES:
        rows = jnp.concatenate([rows, jnp.zeros((LANES - r, LANES), F32)], axis=0)
    return rows.T


def _head_rmsnorm(o, gain, dv):
    outs = []
    for j in range(o.shape[1] // dv):
        oj = o[:, j * dv:(j + 1) * dv]
        ms = jnp.mean(oj * oj, axis=-1, keepdims=True)
        outs.append(oj * lax.rsqrt(ms + NORM_EPS))
    on = outs[0] if len(outs) == 1 else jnp.concatenate(outs, axis=1)
    return on * gain


def _layernorm(z, g, b):
    mu = jnp.mean(z, axis=-1, keepdims=True)
    zc = z - mu
    var = jnp.mean(zc * zc, axis=-1, keepdims=True)
    return zc * lax.rsqrt(var + NORM_EPS) * g + b


class _Task(NamedTuple):
    cost: float
    run: Any


def _merge(*lists):
    totals = [sum(t.cost for t in l) or 1.0 for l in lists]
    pos = [0] * len(lists)
    done = [0.0] * len(lists)
    out = []
    for _ in range(sum(len(l) for l in lists)):
        live = [i for i in range(len(lists)) if pos[i] < len(lists[i])]
        j = min(live, key=lambda i: (done[i] + 0.5 * lists[i][pos[i]].cost) / totals[i])
        out.append(lists[j][pos[j]])
        done[j] += lists[j][pos[j]].cost
        pos[j] += 1
    return out


def _run(tasks):
    for t in tasks:
        t.run()


def _row_blocks(nrows, step=ROW_BLOCK):
    return [slice(r, r + step) for r in range(0, nrows, step)]


class _Branch(NamedTuple):
    heads: int
    dk: int
    dv: int
    scale: float
    q: Any
    k: Any
    b: Any
    o: Any
    state: Any


def _slab(buf, hc, dv):
    per = buf.shape[2] // dv
    return hc // per, slice((hc % per) * dv, (hc % per + 1) * dv)


def _store_decay(b_s, n, bt):
    per = COL_TILE // LANES
    for j in range(per):
        b_s[n * per + j] = bt[:, j * LANES:(j + 1) * LANES]


def _store_values(w_s, n, vt, dk, dv, nchunks):
    per = COL_TILE // dv
    for j in range(per):
        for ci in range(nchunks):
            i, cs = _slab(w_s, (n * per + j) * nchunks + ci, dv)
            w_s[i, dk:dk + CHUNK, cs] = vt[ci * CHUNK:(ci + 1) * CHUNK, j * dv:(j + 1) * dv]


def _recurrence_tasks(br, qa_s, kd_s, w_s, u_s, nchunks):
    nheads, dk, dv = br.heads, br.dk, br.dv
    r = lax.broadcasted_iota(jnp.int32, (CHUNK, LANES), 0)
    c = lax.broadcasted_iota(jnp.int32, (CHUNK, LANES), 1)
    causal = c <= r
    pairs = [(ci, h) for ci in range(nchunks) for h in range(nheads)]
    rows_of = lambda ci: slice(ci * CHUNK, (ci + 1) * CHUNK)
    cols = []

    def operands(ci, h):
        rows, kcols = rows_of(ci), slice(h * dk, (h + 1) * dk)
        q, k, b = br.q[rows, kcols], br.k[rows, kcols], br.b[h, rows, :]
        if br.scale != 1.0:
            q = q * br.scale
        qa_s[h, rows, 0:dk] = (q * jnp.exp(b)).astype(BF16)
        kd_s[h, rows, :] = (k * jnp.exp(b[CHUNK - 1:CHUNK, :] - b)).astype(BF16)
        qa_s[h, rows, dk:dk + LANES] = _intra_scores(q, k, b, causal).astype(BF16)

    def outer(ci, h):
        i, cs = _slab(w_s, h * nchunks + ci, dv)
        u_s[i, :, cs] = _dot_tn(kd_s[h, rows_of(ci), :], w_s[i, dk:dk + CHUNK, cs])

    def decay_columns():
        half = CHUNK // 2
        for h in range(nheads):
            cols.append(_columns(jnp.exp(br.b[h, pl.ds(half - 1, 2 * nchunks, stride=half), :])))

    def chain(ci, h):
        i, cs = _slab(w_s, h * nchunks + ci, dv)
        s = br.state[0, 0, h]
        w_s[i, 0:dk, cs] = s.astype(BF16)
        br.state[0, 0, h] = s * cols[h][:, 2 * ci + 1:2 * ci + 2] + u_s[i, :, cs]

    def output(ci, h):
        i, cs = _slab(w_s, h * nchunks + ci, dv)
        br.o[rows_of(ci), h * dv:(h + 1) * dv] = _dot(qa_s[h, rows_of(ci), :], w_s[i, :, cs])

    bind = lambda cost, fn: [_Task(cost, lambda ci=ci, h=h: fn(ci, h)) for ci, h in pairs]
    return bind(70, operands), (bind(32, outer) + [_Task(16 * nheads, decay_columns)]
                                + bind(8 + dv // 8, chain) + bind(16, output))


def _prompt_kernel(x_ref, win_ref, wga_ref, wglr_ref, bg_ref, gng_ref, wbg_ref,
                   lbp_ref, hng_ref, wbh_ref, wout_ref, lng_ref, lnb_ref,
                   y_ref, sg_ref, sh_ref,
                   xb_s, tri_s, ga_s, qg_s, kg_s, bgl_s, qh_s, kh_s, bh_s, h3_s, qa_s, kd_s, w_s, u_s,
                   og_s, oh_s, sgr_s, shr_s, mga_s, mgb_s, g_s, m_s, p_s):
    tb = TOK_BLOCK
    nch = tb // CHUNK
    tiles = lambda width: range(width // COL_TILE)
    col = lambda n, base=0: slice(base + n * COL_TILE, base + (n + 1) * COL_TILE)
    gla = _Branch(GLA_H, GLA_DK, GLA_DV, GLA_DK ** -0.5, qg_s, kg_s, bgl_s, og_s, sg_ref)
    hgrn = _Branch(HG_H, HG_DK, HG_DV, 1.0, qh_s, kh_s, bh_s, oh_s, sh_ref)
    mm_cost = 256.0

    @pl.when(pl.program_id(1) == 0)
    def _():
        sg_ref[...] = jnp.zeros_like(sg_ref)
        sh_ref[...] = jnp.zeros_like(sh_ref)
        tri_s[...] = _block_tri(tb)
        w_s[:, LANES + CHUNK:, :] = jnp.zeros((w_s.shape[0], LANES - CHUNK, w_s.shape[2]), BF16)

    xb_s[...] = x_ref[0].astype(BF16)
    lb = _lower_bound(lbp_ref[...])

    def mm(dst, w_ref, wcols, lhs=xb_s, cast=None, cost=mm_cost):
        def run():
            r = _dot(lhs[...], w_ref[:, wcols])
            dst(r if cast is None else r.astype(cast))
        return _Task(cost, run)

    def store(ref, cols):
        def put(v):
            ref[:, cols] = v
        return put

    def ew(cost, fn, nrows=tb, step=ROW_BLOCK):
        def run():
            for rb in _row_blocks(nrows, step):
                fn(rb)
        return _Task(cost, run)

    def cumsum_task(b_s, n):
        def run():
            t = tri_s[...]
            _store_decay(b_s, n, _dot(t, h3_s[0, :, col(n)]) + _dot(t, h3_s[1, :, col(n)]) + _dot(t, h3_s[2, :, col(n)]))
        return _Task(192.0, run)

    def put_split3(rb, cs, g):
        hi, mid, lo = _split3(g)
        h3_s[0, rb, cs] = hi
        h3_s[1, rb, cs] = mid
        h3_s[2, rb, cs] = lo

    def gla_logit(n):
        def run():
            qg_s[:, col(n)] = _dot(ga_s[...], wglr_ref[:, col(n)]) + bg_ref[:, col(n)]
        return _Task(64.0, run)

    def gla_log(n):
        def fn(rb):
            put_split3(rb, col(n), _log_sigmoid(qg_s[rb, col(n)]) * (1.0 / GLA_TAU))
        return ew(300.0, fn)

    gla_in = [mm(store(ga_s, slice(None)), wga_ref, slice(None), cast=BF16, cost=128.0)]
    gla_in += [mm(store(kg_s, col(n)), win_ref, col(n, C_GK)) for n in tiles(GLA_K)]
    gla_in += [gla_logit(n) for n in tiles(GLA_K)]
    gla_in += [t for n in tiles(GLA_K) for t in (
        mm(lambda v, n=n: _store_values(w_s, n, v, GLA_DK, GLA_DV, nch), win_ref, col(n, C_GV), cast=BF16),
        gla_log(n),
        mm(lambda v, n=n: _store_values(w_s, n + GLA_K // COL_TILE, v, GLA_DK, GLA_DV, nch), win_ref,
           col(n + GLA_K // COL_TILE, C_GV), cast=BF16))]
    gla_in += [cumsum_task(bgl_s, n) for n in tiles(GLA_K)]
    gla_in += [mm(store(qg_s, col(n)), win_ref, col(n, C_GQ)) for n in tiles(GLA_K)]
    _run(gla_in)

    def hq_ew(n):
        def fn(rb):
            qh_s[rb, col(n)] = _silu(qh_s[rb, col(n)])
        return ew(130.0, fn)

    def hf_ew(n):
        def fn(rb):
            sf, sfn = _sigmoid_pair(kh_s[rb, col(n)])
            lbn = lb[:, col(n)]
            kh_s[rb, col(n)] = (1.0 - lbn) * sfn
            put_split3(rb, col(n), jnp.log(lbn + (1.0 - lbn) * sf))
        return ew(450.0, fn)

    def gate_ew(ref, act, n):
        def fn(rb):
            ref[rb, col(n)] = act(ref[rb, col(n)])
        return ew(130.0, fn)

    hg_in = []
    for n in tiles(HG_K):
        hg_in += [mm(store(kh_s, col(n)), win_ref, col(n, C_HF)),
                  mm(store(qh_s, col(n)), win_ref, col(n, C_HQ)),
                  hf_ew(n), hq_ew(n), cumsum_task(bh_s, n)]
    gates = []
    for n in tiles(GLA_V):
        gates += [mm(store(sgr_s, col(n)), win_ref, col(n, C_GR)), gate_ew(sgr_s, _silu, n)]
    for n in tiles(D_MODEL):
        gates += [mm(store(mga_s, col(n)), win_ref, col(n, C_MA)), gate_ew(mga_s, _sigmoid, n)]
    gates_b = []
    for n in tiles(HG_V):
        gates_b += [mm(store(shr_s, col(n)), win_ref, col(n, C_HR)), gate_ew(shr_s, _silu, n)]
    hg_v = [mm(lambda v, n=n: _store_values(w_s, n, v, HG_DK, HG_DV, nch), win_ref, col(n, C_HI), cast=BF16)
            for n in tiles(HG_V)]
    mgb = []
    for n in tiles(D_MODEL):
        mgb += [mm(store(mgb_s, col(n)), win_ref, col(n, C_MB)), gate_ew(mgb_s, _sigmoid, n)]

    ra_ops, ra_rest = _recurrence_tasks(gla, qa_s, kd_s, w_s, u_s, nch)
    _run(_merge(ra_ops + ra_rest, hg_in + gates))

    def gla_norm(n):
        def fn(rb):
            g_s[rb, col(n)] = (_head_rmsnorm(og_s[rb, col(n)], gng_ref[:, col(n)], GLA_DV) * sgr_s[rb, col(n)]).astype(BF16)
        return ew(200.0, fn)

    def gla_scale(n):
        def fn(rb):
            p_s[rb, col(n)] = p_s[rb, col(n)] * mga_s[rb, col(n)]
        return ew(50.0, fn)

    gla_post = [gla_norm(n) for n in tiles(GLA_V)]
    for n in tiles(D_MODEL):
        gla_post += [mm(store(p_s, col(n)), wbg_ref, col(n), lhs=g_s), gla_scale(n)]
    rb_ops, rb_rest = _recurrence_tasks(hgrn, qa_s, kd_s, w_s, u_s, nch)
    _run(_merge(rb_ops, hg_v + gates_b))
    _run(_merge(rb_rest, gla_post))

    def hgrn_norm(n):
        def fn(rb):
            m_s[rb, col(n)] = (_head_rmsnorm(oh_s[rb, col(n)], hng_ref[:, col(n)], HG_DV) * shr_s[rb, col(n)]).astype(BF16)
        return ew(220.0, fn)

    def merge_ew(n):
        def fn(rb):
            g_s[rb, col(n)] = (p_s[rb, col(n)] + mgb_s[rb, col(n)] * sgr_s[rb, col(n)]).astype(BF16)
        return ew(60.0, fn)

    def final_ew():
        def fn(rb):
            z = DEEPNORM_ALPHA * x_ref[0, rb, :] + shr_s[rb, :]
            y_ref[0, rb, :] = _layernorm(z, lng_ref[...], lnb_ref[...])
        return ew(700.0, fn, step=LN_ROW_BLOCK)

    _run(_merge([hgrn_norm(n) for n in tiles(HG_V)], mgb))
    _run([t for n in tiles(D_MODEL) for t in (mm(store(sgr_s, col(n)), wbh_ref, col(n), lhs=m_s),)])
    _run([merge_ew(n) for n in tiles(D_MODEL)])
    _run([mm(store(shr_s, col(n)), wout_ref, col(n), lhs=g_s) for n in tiles(D_MODEL)])
    _run([final_ew()])


def _whole(shape):
    return pl.BlockSpec(memory_space=pltpu.VMEM)


def _prompt_call(x, win, wga, wglr, bg, gng, wbg, lbp, hng, wbh, wout, lng, lnb):
    bsz, seq, d = x.shape
    tb = TOK_BLOCK
    nch = tb // CHUNK
    weights = (win, wga, wglr, bg, gng, wbg, lbp, hng, wbh, wout, lng, lnb)
    return pl.pallas_call(
        _prompt_kernel,
        grid=(bsz, seq // tb),
        in_specs=[pl.BlockSpec((1, tb, d), lambda b, t: (b, t, 0))] + [_whole(w.shape) for w in weights],
        out_specs=[
            pl.BlockSpec((1, tb, d), lambda b, t: (b, t, 0)),
            pl.BlockSpec((1, 1, GLA_H, GLA_DK, GLA_DV), lambda b, t: (0, b, 0, 0, 0)),
            pl.BlockSpec((1, 1, HG_H, HG_DK, HG_DV), lambda b, t: (0, b, 0, 0, 0)),
        ],
        out_shape=[
            jax.ShapeDtypeStruct((bsz, seq, d), F32),
            jax.ShapeDtypeStruct((DEPTH, bsz, GLA_H, GLA_DK, GLA_DV), F32),
            jax.ShapeDtypeStruct((DEPTH, bsz, HG_H, HG_DK, HG_DV), F32),
        ],
        scratch_shapes=[
            pltpu.VMEM((tb, d), BF16),
            pltpu.VMEM((tb, tb), BF16),
            pltpu.VMEM((tb, LANES), BF16),
            pltpu.VMEM((tb, GLA_K), F32),
            pltpu.VMEM((tb, GLA_K), F32),
            pltpu.VMEM((GLA_H, tb, LANES), F32),
            pltpu.VMEM((tb, HG_K), F32),
            pltpu.VMEM((tb, HG_K), F32),
            pltpu.VMEM((HG_H, tb, LANES), F32),
            pltpu.VMEM((3, tb, HG_K), BF16),
            pltpu.VMEM((HG_H, tb, 2 * LANES), BF16),
            pltpu.VMEM((HG_H, tb, LANES), BF16),
            pltpu.VMEM((GLA_H * nch, 2 * LANES, GLA_DV), BF16),
            pltpu.VMEM((GLA_H * nch, GLA_DK, GLA_DV), F32),
            pltpu.VMEM((tb, GLA_V), F32),
            pltpu.VMEM((tb, HG_V), F32),
            pltpu.VMEM((tb, GLA_V), F32),
            pltpu.VMEM((tb, HG_V), F32),
            pltpu.VMEM((tb, d), F32),
            pltpu.VMEM((tb, d), F32),
            pltpu.VMEM((tb, d), BF16),
            pltpu.VMEM((tb, d), BF16),
            pltpu.VMEM((tb, d), F32),
        ],
        compiler_params=pltpu.CompilerParams(
            dimension_semantics=("arbitrary", "arbitrary"), vmem_limit_bytes=VMEM_LIMIT),
        name="prompt_layer",
    )(x, *weights)


def _sample_proj_kernel(x_ref, win_ref, wga_ref, wglr_ref, bg_ref, lbp_ref,
                        eg_ref, kg_ref, qg_ref, vg_ref, eh_ref, kh_ref, qh_ref, vh_ref,
                        sgr_ref, shr_ref, mga_ref, mgb_ref):
    xb = x_ref[...].astype(BF16)
    proj = lambda base, width: _dot(xb, win_ref[:, base:base + width])
    qg_ref[...] = proj(C_GQ, GLA_K) * (GLA_DK ** -0.5)
    kg_ref[...] = proj(C_GK, GLA_K)
    vg_ref[...] = proj(C_GV, GLA_V)
    sgr_ref[...] = _silu(proj(C_GR, GLA_V))
    ga = _dot(xb, wga_ref[...]).astype(BF16)
    a_logit = _dot(ga, wglr_ref[...]) + bg_ref[...]
    eg_ref[...] = jnp.exp(_log_sigmoid(a_logit) * (1.0 / GLA_TAU))
    lb = _lower_bound(lbp_ref[...])
    qh_ref[...] = _silu(proj(C_HQ, HG_K))
    sf, sfn = _sigmoid_pair(proj(C_HF, HG_K))
    eh_ref[...] = jnp.exp(jnp.log(lb + (1.0 - lb) * sf))
    kh_ref[...] = (1.0 - lb) * sfn
    vh_ref[...] = proj(C_HI, HG_V)
    shr_ref[...] = _silu(proj(C_HR, HG_V))
    mga_ref[...] = _sigmoid(proj(C_MA, D_MODEL))
    mgb_ref[...] = _sigmoid(proj(C_MB, D_MODEL))


def _sample_proj_call(x, win, wga, wglr, bg, lbp):
    n = x.shape[0]
    widths = (GLA_K, GLA_K, GLA_K, GLA_V, HG_K, HG_K, HG_K, HG_V, GLA_V, HG_V, D_MODEL, D_MODEL)
    args = (x, win, wga, wglr, bg, lbp)
    return pl.pallas_call(
        _sample_proj_kernel,
        in_specs=[_whole(a.shape) for a in args],
        out_specs=[_whole((n, w)) for w in widths],
        out_shape=[jax.ShapeDtypeStruct((n, w), F32) for w in widths],
        compiler_params=pltpu.CompilerParams(vmem_limit_bytes=VMEM_LIMIT),
        name="sample_proj",
    )(*args)


def _state_update(e, k, q, v, s_ref, snew_ref, o_ref, nheads, dk, dv):
    for h in range(nheads):
        ks = slice(h * dk, (h + 1) * dk)
        ec, kc, qc = (_columns(a[:, ks]) for a in (e, k, q))
        for j in range(SEQ_BLOCK):
            s_new = s_ref[j, h] * ec[:, j:j + 1] + kc[:, j:j + 1] * v[j:j + 1, h * dv:(h + 1) * dv]
            snew_ref[j, h] = s_new
            o_ref[j:j + 1, h * dv:(h + 1) * dv] = jnp.sum(s_new * qc[:, j:j + 1], axis=0, keepdims=True)


def _sample_state_kernel(eg_ref, kg_ref, qg_ref, vg_ref, eh_ref, kh_ref, qh_ref, vh_ref, sg_ref, sh_ref,
                         sgn_ref, shn_ref, oa_ref, ob_ref):
    _state_update(eg_ref[...], kg_ref[...], qg_ref[...], vg_ref[...], sg_ref, sgn_ref, oa_ref,
                  GLA_H, GLA_DK, GLA_DV)
    _state_update(eh_ref[...], kh_ref[...], qh_ref[...], vh_ref[...], sh_ref, shn_ref, ob_ref,
                  HG_H, HG_DK, HG_DV)


def _sample_state_call(eg, kg, qg, vg, eh, kh, qh, vh, sg, sh):
    n = sg.shape[0]
    sb = SEQ_BLOCK
    row = lambda w: pl.BlockSpec((sb, w), lambda i: (i, 0))
    gspec = pl.BlockSpec((sb, GLA_H, GLA_DK, GLA_DV), lambda i: (i, 0, 0, 0))
    hspec = pl.BlockSpec((sb, HG_H, HG_DK, HG_DV), lambda i: (i, 0, 0, 0))
    return pl.pallas_call(
        _sample_state_kernel,
        grid=(n // sb,),
        in_specs=[row(GLA_K), row(GLA_K), row(GLA_K), row(GLA_V), row(HG_K), row(HG_K), row(HG_K), row(HG_V),
                  gspec, hspec],
        out_specs=[gspec, hspec, row(GLA_V), row(HG_V)],
        out_shape=[jax.ShapeDtypeStruct(sg.shape, F32), jax.ShapeDtypeStruct(sh.shape, F32),
                   jax.ShapeDtypeStruct((n, GLA_V), F32), jax.ShapeDtypeStruct((n, HG_V), F32)],
        compiler_params=pltpu.CompilerParams(dimension_semantics=("arbitrary",), vmem_limit_bytes=VMEM_LIMIT),
        name="sample_state",
    )(eg, kg, qg, vg, eh, kh, qh, vh, sg, sh)


def _sample_post_kernel(x_ref, oa_ref, ob_ref, sgr_ref, shr_ref, mga_ref, mgb_ref,
                        gng_ref, wbg_ref, hng_ref, wbh_ref, wout_ref, lng_ref, lnb_ref, y_ref):
    ga = (_head_rmsnorm(oa_ref[...], gng_ref[...], GLA_DV) * sgr_ref[...]).astype(BF16)
    gb = (_head_rmsnorm(ob_ref[...], hng_ref[...], HG_DV) * shr_ref[...]).astype(BF16)
    merged = mga_ref[...] * _dot(ga, wbg_ref[...]) + mgb_ref[...] * _dot(gb, wbh_ref[...])
    z = DEEPNORM_ALPHA * x_ref[...] + _dot(merged.astype(BF16), wout_ref[...])
    y_ref[...] = _layernorm(z, lng_ref[...], lnb_ref[...])


def _sample_post_call(*args):
    n = args[0].shape[0]
    return pl.pallas_call(
        _sample_post_kernel,
        in_specs=[_whole(a.shape) for a in args],
        out_specs=_whole((n, D_MODEL)),
        out_shape=jax.ShapeDtypeStruct((n, D_MODEL), F32),
        compiler_params=pltpu.CompilerParams(vmem_limit_bytes=VMEM_LIMIT),
        name="sample_post",
    )(*args)


def _weight_prep_kernel(wt_ref, gat_ref, win_ref, wga_ref):
    win_ref[...] = wt_ref[...].T.astype(BF16)

    @pl.when(pl.program_id(0) == 0)
    def _():
        d = gat_ref.shape[1]
        ga = jnp.concatenate([gat_ref[...], jnp.zeros((LANES - GLA_RANK, d), F32)], axis=0)
        wga_ref[...] = ga.T.astype(BF16)


def _weight_prep_call(w_in):
    _, d, width = w_in.shape
    wt = jnp.swapaxes(w_in, 1, 2)[0]
    blk = COL_TILE
    src_row = lambda i: (i * (blk // GLA_RANK) + (i >= C_HQ // blk).astype(jnp.int32)) * GLA_RANK
    return pl.pallas_call(
        _weight_prep_kernel,
        grid=((width - GLA_RANK) // blk,),
        in_specs=[pl.BlockSpec((pl.Element(blk), pl.Element(d)), lambda i: (src_row(i), 0)),
                  pl.BlockSpec((pl.Element(GLA_RANK), pl.Element(d)), lambda i: (C_HQ, 0))],
        out_specs=[pl.BlockSpec((d, blk), lambda i: (0, i)),
                   pl.BlockSpec((d, LANES), lambda i: (0, 0))],
        out_shape=[jax.ShapeDtypeStruct((d, width - GLA_RANK), BF16), jax.ShapeDtypeStruct((d, LANES), BF16)],
        compiler_params=pltpu.CompilerParams(dimension_semantics=("arbitrary",), vmem_limit_bytes=VMEM_LIMIT),
        name="weight_prep",
    )(wt, wt)


def kernel(x_prompt, x_sample, state_gla, state_hgrn, w_in, w_gate_lr, b_gate_lr, gla_norm_g, w_br_gla,
           hgrn_lb_param, hgrn_norm_g, w_br_hgrn, w_out, ln_g, ln_b):
    assert w_in.shape[0] == DEPTH and x_sample.shape[1] == 1
    win, wga = _weight_prep_call(w_in)
    wglr = jnp.pad(w_gate_lr[0], ((0, LANES - GLA_RANK), (0, 0))).astype(BF16)
    bg = b_gate_lr[0].reshape(1, GLA_K)
    gng = gla_norm_g[0].reshape(1, GLA_V)
    hng = hgrn_norm_g[0].reshape(1, HG_V)
    wbg = w_br_gla[0].astype(BF16)
    wbh = w_br_hgrn[0].astype(BF16)
    wout = w_out[0].astype(BF16)
    lng = ln_g[0].reshape(1, D_MODEL)
    lnb = ln_b[0].reshape(1, D_MODEL)
    lbp = hgrn_lb_param

    y_prompt, gla_p, hgrn_p = _prompt_call(x_prompt, win, wga, wglr, bg, gng, wbg, lbp, hng, wbh, wout, lng, lnb)

    xs = x_sample[:, 0, :]
    eg, kg, qg, vg, eh, kh, qh, vh, sgr, shr, mga, mgb = _sample_proj_call(xs, win, wga, wglr, bg, lbp)
    gla_s, hgrn_s, oa, ob = _sample_state_call(eg, kg, qg, vg, eh, kh, qh, vh, state_gla[0], state_hgrn[0])
    ys = _sample_post_call(xs, oa, ob, sgr, shr, mga, mgb, gng, wbg, hng, wbh, wout, lng, lnb)
    return (y_prompt, ys[:, None, :], gla_p, hgrn_p, gla_s[None], hgrn_s[None])
```

```python
from typing import Any, NamedTuple

import jax
import jax.numpy as jnp
from jax import lax
from jax.experimental import pallas as pl
from jax.experimental.pallas import tpu as pltpu

F32 = jnp.float32
BF16 = jnp.bfloat16

D_MODEL = 1024
GLA_H, GLA_DK, GLA_DV = 4, 128, 256
HG_H, HG_DK, HG_DV = 8, 128, 128
GLA_RANK = 16
GLA_TAU = 16.0
LOG2_E = 1.4426950408889634
GLA_K = GLA_H * GLA_DK
GLA_V = GLA_H * GLA_DV
HG_K = HG_H * HG_DK
HG_V = HG_H * HG_DV
C_GQ, C_GK, C_GV, C_GR = 0, GLA_K, 2 * GLA_K, 2 * GLA_K + GLA_V
C_HQ = C_GR + GLA_V
C_HF, C_HI, C_HR = C_HQ + HG_K, C_HQ + 2 * HG_K, C_HQ + 2 * HG_K + HG_V
C_MA = C_HR + HG_V
C_MB = C_MA + D_MODEL
CHUNK = 64
SUB = 16
NORM_EPS = 1e-5
DEPTH = 1
DEEPNORM_ALPHA = (2.0 * DEPTH) ** 0.25
LANES = 128
BF16_SUBLANES = 16
TOK_BLOCK = 256
COL_TILE = 256
ROW_BLOCK = 64
LN_ROW_BLOCK = 16
SEQ_BLOCK = 8
WEIGHT_PREP_COLS = 1024
VMEM_LIMIT = 60 * 1024 * 1024


def _dot(a, b):
    return jnp.dot(a, b, preferred_element_type=F32)


def _dot_nt(a, b):
    return lax.dot_general(a, b, (((1,), (1,)), ((), ())), preferred_element_type=F32)


def _dot_tn(a, b):
    return lax.dot_general(a, b, (((0,), (0,)), ((), ())), preferred_element_type=F32)


def _sigmoid(x):
    return 1.0 / (1.0 + jnp.exp(-x))


def _sigmoid_pair(x):
    e = jnp.exp(-jnp.abs(x))
    s = 1.0 / (1.0 + e)
    t = e * s
    pos = x >= 0.0
    return jnp.where(pos, s, t), jnp.where(pos, t, s)


def _silu(x):
    return x * _sigmoid(x)


def _log_sigmoid(x):
    return jnp.minimum(x, 0.0) - jnp.log1p(jnp.exp(-jnp.abs(x)))


def _split3(x):
    hi = x.astype(BF16)
    r = x - hi.astype(F32)
    mid = r.astype(BF16)
    lo = (r - mid.astype(F32)).astype(BF16)
    return hi, mid, lo


def _block_tri(n):
    r = lax.broadcasted_iota(jnp.int32, (n, n), 0)
    c = lax.broadcasted_iota(jnp.int32, (n, n), 1)
    shift = CHUNK.bit_length() - 1
    keep = (c <= r) & (jnp.right_shift(r, shift) == jnp.right_shift(c, shift))
    return jnp.where(keep, 1.0, 0.0).astype(BF16)


def _lower_bound(lbp):
    rows = [lbp[i:i + 1, :] for i in range(lbp.shape[0])]
    m = rows[0]
    for r in rows[1:]:
        m = jnp.maximum(m, r)
    es = [jnp.exp(r - m) for r in rows]
    tot = es[0]
    for e in es[1:]:
        tot = tot + e
    return es[0] / tot


def _intra_scores(q, k, b, causal):
    dk = q.shape[1]
    rows = []
    for i in range(CHUNK // SUB):
        lo, hi = i * SUB, (i + 1) * SUB
        bs = b[lo:lo + 1, :]
        qi = (q[lo:hi] * jnp.exp2(b[lo:hi] - bs)).astype(BF16)
        kj = (k[0:hi] * jnp.exp2(bs - b[0:hi])).astype(BF16)
        kj = jnp.concatenate([kj, jnp.zeros((LANES - hi, dk), BF16)], axis=0)
        rows.append(_dot_nt(qi, kj))
    a = jnp.concatenate(rows, axis=0)
    return jnp.where(causal, a, 0.0)


def _columns(rows):
    r = rows.shape[0]
    if r < LANES:
        rows = jnp.concatenate([rows, jnp.zeros((LANES - r, LANES), F32)], axis=0)
    return rows.T


def _head_rmsnorm(o, gain, dv):
    outs = []
    for j in range(o.shape[1] // dv):
        oj = o[:, j * dv:(j + 1) * dv]
        ms = jnp.mean(oj * oj, axis=-1, keepdims=True)
        outs.append(oj * lax.rsqrt(ms + NORM_EPS))
    on = outs[0] if len(outs) == 1 else jnp.concatenate(outs, axis=1)
    return on * gain


def _layernorm(z, g, b):
    mu = jnp.mean(z, axis=-1, keepdims=True)
    zc = z - mu
    var = jnp.mean(zc * zc, axis=-1, keepdims=True)
    return zc * lax.rsqrt(var + NORM_EPS) * g + b


class _Task(NamedTuple):
    cost: float
    run: Any


def _merge(*lists):
    totals = [sum(t.cost for t in l) or 1.0 for l in lists]
    pos = [0] * len(lists)
    done = [0.0] * len(lists)
    out = []
    for _ in range(sum(len(l) for l in lists)):
        live = [i for i in range(len(lists)) if pos[i] < len(lists[i])]
        j = min(live, key=lambda i: (done[i] + 0.5 * lists[i][pos[i]].cost) / totals[i])
        out.append(lists[j][pos[j]])
        done[j] += lists[j][pos[j]].cost
        pos[j] += 1
    return out


def _run(tasks):
    for t in tasks:
        t.run()


def _row_blocks(nrows, step=ROW_BLOCK):
    return [slice(r, r + step) for r in range(0, nrows, step)]


class _Branch(NamedTuple):
    heads: int
    dk: int
    dv: int
    scale: float
    q: Any
    k: Any
    b: Any
    o: Any
    state: Any


def _slab(buf, hc, dv):
    per = buf.shape[2] // dv
    return hc // per, slice((hc % per) * dv, (hc % per + 1) * dv)


def _store_decay(b_s, n, bt):
    per = COL_TILE // LANES
    for j in range(per):
        b_s[n * per + j] = bt[:, j * LANES:(j + 1) * LANES]


def _store_values(w_s, n, vt, dk, dv, nchunks):
    per = COL_TILE // dv
    for j in range(per):
        for ci in range(nchunks):
            i, cs = _slab(w_s, (n * per + j) * nchunks + ci, dv)
            w_s[i, dk:dk + CHUNK, cs] = vt[ci * CHUNK:(ci + 1) * CHUNK, j * dv:(j + 1) * dv]


def _recurrence_tasks(br, qa_s, kd_s, w_s, u_s, nchunks):
    nheads, dk, dv = br.heads, br.dk, br.dv
    r = lax.broadcasted_iota(jnp.int32, (CHUNK, LANES), 0)
    c = lax.broadcasted_iota(jnp.int32, (CHUNK, LANES), 1)
    causal = c <= r
    pairs = [(ci, h) for ci in range(nchunks) for h in range(nheads)]
    rows_of = lambda ci: slice(ci * CHUNK, (ci + 1) * CHUNK)
    cols = []

    def operands(ci, h):
        rows, kcols = rows_of(ci), slice(h * dk, (h + 1) * dk)
        q, k, b = br.q[rows, kcols], br.k[rows, kcols], br.b[h, rows, :]
        if br.scale != 1.0:
            q = q * br.scale
        qa_s[h, rows, 0:dk] = (q * jnp.exp2(b)).astype(BF16)
        kd_s[h, rows, :] = (k * jnp.exp2(b[CHUNK - 1:CHUNK, :] - b)).astype(BF16)
        qa_s[h, rows, dk:dk + LANES] = _intra_scores(q, k, b, causal).astype(BF16)

    def outer(ci, h):
        i, cs = _slab(w_s, h * nchunks + ci, dv)
        u_s[i, :, cs] = _dot_tn(kd_s[h, rows_of(ci), :], w_s[i, dk:dk + CHUNK, cs])

    def decay_columns():
        half = CHUNK // 2
        for h in range(nheads):
            cols.append(_columns(jnp.exp2(br.b[h, pl.ds(half - 1, 2 * nchunks, stride=half), :])))

    def chain(ci, h):
        i, cs = _slab(w_s, h * nchunks + ci, dv)
        s = br.state[0, 0, h]
        w_s[i, 0:dk, cs] = s.astype(BF16)
        br.state[0, 0, h] = s * cols[h][:, 2 * ci + 1:2 * ci + 2] + u_s[i, :, cs]

    def output(ci, h):
        i, cs = _slab(w_s, h * nchunks + ci, dv)
        br.o[rows_of(ci), h * dv:(h + 1) * dv] = _dot(qa_s[h, rows_of(ci), :], w_s[i, :, cs])

    bind = lambda cost, fn: [_Task(cost, lambda ci=ci, h=h: fn(ci, h)) for ci, h in pairs]
    return bind(70, operands), (bind(32, outer) + [_Task(16 * nheads, decay_columns)]
                                + bind(8 + dv // 8, chain) + bind(16, output))


def _prompt_kernel(x_ref, win_ref, wga_ref, wglr_ref, bg_ref, gng_ref, wbg_ref,
                   lbp_ref, hng_ref, wbh_ref, wout_ref, lng_ref, lnb_ref,
                   y_ref, sg_ref, sh_ref,
                   xb_s, tri_s, ga_s, qg_s, kg_s, bgl_s, qh_s, kh_s, bh_s, h3_s, qa_s, kd_s, w_s, u_s,
                   og_s, oh_s, sgr_s, shr_s, mga_s, mgb_s, g_s, m_s, p_s):
    tb = TOK_BLOCK
    nch = tb // CHUNK
    tiles = lambda width: range(width // COL_TILE)
    col = lambda n, base=0: slice(base + n * COL_TILE, base + (n + 1) * COL_TILE)
    gla = _Branch(GLA_H, GLA_DK, GLA_DV, GLA_DK ** -0.5, qg_s, kg_s, bgl_s, og_s, sg_ref)
    hgrn = _Branch(HG_H, HG_DK, HG_DV, 1.0, qh_s, kh_s, bh_s, oh_s, sh_ref)
    mm_cost = 256.0

    @pl.when(pl.program_id(1) == 0)
    def _():
        sg_ref[...] = jnp.zeros_like(sg_ref)
        sh_ref[...] = jnp.zeros_like(sh_ref)
        tri_s[...] = _block_tri(tb)
        w_s[:, LANES + CHUNK:, :] = jnp.zeros((w_s.shape[0], LANES - CHUNK, w_s.shape[2]), BF16)

    xb_s[...] = x_ref[0].astype(BF16)
    lb = _lower_bound(lbp_ref[...])

    def mm(dst, w_ref, wcols, lhs=xb_s, cast=None, cost=mm_cost):
        def run():
            r = _dot(lhs[...], w_ref[:, wcols])
            dst(r if cast is None else r.astype(cast))
        return _Task(cost, run)

    def store(ref, cols):
        def put(v):
            ref[:, cols] = v
        return put

    def ew(cost, fn, rows=(0, tb), step=ROW_BLOCK):
        def run():
            for r in range(rows[0], rows[1], step):
                fn(slice(r, r + step))
        return _Task(cost, run)

    def cumsum_task(b_s, n):
        def run():
            t = tri_s[...]
            _store_decay(b_s, n, _dot(t, h3_s[0, :, col(n)]) + _dot(t, h3_s[1, :, col(n)]) + _dot(t, h3_s[2, :, col(n)]))
        return _Task(192.0, run)

    def put_split3(rb, cs, g):
        hi, mid, lo = _split3(g)
        h3_s[0, rb, cs] = hi
        h3_s[1, rb, cs] = mid
        h3_s[2, rb, cs] = lo

    def gla_logit(n):
        def run():
            qg_s[:, col(n)] = _dot(ga_s[...], wglr_ref[:, col(n)]) + bg_ref[:, col(n)]
        return _Task(64.0, run)

    def gla_log(n):
        def fn(rb):
            put_split3(rb, col(n), _log_sigmoid(qg_s[rb, col(n)]) * (LOG2_E / GLA_TAU))
        return ew(300.0, fn)

    gla_in = [mm(store(ga_s, slice(None)), wga_ref, slice(None), cast=BF16, cost=128.0)]
    gla_in += [mm(store(kg_s, col(n)), win_ref, col(n, C_GK)) for n in tiles(GLA_K)]
    gla_in += [gla_logit(n) for n in tiles(GLA_K)]
    gla_in += [t for n in tiles(GLA_K) for t in (
        mm(lambda v, n=n: _store_values(w_s, n, v, GLA_DK, GLA_DV, nch), win_ref, col(n, C_GV), cast=BF16),
        gla_log(n),
        mm(lambda v, n=n: _store_values(w_s, n + GLA_K // COL_TILE, v, GLA_DK, GLA_DV, nch), win_ref,
           col(n + GLA_K // COL_TILE, C_GV), cast=BF16))]
    gla_in += [cumsum_task(bgl_s, n) for n in tiles(GLA_K)]
    gla_in += [mm(store(qg_s, col(n)), win_ref, col(n, C_GQ)) for n in tiles(GLA_K)]
    _run(gla_in)

    def hq_ew(n):
        def fn(rb):
            qh_s[rb, col(n)] = _silu(qh_s[rb, col(n)])
        return ew(130.0, fn)

    def hf_ew(n):
        def fn(rb):
            sf, sfn = _sigmoid_pair(kh_s[rb, col(n)])
            lbn = lb[:, col(n)]
            kh_s[rb, col(n)] = (1.0 - lbn) * sfn
            put_split3(rb, col(n), jnp.log2(lbn + (1.0 - lbn) * sf))
        return ew(450.0, fn)

    def gate_ew(ref, act, n):
        def fn(rb):
            ref[rb, col(n)] = act(ref[rb, col(n)])
        return ew(130.0, fn)

    hg_in = []
    for n in tiles(HG_K):
        hg_in += [mm(store(kh_s, col(n)), win_ref, col(n, C_HF)),
                  mm(store(qh_s, col(n)), win_ref, col(n, C_HQ)),
                  hf_ew(n), hq_ew(n), cumsum_task(bh_s, n)]
    gates = []
    for n in tiles(GLA_V):
        gates += [mm(store(sgr_s, col(n)), win_ref, col(n, C_GR)), gate_ew(sgr_s, _silu, n)]
    for n in tiles(D_MODEL):
        gates += [mm(store(mga_s, col(n)), win_ref, col(n, C_MA)), gate_ew(mga_s, _sigmoid, n)]
    gates_b = []
    for n in tiles(HG_V):
        gates_b += [mm(store(shr_s, col(n)), win_ref, col(n, C_HR)), gate_ew(shr_s, _silu, n)]
    hg_v = [mm(lambda v, n=n: _store_values(w_s, n, v, HG_DK, HG_DV, nch), win_ref, col(n, C_HI), cast=BF16)
            for n in tiles(HG_V)]
    mgb = []
    for n in tiles(D_MODEL):
        mgb += [mm(store(mgb_s, col(n)), win_ref, col(n, C_MB)), gate_ew(mgb_s, _sigmoid, n)]

    ra_ops, ra_rest = _recurrence_tasks(gla, qa_s, kd_s, w_s, u_s, nch)
    _run(_merge(ra_ops + ra_rest, hg_in + gates))

    def gla_norm(n):
        def fn(rb):
            g_s[rb, col(n)] = (_head_rmsnorm(og_s[rb, col(n)], gng_ref[:, col(n)], GLA_DV) * sgr_s[rb, col(n)]).astype(BF16)
        return ew(200.0, fn)

    def gla_scale(n):
        def fn(rb):
            p_s[rb, col(n)] = p_s[rb, col(n)] * mga_s[rb, col(n)]
        return ew(50.0, fn)

    gla_post = [gla_norm(n) for n in tiles(GLA_V)]
    for n in tiles(D_MODEL):
        gla_post += [mm(store(p_s, col(n)), wbg_ref, col(n), lhs=g_s), gla_scale(n)]
    rb_ops, rb_rest = _recurrence_tasks(hgrn, qa_s, kd_s, w_s, u_s, nch)
    _run(_merge(rb_ops, hg_v + gates_b))
    _run(_merge(rb_rest, gla_post))

    def hgrn_norm(n):
        def fn(rb):
            m_s[rb, col(n)] = (_head_rmsnorm(oh_s[rb, col(n)], hng_ref[:, col(n)], HG_DV) * shr_s[rb, col(n)]).astype(BF16)
        return ew(220.0, fn)

    def merge_ew(n):
        def fn(rb):
            g_s[rb, col(n)] = (p_s[rb, col(n)] + mgb_s[rb, col(n)] * sgr_s[rb, col(n)]).astype(BF16)
        return ew(60.0, fn)

    def final_ew():
        def fn(rb):
            z = DEEPNORM_ALPHA * x_ref[0, rb, :] + shr_s[rb, :]
            y_ref[0, rb, :] = _layernorm(z, lng_ref[...], lnb_ref[...])
        return ew(700.0, fn, step=LN_ROW_BLOCK)

    _run(_merge([hgrn_norm(n) for n in tiles(HG_V)], mgb))
    _run([mm(store(sgr_s, col(n)), wbh_ref, col(n), lhs=m_s) for n in tiles(D_MODEL)])
    _run([merge_ew(n) for n in tiles(D_MODEL)])
    _run([mm(store(shr_s, col(n)), wout_ref, col(n), lhs=g_s) for n in tiles(D_MODEL)])
    _run([final_ew()])


def _whole(shape):
    return pl.BlockSpec(memory_space=pltpu.VMEM)


def _prompt_call(x, win, wga, wglr, bg, gng, wbg, lbp, hng, wbh, wout, lng, lnb):
    bsz, seq, d = x.shape
    tb = TOK_BLOCK
    nch = tb // CHUNK
    weights = (win, wga, wglr, bg, gng, wbg, lbp, hng, wbh, wout, lng, lnb)
    return pl.pallas_call(
        _prompt_kernel,
        grid=(bsz, seq // tb),
        in_specs=[pl.BlockSpec((1, tb, d), lambda b, t: (b, t, 0))] + [_whole(w.shape) for w in weights],
        out_specs=[
            pl.BlockSpec((1, tb, d), lambda b, t: (b, t, 0)),
            pl.BlockSpec((1, 1, GLA_H, GLA_DK, GLA_DV), lambda b, t: (0, b, 0, 0, 0)),
            pl.BlockSpec((1, 1, HG_H, HG_DK, HG_DV), lambda b, t: (0, b, 0, 0, 0)),
        ],
        out_shape=[
            jax.ShapeDtypeStruct((bsz, seq, d), F32),
            jax.ShapeDtypeStruct((DEPTH, bsz, GLA_H, GLA_DK, GLA_DV), F32),
            jax.ShapeDtypeStruct((DEPTH, bsz, HG_H, HG_DK, HG_DV), F32),
        ],
        scratch_shapes=[
            pltpu.VMEM((tb, d), BF16),
            pltpu.VMEM((tb, tb), BF16),
            pltpu.VMEM((tb, LANES), BF16),
            pltpu.VMEM((tb, GLA_K), F32),
            pltpu.VMEM((tb, GLA_K), F32),
            pltpu.VMEM((GLA_H, tb, LANES), F32),
            pltpu.VMEM((tb, HG_K), F32),
            pltpu.VMEM((tb, HG_K), F32),
            pltpu.VMEM((HG_H, tb, LANES), F32),
            pltpu.VMEM((3, tb, HG_K), BF16),
            pltpu.VMEM((HG_H, tb, 2 * LANES), BF16),
            pltpu.VMEM((HG_H, tb, LANES), BF16),
            pltpu.VMEM((GLA_H * nch, 2 * LANES, GLA_DV), BF16),
            pltpu.VMEM((GLA_H * nch, GLA_DK, GLA_DV), F32),
            pltpu.VMEM((tb, GLA_V), F32),
            pltpu.VMEM((tb, HG_V), F32),
            pltpu.VMEM((tb, GLA_V), F32),
            pltpu.VMEM((tb, HG_V), F32),
            pltpu.VMEM((tb, d), F32),
            pltpu.VMEM((tb, d), F32),
            pltpu.VMEM((tb, d), BF16),
            pltpu.VMEM((tb, d), BF16),
            pltpu.VMEM((tb, d), F32),
        ],
        compiler_params=pltpu.CompilerParams(
            dimension_semantics=("arbitrary", "arbitrary"), vmem_limit_bytes=VMEM_LIMIT),
        name="prompt_layer",
    )(x, *weights)


def _sample_proj_kernel(x_ref, win_ref, wga_ref, wglr_ref, bg_ref, lbp_ref,
                        eg_ref, kg_ref, qg_ref, vg_ref, eh_ref, kh_ref, qh_ref, vh_ref,
                        sgr_ref, shr_ref, mga_ref, mgb_ref):
    xb = x_ref[...].astype(BF16)
    proj = lambda base, width: _dot(xb, win_ref[:, base:base + width])
    qg_ref[...] = proj(C_GQ, GLA_K) * (GLA_DK ** -0.5)
    kg_ref[...] = proj(C_GK, GLA_K)
    vg_ref[...] = proj(C_GV, GLA_V)
    sgr_ref[...] = _silu(proj(C_GR, GLA_V))
    ga = _dot(xb, wga_ref[...]).astype(BF16)
    a_logit = _dot(ga, wglr_ref[...]) + bg_ref[...]
    eg_ref[...] = jnp.exp(_log_sigmoid(a_logit) * (1.0 / GLA_TAU))
    lb = _lower_bound(lbp_ref[...])
    qh_ref[...] = _silu(proj(C_HQ, HG_K))
    sf, sfn = _sigmoid_pair(proj(C_HF, HG_K))
    eh_ref[...] = jnp.exp(jnp.log(lb + (1.0 - lb) * sf))
    kh_ref[...] = (1.0 - lb) * sfn
    vh_ref[...] = proj(C_HI, HG_V)
    shr_ref[...] = _silu(proj(C_HR, HG_V))
    mga_ref[...] = _sigmoid(proj(C_MA, D_MODEL))
    mgb_ref[...] = _sigmoid(proj(C_MB, D_MODEL))


def _sample_proj_call(x, win, wga, wglr, bg, lbp):
    n = x.shape[0]
    widths = (GLA_K, GLA_K, GLA_K, GLA_V, HG_K, HG_K, HG_K, HG_V, GLA_V, HG_V, D_MODEL, D_MODEL)
    args = (x, win, wga, wglr, bg, lbp)
    return pl.pallas_call(
        _sample_proj_kernel,
        in_specs=[_whole(a.shape) for a in args],
        out_specs=[_whole((n, w)) for w in widths],
        out_shape=[jax.ShapeDtypeStruct((n, w), F32) for w in widths],
        compiler_params=pltpu.CompilerParams(vmem_limit_bytes=VMEM_LIMIT),
        name="sample_proj",
    )(*args)


def _state_update(e, k, q, v, s_ref, snew_ref, o_ref, nheads, dk, dv):
    pad_rows = BF16_SUBLANES - SEQ_BLOCK
    row = lax.broadcasted_iota(jnp.int32, (BF16_SUBLANES, dk), 0)
    padded = lambda a: jnp.concatenate([a, jnp.zeros((pad_rows, a.shape[1]), a.dtype)], axis=0)
    ks = lambda h: slice(h * dk, (h + 1) * dk)
    vs = lambda h: slice(h * dv, (h + 1) * dv)
    for h in range(nheads):
        k16 = padded(k[:, ks(h)])
        v16 = padded(v[:, vs(h)]).astype(BF16)
        for j in range(SEQ_BLOCK):
            snew_ref[j, h] = _dot_tn(jnp.where(row == j, k16, 0.0).astype(BF16), v16)
    for h in range(nheads):
        ec = _columns(e[:, ks(h)])
        for j in range(SEQ_BLOCK):
            snew_ref[j, h] = s_ref[j, h] * ec[:, j:j + 1] + snew_ref[j, h]
    for h in range(nheads):
        q16 = padded(q[:, ks(h)]).astype(BF16)
        for j in range(SEQ_BLOCK):
            o_ref[j:j + 1, vs(h)] = _dot(q16, snew_ref[j, h].astype(BF16))[j:j + 1, :]


def _sample_state_kernel(eg_ref, kg_ref, qg_ref, vg_ref, eh_ref, kh_ref, qh_ref, vh_ref, sg_ref, sh_ref,
                         sgn_ref, shn_ref, oa_ref, ob_ref):
    _state_update(eg_ref[...], kg_ref[...], qg_ref[...], vg_ref[...], sg_ref, sgn_ref, oa_ref,
                  GLA_H, GLA_DK, GLA_DV)
    _state_update(eh_ref[...], kh_ref[...], qh_ref[...], vh_ref[...], sh_ref, shn_ref, ob_ref,
                  HG_H, HG_DK, HG_DV)


def _sample_state_call(eg, kg, qg, vg, eh, kh, qh, vh, sg, sh):
    n = sg.shape[0]
    sb = SEQ_BLOCK
    row = lambda w: pl.BlockSpec((sb, w), lambda i: (i, 0))
    gspec = pl.BlockSpec((sb, GLA_H, GLA_DK, GLA_DV), lambda i: (i, 0, 0, 0))
    hspec = pl.BlockSpec((sb, HG_H, HG_DK, HG_DV), lambda i: (i, 0, 0, 0))
    return pl.pallas_call(
        _sample_state_kernel,
        grid=(n // sb,),
        in_specs=[row(GLA_K), row(GLA_K), row(GLA_K), row(GLA_V), row(HG_K), row(HG_K), row(HG_K), row(HG_V),
                  gspec, hspec],
        out_specs=[gspec, hspec, row(GLA_V), row(HG_V)],
        out_shape=[jax.ShapeDtypeStruct(sg.shape, F32), jax.ShapeDtypeStruct(sh.shape, F32),
                   jax.ShapeDtypeStruct((n, GLA_V), F32), jax.ShapeDtypeStruct((n, HG_V), F32)],
        compiler_params=pltpu.CompilerParams(dimension_semantics=("arbitrary",), vmem_limit_bytes=VMEM_LIMIT),
        name="sample_state",
    )(eg, kg, qg, vg, eh, kh, qh, vh, sg, sh)


def _sample_post_kernel(x_ref, oa_ref, ob_ref, sgr_ref, shr_ref, mga_ref, mgb_ref,
                        gng_ref, wbg_ref, hng_ref, wbh_ref, wout_ref, lng_ref, lnb_ref, y_ref):
    ga = (_head_rmsnorm(oa_ref[...], gng_ref[...], GLA_DV) * sgr_ref[...]).astype(BF16)
    gb = (_head_rmsnorm(ob_ref[...], hng_ref[...], HG_DV) * shr_ref[...]).astype(BF16)
    merged = mga_ref[...] * _dot(ga, wbg_ref[...]) + mgb_ref[...] * _dot(gb, wbh_ref[...])
    z = DEEPNORM_ALPHA * x_ref[...] + _dot(merged.astype(BF16), wout_ref[...])
    y_ref[...] = _layernorm(z, lng_ref[...], lnb_ref[...])


def _sample_post_call(*args):
    n = args[0].shape[0]
    return pl.pallas_call(
        _sample_post_kernel,
        in_specs=[_whole(a.shape) for a in args],
        out_specs=_whole((n, D_MODEL)),
        out_shape=jax.ShapeDtypeStruct((n, D_MODEL), F32),
        compiler_params=pltpu.CompilerParams(vmem_limit_bytes=VMEM_LIMIT),
        name="sample_post",
    )(*args)


def _weight_prep_kernel(wt_ref, gat_ref, win_ref, wga_ref):
    win_ref[...] = wt_ref[...].T.astype(BF16)

    @pl.when(pl.program_id(0) == 0)
    def _():
        d = gat_ref.shape[1]
        ga = jnp.concatenate([gat_ref[...], jnp.zeros((LANES - GLA_RANK, d), F32)], axis=0)
        wga_ref[...] = ga.T.astype(BF16)


def _weight_prep_call(w_in):
    _, d, width = w_in.shape
    wt = jnp.swapaxes(w_in, 1, 2)[0]
    blk = WEIGHT_PREP_COLS
    assert C_HQ % blk == 0 and (width - GLA_RANK) % blk == 0
    src_row = lambda i: (i * (blk // GLA_RANK) + (i >= C_HQ // blk).astype(jnp.int32)) * GLA_RANK
    return pl.pallas_call(
        _weight_prep_kernel,
        grid=((width - GLA_RANK) // blk,),
        in_specs=[pl.BlockSpec((pl.Element(blk), pl.Element(d)), lambda i: (src_row(i), 0)),
                  pl.BlockSpec((pl.Element(GLA_RANK), pl.Element(d)), lambda i: (C_HQ, 0))],
        out_specs=[pl.BlockSpec((d, blk), lambda i: (0, i)),
                   pl.BlockSpec((d, LANES), lambda i: (0, 0))],
        out_shape=[jax.ShapeDtypeStruct((d, width - GLA_RANK), BF16), jax.ShapeDtypeStruct((d, LANES), BF16)],
        compiler_params=pltpu.CompilerParams(dimension_semantics=("arbitrary",), vmem_limit_bytes=VMEM_LIMIT),
        name="weight_prep",
    )(wt, wt)


def kernel(x_prompt, x_sample, state_gla, state_hgrn, w_in, w_gate_lr, b_gate_lr, gla_norm_g, w_br_gla,
           hgrn_lb_param, hgrn_norm_g, w_br_hgrn, w_out, ln_g, ln_b):
    assert w_in.shape[0] == DEPTH and x_sample.shape[1] == 1
    win, wga = _weight_prep_call(w_in)
    wglr = jnp.pad(w_gate_lr[0], ((0, LANES - GLA_RANK), (0, 0))).astype(BF16)
    bg = b_gate_lr[0].reshape(1, GLA_K)
    gng = gla_norm_g[0].reshape(1, GLA_V)
    hng = hgrn_norm_g[0].reshape(1, HG_V)
    wbg = w_br_gla[0].astype(BF16)
    wbh = w_br_hgrn[0].astype(BF16)
    wout = w_out[0].astype(BF16)
    lng = ln_g[0].reshape(1, D_MODEL)
    lnb = ln_b[0].reshape(1, D_MODEL)
    lbp = hgrn_lb_param

    y_prompt, gla_p, hgrn_p = _prompt_call(x_prompt, win, wga, wglr, bg, gng, wbg, lbp, hng, wbh, wout, lng, lnb)

    xs = x_sample[:, 0, :]
    eg, kg, qg, vg, eh, kh, qh, vh, sgr, shr, mga, mgb = _sample_proj_call(xs, win, wga, wglr, bg, lbp)
    gla_s, hgrn_s, oa, ob = _sample_state_call(eg, kg, qg, vg, eh, kh, qh, vh, state_gla[0], state_hgrn[0])
    ys = _sample_post_call(xs, oa, ob, sgr, shr, mga, mgb, gng, wbg, hng, wbh, wout, lng, lnb)
    return (y_prompt, ys[:, None, :], gla_p, hgrn_p, gla_s[None], hgrn_s[None])
```

```python
from typing import Any, NamedTuple

import jax
import jax.numpy as jnp
from jax import lax
from jax.experimental import pallas as pl
from jax.experimental.pallas import tpu as pltpu

F32 = jnp.float32
BF16 = jnp.bfloat16

D_MODEL = 1024
GLA_H, GLA_DK, GLA_DV = 4, 128, 256
HG_H, HG_DK, HG_DV = 8, 128, 128
GLA_RANK = 16
GLA_TAU = 16.0
LOG2_E = 1.4426950408889634
GLA_K = GLA_H * GLA_DK
GLA_V = GLA_H * GLA_DV
HG_K = HG_H * HG_DK
HG_V = HG_H * HG_DV
C_GQ, C_GK, C_GV, C_GR = 0, GLA_K, 2 * GLA_K, 2 * GLA_K + GLA_V
C_HQ = C_GR + GLA_V
C_HF, C_HI, C_HR = C_HQ + HG_K, C_HQ + 2 * HG_K, C_HQ + 2 * HG_K + HG_V
C_MA = C_HR + HG_V
C_MB = C_MA + D_MODEL
CHUNK = 64
SUB = 16
NORM_EPS = 1e-5
DEPTH = 1
DEEPNORM_ALPHA = (2.0 * DEPTH) ** 0.25
LANES = 128
BF16_SUBLANES = 16
TOK_BLOCK = 256
COL_TILE = 256
ROW_BLOCK = 64
LN_ROW_BLOCK = 16
SEQ_BLOCK = 8
WEIGHT_PREP_COLS = 1024
VMEM_LIMIT = 60 * 1024 * 1024


def _dot(a, b):
    return jnp.dot(a, b, preferred_element_type=F32)


def _dot_nt(a, b):
    return lax.dot_general(a, b, (((1,), (1,)), ((), ())), preferred_element_type=F32)


def _dot_tn(a, b):
    return lax.dot_general(a, b, (((0,), (0,)), ((), ())), preferred_element_type=F32)


def _sigmoid(x):
    return 1.0 / (1.0 + jnp.exp(-x))


def _sigmoid_pair(x):
    e = jnp.exp(-jnp.abs(x))
    s = 1.0 / (1.0 + e)
    t = e * s
    pos = x >= 0.0
    return jnp.where(pos, s, t), jnp.where(pos, t, s)


def _silu(x):
    return x * _sigmoid(x)


def _log_sigmoid(x):
    return jnp.minimum(x, 0.0) - jnp.log1p(jnp.exp(-jnp.abs(x)))


def _split3(x):
    hi = x.astype(BF16)
    r = x - hi.astype(F32)
    mid = r.astype(BF16)
    lo = (r - mid.astype(F32)).astype(BF16)
    return hi, mid, lo


def _block_tri(n):
    r = lax.broadcasted_iota(jnp.int32, (n, n), 0)
    c = lax.broadcasted_iota(jnp.int32, (n, n), 1)
    shift = CHUNK.bit_length() - 1
    keep = (c <= r) & (jnp.right_shift(r, shift) == jnp.right_shift(c, shift))
    return jnp.where(keep, 1.0, 0.0).astype(BF16)


def _lower_bound(lbp):
    rows = [lbp[i:i + 1, :] for i in range(lbp.shape[0])]
    m = rows[0]
    for r in rows[1:]:
        m = jnp.maximum(m, r)
    es = [jnp.exp(r - m) for r in rows]
    tot = es[0]
    for e in es[1:]:
        tot = tot + e
    return es[0] / tot


def _score_mask():
    t = lax.broadcasted_iota(jnp.int32, (CHUNK, LANES), 0)
    lane = lax.broadcasted_iota(jnp.int32, (CHUNK, LANES), 1)
    s = lane - CHUNK * (jnp.right_shift(t, SUB.bit_length() - 1) & 1)
    return (s >= 0) & (s <= t)


def _intra_scores(q, k, b, keep):
    dk = q.shape[1]
    nsub = CHUNK // SUB
    qs, ks = [], []
    for i in range(nsub):
        lo, hi = i * SUB, (i + 1) * SUB
        bs = b[lo:lo + 1, :]
        qs.append(q[lo:hi] * jnp.exp2(b[lo:hi] - bs))
        ks.append((k[0:hi] * jnp.exp2(bs - b[0:hi])).astype(BF16))
        if hi < CHUNK:
            ks.append(jnp.zeros((CHUNK - hi, dk), BF16))
    prod = _dot_nt(jnp.concatenate(qs, axis=0).astype(BF16), jnp.concatenate(ks, axis=0))
    per = LANES // CHUNK
    a = jnp.concatenate([prod[i * SUB:(i + 1) * SUB, (i // per) * LANES:(i // per + 1) * LANES]
                         for i in range(nsub)], axis=0)
    return jnp.where(keep, a, 0.0)


def _columns(rows):
    r = rows.shape[0]
    if r < LANES:
        rows = jnp.concatenate([rows, jnp.zeros((LANES - r, LANES), F32)], axis=0)
    return rows.T


def _head_rmsnorm(o, gain, dv):
    outs = []
    for j in range(o.shape[1] // dv):
        oj = o[:, j * dv:(j + 1) * dv]
        ms = jnp.mean(oj * oj, axis=-1, keepdims=True)
        outs.append(oj * lax.rsqrt(ms + NORM_EPS))
    on = outs[0] if len(outs) == 1 else jnp.concatenate(outs, axis=1)
    return on * gain


def _layernorm(z, g, b):
    mu = jnp.mean(z, axis=-1, keepdims=True)
    zc = z - mu
    var = jnp.mean(zc * zc, axis=-1, keepdims=True)
    return zc * lax.rsqrt(var + NORM_EPS) * g + b


class _Task(NamedTuple):
    cost: float
    run: Any


def _merge(*lists):
    totals = [sum(t.cost for t in l) or 1.0 for l in lists]
    pos = [0] * len(lists)
    done = [0.0] * len(lists)
    out = []
    for _ in range(sum(len(l) for l in lists)):
        live = [i for i in range(len(lists)) if pos[i] < len(lists[i])]
        j = min(live, key=lambda i: (done[i] + 0.5 * lists[i][pos[i]].cost) / totals[i])
        out.append(lists[j][pos[j]])
        done[j] += lists[j][pos[j]].cost
        pos[j] += 1
    return out


def _lagged(groups, lag=1):
    out = []
    for i, (mms, _) in enumerate(groups):
        out += mms
        if i >= lag:
            out += groups[i - lag][1]
    for _, ews in groups[max(len(groups) - lag, 0):]:
        out += ews
    return out


def _run(tasks):
    for t in tasks:
        t.run()


def _row_blocks(nrows, step=ROW_BLOCK):
    return [slice(r, r + step) for r in range(0, nrows, step)]


class _Branch(NamedTuple):
    heads: int
    dk: int
    dv: int
    scale: float
    q: Any
    k: Any
    b: Any
    o: Any
    state: Any


def _slab(buf, hc, dv):
    per = buf.shape[2] // dv
    return hc // per, slice((hc % per) * dv, (hc % per + 1) * dv)


def _store_decay(b_s, n, bt):
    per = COL_TILE // LANES
    for j in range(per):
        b_s[n * per + j] = bt[:, j * LANES:(j + 1) * LANES]


def _store_values(w_s, n, vt, dk, dv, nchunks):
    per = COL_TILE // dv
    for j in range(per):
        for ci in range(nchunks):
            i, cs = _slab(w_s, (n * per + j) * nchunks + ci, dv)
            v = vt[ci * CHUNK:(ci + 1) * CHUNK, j * dv:(j + 1) * dv]
            w_s[i, dk:dk + CHUNK, cs] = v
            w_s[i, dk + CHUNK:dk + 2 * CHUNK, cs] = v


def _recurrence_tasks(br, qa_s, kd_s, w_s, u_s, nchunks):
    nheads, dk, dv = br.heads, br.dk, br.dv
    keep = _score_mask()
    pairs = [(ci, h) for ci in range(nchunks) for h in range(nheads)]
    rows_of = lambda ci: slice(ci * CHUNK, (ci + 1) * CHUNK)
    cols = []

    def operands(ci, h):
        rows, kcols = rows_of(ci), slice(h * dk, (h + 1) * dk)
        q, k, b = br.q[rows, kcols], br.k[rows, kcols], br.b[h, rows, :]
        if br.scale != 1.0:
            q = q * br.scale
        qa_s[h, rows, 0:dk] = (q * jnp.exp2(b)).astype(BF16)
        kd_s[h, rows, :] = (k * jnp.exp2(b[CHUNK - 1:CHUNK, :] - b)).astype(BF16)
        qa_s[h, rows, dk:dk + LANES] = _intra_scores(q, k, b, keep).astype(BF16)

    def outer(ci, h):
        i, cs = _slab(w_s, h * nchunks + ci, dv)
        u_s[i, :, cs] = _dot_tn(kd_s[h, rows_of(ci), :], w_s[i, dk:dk + CHUNK, cs])

    def decay_columns():
        half = CHUNK // 2
        for h in range(nheads):
            cols.append(_columns(jnp.exp2(br.b[h, pl.ds(half - 1, 2 * nchunks, stride=half), :])))

    def chain(ci, h):
        i, cs = _slab(w_s, h * nchunks + ci, dv)
        s = br.state[0, 0, h]
        w_s[i, 0:dk, cs] = s.astype(BF16)
        br.state[0, 0, h] = s * cols[h][:, 2 * ci + 1:2 * ci + 2] + u_s[i, :, cs]

    def output(ci, h):
        i, cs = _slab(w_s, h * nchunks + ci, dv)
        br.o[rows_of(ci), h * dv:(h + 1) * dv] = _dot(qa_s[h, rows_of(ci), :], w_s[i, :, cs])

    bind = lambda cost, fn: [_Task(cost, lambda ci=ci, h=h: fn(ci, h)) for ci, h in pairs]
    return bind(70, operands), (bind(32, outer) + [_Task(16 * nheads, decay_columns)]
                                + bind(8 + dv // 8, chain) + bind(16, output))


def _prompt_kernel(x_ref, win_ref, wga_ref, wglr_ref, bg_ref, gng_ref, wbg_ref,
                   lbp_ref, hng_ref, wbh_ref, wout_ref, lng_ref, lnb_ref,
                   y_ref, sg_ref, sh_ref,
                   xb_s, tri_s, ga_s, qg_s, kg_s, bgl_s, qh_s, kh_s, bh_s, h3_s, qa_s, kd_s, w_s, u_s,
                   og_s, oh_s, sgr_s, shr_s, mga_s, mgb_s, g_s, m_s, p_s):
    tb = TOK_BLOCK
    nch = tb // CHUNK
    tiles = lambda width: range(width // COL_TILE)
    col = lambda n, base=0: slice(base + n * COL_TILE, base + (n + 1) * COL_TILE)
    gla = _Branch(GLA_H, GLA_DK, GLA_DV, GLA_DK ** -0.5, qg_s, kg_s, bgl_s, og_s, sg_ref)
    hgrn = _Branch(HG_H, HG_DK, HG_DV, 1.0, qh_s, kh_s, bh_s, oh_s, sh_ref)
    mm_cost = 256.0

    @pl.when(pl.program_id(1) == 0)
    def _():
        sg_ref[...] = jnp.zeros_like(sg_ref)
        sh_ref[...] = jnp.zeros_like(sh_ref)
        tri_s[...] = _block_tri(tb)

    xb_s[...] = x_ref[0].astype(BF16)
    lb = _lower_bound(lbp_ref[...])

    def mm(dst, w_ref, wcols, lhs=xb_s, cast=None, cost=mm_cost):
        def run():
            r = _dot(lhs[...], w_ref[:, wcols])
            dst(r if cast is None else r.astype(cast))
        return _Task(cost, run)

    def store(ref, cols):
        def put(v):
            ref[:, cols] = v
        return put

    def ew(cost, fn, rows=(0, tb), step=ROW_BLOCK):
        def run():
            for r in range(rows[0], rows[1], step):
                fn(slice(r, r + step))
        return _Task(cost, run)

    def cumsum_task(b_s, n):
        def run():
            t = tri_s[...]
            _store_decay(b_s, n, _dot(t, h3_s[0, :, col(n)]) + _dot(t, h3_s[1, :, col(n)]) + _dot(t, h3_s[2, :, col(n)]))
        return _Task(192.0, run)

    def put_split3(rb, cs, g):
        hi, mid, lo = _split3(g)
        h3_s[0, rb, cs] = hi
        h3_s[1, rb, cs] = mid
        h3_s[2, rb, cs] = lo

    def gla_logit(n):
        def run():
            qg_s[:, col(n)] = _dot(ga_s[...], wglr_ref[:, col(n)]) + bg_ref[:, col(n)]
        return _Task(64.0, run)

    def gla_log(n):
        def fn(rb):
            put_split3(rb, col(n), _log_sigmoid(qg_s[rb, col(n)]) * (LOG2_E / GLA_TAU))
        return ew(300.0, fn)

    gla_in = [mm(store(ga_s, slice(None)), wga_ref, slice(None), cast=BF16, cost=128.0)]
    gla_in += [mm(store(kg_s, col(n)), win_ref, col(n, C_GK)) for n in tiles(GLA_K)]
    gla_in += [gla_logit(n) for n in tiles(GLA_K)]
    gla_in += [t for n in tiles(GLA_K) for t in (
        mm(lambda v, n=n: _store_values(w_s, n, v, GLA_DK, GLA_DV, nch), win_ref, col(n, C_GV), cast=BF16),
        gla_log(n),
        mm(lambda v, n=n: _store_values(w_s, n + GLA_K // COL_TILE, v, GLA_DK, GLA_DV, nch), win_ref,
           col(n + GLA_K // COL_TILE, C_GV), cast=BF16))]
    gla_in += [cumsum_task(bgl_s, n) for n in tiles(GLA_K)]
    gla_in += [mm(store(qg_s, col(n)), win_ref, col(n, C_GQ)) for n in tiles(GLA_K)]
    _run(gla_in)

    def hq_ew(n):
        def fn(rb):
            qh_s[rb, col(n)] = _silu(qh_s[rb, col(n)])
        return ew(130.0, fn)

    def hf_ew(n):
        def fn(rb):
            sf, sfn = _sigmoid_pair(kh_s[rb, col(n)])
            lbn = lb[:, col(n)]
            kh_s[rb, col(n)] = (1.0 - lbn) * sfn
            put_split3(rb, col(n), jnp.log2(lbn + (1.0 - lbn) * sf))
        return ew(450.0, fn)

    def gate_ew(ref, act, n):
        def fn(rb):
            ref[rb, col(n)] = act(ref[rb, col(n)])
        return ew(130.0, fn)

    hg_in = [([mm(store(kh_s, col(n)), win_ref, col(n, C_HF)),
               mm(store(qh_s, col(n)), win_ref, col(n, C_HQ))],
              [hf_ew(n), hq_ew(n), cumsum_task(bh_s, n)]) for n in tiles(HG_K)]
    gates = [([mm(store(sgr_s, col(n)), win_ref, col(n, C_GR))], [gate_ew(sgr_s, _silu, n)]) for n in tiles(GLA_V)]
    gates += [([mm(store(mga_s, col(n)), win_ref, col(n, C_MA))], [gate_ew(mga_s, _sigmoid, n)])
              for n in tiles(D_MODEL)]
    gates_b = [([mm(store(shr_s, col(n)), win_ref, col(n, C_HR))], [gate_ew(shr_s, _silu, n)]) for n in tiles(HG_V)]
    hg_v = [mm(lambda v, n=n: _store_values(w_s, n, v, HG_DK, HG_DV, nch), win_ref, col(n, C_HI), cast=BF16)
            for n in tiles(HG_V)]
    mgb = [([mm(store(mgb_s, col(n)), win_ref, col(n, C_MB))], [gate_ew(mgb_s, _sigmoid, n)]) for n in tiles(D_MODEL)]

    ra_ops, ra_rest = _recurrence_tasks(gla, qa_s, kd_s, w_s, u_s, nch)
    _run(_merge(ra_ops + ra_rest, _lagged(hg_in + gates)))

    def gla_norm(n):
        def fn(rb):
            g_s[rb, col(n)] = (_head_rmsnorm(og_s[rb, col(n)], gng_ref[:, col(n)], GLA_DV) * sgr_s[rb, col(n)]).astype(BF16)
        return ew(200.0, fn)

    def gla_scale(n):
        def fn(rb):
            p_s[rb, col(n)] = p_s[rb, col(n)] * mga_s[rb, col(n)]
        return ew(50.0, fn)

    gla_post = [gla_norm(n) for n in tiles(GLA_V)]
    for n in tiles(D_MODEL):
        gla_post += [mm(store(p_s, col(n)), wbg_ref, col(n), lhs=g_s), gla_scale(n)]
    rb_ops, rb_rest = _recurrence_tasks(hgrn, qa_s, kd_s, w_s, u_s, nch)
    _run(_merge(rb_ops, hg_v + _lagged(gates_b)))
    _run(_merge(rb_rest, gla_post))

    def hgrn_norm(n):
        def fn(rb):
            m_s[rb, col(n)] = (_head_rmsnorm(oh_s[rb, col(n)], hng_ref[:, col(n)], HG_DV) * shr_s[rb, col(n)]).astype(BF16)
        return ew(220.0, fn)

    def merge_ew(n):
        def fn(rb):
            g_s[rb, col(n)] = (p_s[rb, col(n)] + mgb_s[rb, col(n)] * sgr_s[rb, col(n)]).astype(BF16)
        return ew(60.0, fn)

    def final_ew():
        def fn(rb):
            z = DEEPNORM_ALPHA * x_ref[0, rb, :] + shr_s[rb, :]
            y_ref[0, rb, :] = _layernorm(z, lng_ref[...], lnb_ref[...])
        return ew(700.0, fn, step=LN_ROW_BLOCK)

    _run(_merge([hgrn_norm(n) for n in tiles(HG_V)], _lagged(mgb)))
    _run([mm(store(sgr_s, col(n)), wbh_ref, col(n), lhs=m_s) for n in tiles(D_MODEL)])
    _run([merge_ew(n) for n in tiles(D_MODEL)])
    _run([mm(store(shr_s, col(n)), wout_ref, col(n), lhs=g_s) for n in tiles(D_MODEL)])
    _run([final_ew()])


def _whole(shape):
    return pl.BlockSpec(memory_space=pltpu.VMEM)


def _prompt_call(x, win, wga, wglr, bg, gng, wbg, lbp, hng, wbh, wout, lng, lnb):
    bsz, seq, d = x.shape
    tb = TOK_BLOCK
    nch = tb // CHUNK
    weights = (win, wga, wglr, bg, gng, wbg, lbp, hng, wbh, wout, lng, lnb)
    return pl.pallas_call(
        _prompt_kernel,
        grid=(bsz, seq // tb),
        in_specs=[pl.BlockSpec((1, tb, d), lambda b, t: (b, t, 0))] + [_whole(w.shape) for w in weights],
        out_specs=[
            pl.BlockSpec((1, tb, d), lambda b, t: (b, t, 0)),
            pl.BlockSpec((1, 1, GLA_H, GLA_DK, GLA_DV), lambda b, t: (0, b, 0, 0, 0)),
            pl.BlockSpec((1, 1, HG_H, HG_DK, HG_DV), lambda b, t: (0, b, 0, 0, 0)),
        ],
        out_shape=[
            jax.ShapeDtypeStruct((bsz, seq, d), F32),
            jax.ShapeDtypeStruct((DEPTH, bsz, GLA_H, GLA_DK, GLA_DV), F32),
            jax.ShapeDtypeStruct((DEPTH, bsz, HG_H, HG_DK, HG_DV), F32),
        ],
        scratch_shapes=[
            pltpu.VMEM((tb, d), BF16),
            pltpu.VMEM((tb, tb), BF16),
            pltpu.VMEM((tb, LANES), BF16),
            pltpu.VMEM((tb, GLA_K), F32),
            pltpu.VMEM((tb, GLA_K), F32),
            pltpu.VMEM((GLA_H, tb, LANES), F32),
            pltpu.VMEM((tb, HG_K), F32),
            pltpu.VMEM((tb, HG_K), F32),
            pltpu.VMEM((HG_H, tb, LANES), F32),
            pltpu.VMEM((3, tb, HG_K), BF16),
            pltpu.VMEM((HG_H, tb, 2 * LANES), BF16),
            pltpu.VMEM((HG_H, tb, LANES), BF16),
            pltpu.VMEM((GLA_H * nch, 2 * LANES, GLA_DV), BF16),
            pltpu.VMEM((GLA_H * nch, GLA_DK, GLA_DV), F32),
            pltpu.VMEM((tb, GLA_V), F32),
            pltpu.VMEM((tb, HG_V), F32),
            pltpu.VMEM((tb, GLA_V), F32),
            pltpu.VMEM((tb, HG_V), F32),
            pltpu.VMEM((tb, d), F32),
            pltpu.VMEM((tb, d), F32),
            pltpu.VMEM((tb, d), BF16),
            pltpu.VMEM((tb, d), BF16),
            pltpu.VMEM((tb, d), F32),
        ],
        compiler_params=pltpu.CompilerParams(
            dimension_semantics=("arbitrary", "arbitrary"), vmem_limit_bytes=VMEM_LIMIT),
        name="prompt_layer",
    )(x, *weights)


def _sample_proj_kernel(x_ref, win_ref, wga_ref, wglr_ref, bg_ref, lbp_ref,
                        eg_ref, kg_ref, qg_ref, vg_ref, eh_ref, kh_ref, qh_ref, vh_ref,
                        sgr_ref, shr_ref, mga_ref, mgb_ref):
    xb = x_ref[...].astype(BF16)
    proj = lambda base, width: _dot(xb, win_ref[:, base:base + width])
    qg_ref[...] = proj(C_GQ, GLA_K) * (GLA_DK ** -0.5)
    kg_ref[...] = proj(C_GK, GLA_K)
    vg_ref[...] = proj(C_GV, GLA_V)
    sgr_ref[...] = _silu(proj(C_GR, GLA_V))
    ga = _dot(xb, wga_ref[...]).astype(BF16)
    a_logit = _dot(ga, wglr_ref[...]) + bg_ref[...]
    eg_ref[...] = jnp.exp(_log_sigmoid(a_logit) * (1.0 / GLA_TAU))
    lb = _lower_bound(lbp_ref[...])
    qh_ref[...] = _silu(proj(C_HQ, HG_K))
    sf, sfn = _sigmoid_pair(proj(C_HF, HG_K))
    eh_ref[...] = jnp.exp(jnp.log(lb + (1.0 - lb) * sf))
    kh_ref[...] = (1.0 - lb) * sfn
    vh_ref[...] = proj(C_HI, HG_V)
    shr_ref[...] = _silu(proj(C_HR, HG_V))
    mga_ref[...] = _sigmoid(proj(C_MA, D_MODEL))
    mgb_ref[...] = _sigmoid(proj(C_MB, D_MODEL))


def _sample_proj_call(x, win, wga, wglr, bg, lbp):
    n = x.shape[0]
    widths = (GLA_K, GLA_K, GLA_K, GLA_V, HG_K, HG_K, HG_K, HG_V, GLA_V, HG_V, D_MODEL, D_MODEL)
    args = (x, win, wga, wglr, bg, lbp)
    return pl.pallas_call(
        _sample_proj_kernel,
        in_specs=[_whole(a.shape) for a in args],
        out_specs=[_whole((n, w)) for w in widths],
        out_shape=[jax.ShapeDtypeStruct((n, w), F32) for w in widths],
        compiler_params=pltpu.CompilerParams(vmem_limit_bytes=VMEM_LIMIT),
        name="sample_proj",
    )(*args)


def _state_update(e, k, q, v, s_ref, snew_ref, o_ref, nheads, dk, dv):
    pad_rows = BF16_SUBLANES - SEQ_BLOCK
    row = lax.broadcasted_iota(jnp.int32, (BF16_SUBLANES, dk), 0)
    padded = lambda a: jnp.concatenate([a, jnp.zeros((pad_rows, a.shape[1]), a.dtype)], axis=0)
    ks = lambda h: slice(h * dk, (h + 1) * dk)
    vs = lambda h: slice(h * dv, (h + 1) * dv)
    for h in range(nheads):
        k16 = padded(k[:, ks(h)])
        v16 = padded(v[:, vs(h)]).astype(BF16)
        for j in range(SEQ_BLOCK):
            snew_ref[j, h] = _dot_tn(jnp.where(row == j, k16, 0.0).astype(BF16), v16)
    for h in range(nheads):
        ec = _columns(e[:, ks(h)])
        for j in range(SEQ_BLOCK):
            snew_ref[j, h] = s_ref[j, h] * ec[:, j:j + 1] + snew_ref[j, h]
    for h in range(nheads):
        q16 = padded(q[:, ks(h)]).astype(BF16)
        for j in range(SEQ_BLOCK):
            o_ref[j:j + 1, vs(h)] = _dot(q16, snew_ref[j, h].astype(BF16))[j:j + 1, :]


def _sample_state_kernel(eg_ref, kg_ref, qg_ref, vg_ref, eh_ref, kh_ref, qh_ref, vh_ref, sg_ref, sh_ref,
                         sgn_ref, shn_ref, oa_ref, ob_ref):
    _state_update(eg_ref[...], kg_ref[...], qg_ref[...], vg_ref[...], sg_ref, sgn_ref, oa_ref,
                  GLA_H, GLA_DK, GLA_DV)
    _state_update(eh_ref[...], kh_ref[...], qh_ref[...], vh_ref[...], sh_ref, shn_ref, ob_ref,
                  HG_H, HG_DK, HG_DV)


def _sample_state_call(eg, kg, qg, vg, eh, kh, qh, vh, sg, sh):
    n = sg.shape[0]
    sb = SEQ_BLOCK
    row = lambda w: pl.BlockSpec((sb, w), lambda i: (i, 0))
    gspec = pl.BlockSpec((sb, GLA_H, GLA_DK, GLA_DV), lambda i: (i, 0, 0, 0))
    hspec = pl.BlockSpec((sb, HG_H, HG_DK, HG_DV), lambda i: (i, 0, 0, 0))
    return pl.pallas_call(
        _sample_state_kernel,
        grid=(n // sb,),
        in_specs=[row(GLA_K), row(GLA_K), row(GLA_K), row(GLA_V), row(HG_K), row(HG_K), row(HG_K), row(HG_V),
                  gspec, hspec],
        out_specs=[gspec, hspec, row(GLA_V), row(HG_V)],
        out_shape=[jax.ShapeDtypeStruct(sg.shape, F32), jax.ShapeDtypeStruct(sh.shape, F32),
                   jax.ShapeDtypeStruct((n, GLA_V), F32), jax.ShapeDtypeStruct((n, HG_V), F32)],
        compiler_params=pltpu.CompilerParams(dimension_semantics=("arbitrary",), vmem_limit_bytes=VMEM_LIMIT),
        name="sample_state",
    )(eg, kg, qg, vg, eh, kh, qh, vh, sg, sh)


def _sample_post_kernel(x_ref, oa_ref, ob_ref, sgr_ref, shr_ref, mga_ref, mgb_ref,
                        gng_ref, wbg_ref, hng_ref, wbh_ref, wout_ref, lng_ref, lnb_ref, y_ref):
    ga = (_head_rmsnorm(oa_ref[...], gng_ref[...], GLA_DV) * sgr_ref[...]).astype(BF16)
    gb = (_head_rmsnorm(ob_ref[...], hng_ref[...], HG_DV) * shr_ref[...]).astype(BF16)
    merged = mga_ref[...] * _dot(ga, wbg_ref[...]) + mgb_ref[...] * _dot(gb, wbh_ref[...])
    z = DEEPNORM_ALPHA * x_ref[...] + _dot(merged.astype(BF16), wout_ref[...])
    y_ref[...] = _layernorm(z, lng_ref[...], lnb_ref[...])


def _sample_post_call(*args):
    n = args[0].shape[0]
    return pl.pallas_call(
        _sample_post_kernel,
        in_specs=[_whole(a.shape) for a in args],
        out_specs=_whole((n, D_MODEL)),
        out_shape=jax.ShapeDtypeStruct((n, D_MODEL), F32),
        compiler_params=pltpu.CompilerParams(vmem_limit_bytes=VMEM_LIMIT),
        name="sample_post",
    )(*args)


def _weight_prep_kernel(wt_ref, gat_ref, win_ref, wga_ref):
    win_ref[...] = wt_ref[...].T.astype(BF16)

    @pl.when(pl.program_id(0) == 0)
    def _():
        d = gat_ref.shape[1]
        ga = jnp.concatenate([gat_ref[...], jnp.zeros((LANES - GLA_RANK, d), F32)], axis=0)
        wga_ref[...] = ga.T.astype(BF16)


def _weight_prep_call(w_in):
    _, d, width = w_in.shape
    wt = jnp.swapaxes(w_in, 1, 2)[0]
    blk = WEIGHT_PREP_COLS
    assert C_HQ % blk == 0 and (width - GLA_RANK) % blk == 0
    src_row = lambda i: (i * (blk // GLA_RANK) + (i >= C_HQ // blk).astype(jnp.int32)) * GLA_RANK
    return pl.pallas_call(
        _weight_prep_kernel,
        grid=((width - GLA_RANK) // blk,),
        in_specs=[pl.BlockSpec((pl.Element(blk), pl.Element(d)), lambda i: (src_row(i), 0)),
                  pl.BlockSpec((pl.Element(GLA_RANK), pl.Element(d)), lambda i: (C_HQ, 0))],
        out_specs=[pl.BlockSpec((d, blk), lambda i: (0, i)),
                   pl.BlockSpec((d, LANES), lambda i: (0, 0))],
        out_shape=[jax.ShapeDtypeStruct((d, width - GLA_RANK), BF16), jax.ShapeDtypeStruct((d, LANES), BF16)],
        compiler_params=pltpu.CompilerParams(dimension_semantics=("arbitrary",), vmem_limit_bytes=VMEM_LIMIT),
        name="weight_prep",
    )(wt, wt)


def kernel(x_prompt, x_sample, state_gla, state_hgrn, w_in, w_gate_lr, b_gate_lr, gla_norm_g, w_br_gla,
           hgrn_lb_param, hgrn_norm_g, w_br_hgrn, w_out, ln_g, ln_b):
    assert w_in.shape[0] == DEPTH and x_sample.shape[1] == 1
    win, wga = _weight_prep_call(w_in)
    wglr = jnp.pad(w_gate_lr[0], ((0, LANES - GLA_RANK), (0, 0))).astype(BF16)
    bg = b_gate_lr[0].reshape(1, GLA_K)
    gng = gla_norm_g[0].reshape(1, GLA_V)
    hng = hgrn_norm_g[0].reshape(1, HG_V)
    wbg = w_br_gla[0].astype(BF16)
    wbh = w_br_hgrn[0].astype(BF16)
    wout = w_out[0].astype(BF16)
    lng = ln_g[0].reshape(1, D_MODEL)
    lnb = ln_b[0].reshape(1, D_MODEL)
    lbp = hgrn_lb_param

    y_prompt, gla_p, hgrn_p = _prompt_call(x_prompt, win, wga, wglr, bg, gng, wbg, lbp, hng, wbh, wout, lng, lnb)

    xs = x_sample[:, 0, :]
    eg, kg, qg, vg, eh, kh, qh, vh, sgr, shr, mga, mgb = _sample_proj_call(xs, win, wga, wglr, bg, lbp)
    gla_s, hgrn_s, oa, ob = _sample_state_call(eg, kg, qg, vg, eh, kh, qh, vh, state_gla[0], state_hgrn[0])
    ys = _sample_post_call(xs, oa, ob, sgr, shr, mga, mgb, gng, wbg, hng, wbh, wout, lng, lnb)
    return (y_prompt, ys[:, None, :], gla_p, hgrn_p, gla_s[None], hgrn_s[None])
```

```python
from typing import Any, NamedTuple

import jax
import jax.numpy as jnp
from jax import lax
from jax.experimental import pallas as pl
from jax.experimental.pallas import tpu as pltpu

F32 = jnp.float32
BF16 = jnp.bfloat16

D_MODEL = 1024
GLA_H, GLA_DK, GLA_DV = 4, 128, 256
HG_H, HG_DK, HG_DV = 8, 128, 128
GLA_RANK = 16
GLA_TAU = 16.0
LOG2_E = 1.4426950408889634
GLA_K = GLA_H * GLA_DK
GLA_V = GLA_H * GLA_DV
HG_K = HG_H * HG_DK
HG_V = HG_H * HG_DV
C_GQ, C_GK, C_GV, C_GR = 0, GLA_K, 2 * GLA_K, 2 * GLA_K + GLA_V
C_HQ = C_GR + GLA_V
C_HF, C_HI, C_HR = C_HQ + HG_K, C_HQ + 2 * HG_K, C_HQ + 2 * HG_K + HG_V
C_MA = C_HR + HG_V
C_MB = C_MA + D_MODEL
CHUNK = 64
SUB = 16
NORM_EPS = 1e-5
DEPTH = 1
DEEPNORM_ALPHA = (2.0 * DEPTH) ** 0.25
LANES = 128
BF16_SUBLANES = 16
TOK_BLOCK = 256
COL_TILE = 256
ROW_BLOCK = 64
LN_ROW_BLOCK = 16
WEIGHT_PREP_COLS = 1024
VMEM_LIMIT = 60 * 1024 * 1024


def _dot(a, b):
    return jnp.dot(a, b, preferred_element_type=F32)


def _dot_nt(a, b):
    return lax.dot_general(a, b, (((1,), (1,)), ((), ())), preferred_element_type=F32)


def _dot_tn(a, b):
    return lax.dot_general(a, b, (((0,), (0,)), ((), ())), preferred_element_type=F32)


def _sigmoid(x):
    return 1.0 / (1.0 + jnp.exp(-x))


def _sigmoid_pair(x):
    e = jnp.exp(-jnp.abs(x))
    s = 1.0 / (1.0 + e)
    t = e * s
    pos = x >= 0.0
    return jnp.where(pos, s, t), jnp.where(pos, t, s)


def _silu(x):
    return x * _sigmoid(x)


def _log_sigmoid(x):
    return jnp.minimum(x, 0.0) - jnp.log1p(jnp.exp(-jnp.abs(x)))


def _split3(x):
    hi = x.astype(BF16)
    r = x - hi.astype(F32)
    mid = r.astype(BF16)
    lo = (r - mid.astype(F32)).astype(BF16)
    return hi, mid, lo


def _block_tri(n):
    r = lax.broadcasted_iota(jnp.int32, (n, n), 0)
    c = lax.broadcasted_iota(jnp.int32, (n, n), 1)
    shift = CHUNK.bit_length() - 1
    keep = (c <= r) & (jnp.right_shift(r, shift) == jnp.right_shift(c, shift))
    return jnp.where(keep, 1.0, 0.0).astype(BF16)


def _lower_bound(lbp):
    rows = [lbp[i:i + 1, :] for i in range(lbp.shape[0])]
    m = rows[0]
    for r in rows[1:]:
        m = jnp.maximum(m, r)
    es = [jnp.exp(r - m) for r in rows]
    tot = es[0]
    for e in es[1:]:
        tot = tot + e
    return es[0] / tot


def _score_mask():
    t = lax.broadcasted_iota(jnp.int32, (CHUNK, LANES), 0)
    lane = lax.broadcasted_iota(jnp.int32, (CHUNK, LANES), 1)
    s = lane - CHUNK * (jnp.right_shift(t, SUB.bit_length() - 1) & 1)
    return (s >= 0) & (s <= t)


def _intra_scores(q, k, b, keep):
    dk = q.shape[1]
    nsub = CHUNK // SUB
    qs, ks = [], []
    for i in range(nsub):
        lo, hi = i * SUB, (i + 1) * SUB
        bs = b[lo:lo + 1, :]
        qs.append(q[lo:hi] * jnp.exp2(b[lo:hi] - bs))
        ks.append((k[0:hi] * jnp.exp2(bs - b[0:hi])).astype(BF16))
        if hi < CHUNK:
            ks.append(jnp.zeros((CHUNK - hi, dk), BF16))
    prod = _dot_nt(jnp.concatenate(qs, axis=0).astype(BF16), jnp.concatenate(ks, axis=0))
    per = LANES // CHUNK
    a = jnp.concatenate([prod[i * SUB:(i + 1) * SUB, (i // per) * LANES:(i // per + 1) * LANES]
                         for i in range(nsub)], axis=0)
    return jnp.where(keep, a, 0.0)


def _columns(rows):
    r = rows.shape[0]
    if r < LANES:
        rows = jnp.concatenate([rows, jnp.zeros((LANES - r, LANES), F32)], axis=0)
    return rows.T


def _head_rmsnorm(o, gain, dv):
    outs = []
    for j in range(o.shape[1] // dv):
        oj = o[:, j * dv:(j + 1) * dv]
        ms = jnp.mean(oj * oj, axis=-1, keepdims=True)
        outs.append(oj * lax.rsqrt(ms + NORM_EPS))
    on = outs[0] if len(outs) == 1 else jnp.concatenate(outs, axis=1)
    return on * gain


def _layernorm(z, g, b):
    mu = jnp.mean(z, axis=-1, keepdims=True)
    zc = z - mu
    var = jnp.mean(zc * zc, axis=-1, keepdims=True)
    return zc * lax.rsqrt(var + NORM_EPS) * g + b


class _Task(NamedTuple):
    cost: float
    run: Any


def _merge(*lists):
    totals = [sum(t.cost for t in l) or 1.0 for l in lists]
    pos = [0] * len(lists)
    done = [0.0] * len(lists)
    out = []
    for _ in range(sum(len(l) for l in lists)):
        live = [i for i in range(len(lists)) if pos[i] < len(lists[i])]
        j = min(live, key=lambda i: (done[i] + 0.5 * lists[i][pos[i]].cost) / totals[i])
        out.append(lists[j][pos[j]])
        done[j] += lists[j][pos[j]].cost
        pos[j] += 1
    return out


def _lagged(groups, lag=1):
    out = []
    for i, (mms, _) in enumerate(groups):
        out += mms
        if i >= lag:
            out += groups[i - lag][1]
    for _, ews in groups[max(len(groups) - lag, 0):]:
        out += ews
    return out


def _run(tasks):
    for t in tasks:
        t.run()


def _row_blocks(nrows, step=ROW_BLOCK):
    return [slice(r, r + step) for r in range(0, nrows, step)]


class _Branch(NamedTuple):
    heads: int
    dk: int
    dv: int
    scale: float
    q: Any
    k: Any
    b: Any
    o: Any
    state: Any


def _slab(buf, hc, dv):
    per = buf.shape[2] // dv
    return hc // per, slice((hc % per) * dv, (hc % per + 1) * dv)


def _store_decay(b_s, n, bt):
    per = COL_TILE // LANES
    for j in range(per):
        b_s[n * per + j] = bt[:, j * LANES:(j + 1) * LANES]


def _store_values(w_s, n, vt, dk, dv, nchunks):
    per = COL_TILE // dv
    for j in range(per):
        for ci in range(nchunks):
            i, cs = _slab(w_s, (n * per + j) * nchunks + ci, dv)
            v = vt[ci * CHUNK:(ci + 1) * CHUNK, j * dv:(j + 1) * dv]
            w_s[i, dk:dk + CHUNK, cs] = v
            w_s[i, dk + CHUNK:dk + 2 * CHUNK, cs] = v


def _recurrence_tasks(br, qa_s, kd_s, w_s, u_s, nchunks):
    nheads, dk, dv = br.heads, br.dk, br.dv
    keep = _score_mask()
    pairs = [(ci, h) for ci in range(nchunks) for h in range(nheads)]
    rows_of = lambda ci: slice(ci * CHUNK, (ci + 1) * CHUNK)
    cols = []

    def operands(ci, h):
        rows, kcols = rows_of(ci), slice(h * dk, (h + 1) * dk)
        q, k, b = br.q[rows, kcols], br.k[rows, kcols], br.b[h, rows, :]
        if br.scale != 1.0:
            q = q * br.scale
        qa_s[h, rows, 0:dk] = (q * jnp.exp2(b)).astype(BF16)
        kd_s[h, rows, :] = (k * jnp.exp2(b[CHUNK - 1:CHUNK, :] - b)).astype(BF16)
        qa_s[h, rows, dk:dk + LANES] = _intra_scores(q, k, b, keep).astype(BF16)

    def outer(ci, h):
        i, cs = _slab(w_s, h * nchunks + ci, dv)
        u_s[i, :, cs] = _dot_tn(kd_s[h, rows_of(ci), :], w_s[i, dk:dk + CHUNK, cs])

    def decay_columns():
        half = CHUNK // 2
        for h in range(nheads):
            cols.append(_columns(jnp.exp2(br.b[h, pl.ds(half - 1, 2 * nchunks, stride=half), :])))

    def chain(ci, h):
        i, cs = _slab(w_s, h * nchunks + ci, dv)
        s = br.state[0, 0, h]
        w_s[i, 0:dk, cs] = s.astype(BF16)
        br.state[0, 0, h] = s * cols[h][:, 2 * ci + 1:2 * ci + 2] + u_s[i, :, cs]

    def output(ci, h):
        i, cs = _slab(w_s, h * nchunks + ci, dv)
        br.o[rows_of(ci), h * dv:(h + 1) * dv] = _dot(qa_s[h, rows_of(ci), :], w_s[i, :, cs])

    bind = lambda cost, fn: [_Task(cost, lambda ci=ci, h=h: fn(ci, h)) for ci, h in pairs]
    return bind(70, operands), (bind(32, outer) + [_Task(16 * nheads, decay_columns)]
                                + bind(8 + dv // 8, chain) + bind(16, output))


def _sample_state_tasks(e_ref, k_ref, q_ref, v_ref, s_ref, snew_ref, o_ref, nheads, dk, dv):
    nseq = s_ref.shape[0]
    first = lax.broadcasted_iota(jnp.int32, (BF16_SUBLANES, dk), 0) == 0
    ks = lambda h: slice(h * dk, (h + 1) * dk)
    vs = lambda h: slice(h * dv, (h + 1) * dv)
    tile = lambda ref, j, cols: jnp.broadcast_to(ref[0, j:j + 1, cols], (BF16_SUBLANES, cols.stop - cols.start))
    pairs = [(h, j) for h in range(nheads) for j in range(nseq)]

    def outer(h, j):
        kj = jnp.where(first, tile(k_ref, j, ks(h)), 0.0).astype(BF16)
        snew_ref[j, h] = _dot_tn(kj, tile(v_ref, j, vs(h)).astype(BF16))

    def update(h, j):
        ecol = jnp.broadcast_to(e_ref[0, j:j + 1, ks(h)], (dk, LANES)).T
        ecol = ecol if dv == LANES else jnp.concatenate([ecol] * (dv // LANES), axis=1)
        snew_ref[j, h] = s_ref[j, h] * ecol + snew_ref[j, h]

    def readout(h, j):
        o = _dot(tile(q_ref, j, ks(h)).astype(BF16), snew_ref[j, h].astype(BF16))
        o_ref[0, j:j + 1, vs(h)] = o[0:1, :]

    bind = lambda cost, fn: [_Task(cost, lambda h=h, j=j: fn(h, j)) for h, j in pairs]
    return bind(40.0, outer), bind(30.0, update), bind(20.0, readout)


def _prompt_kernel(x_ref, seg_ref, skg_ref, sqg_ref, svg_ref, seh_ref, skh_ref, sqh_ref, svh_ref, ssg_ref, ssh_ref,
                   win_ref, wga_ref, wglr_ref, bg_ref, gng_ref, wbg_ref,
                   lbp_ref, hng_ref, wbh_ref, wout_ref, lng_ref, lnb_ref,
                   y_ref, sg_ref, sh_ref, ssg_new_ref, ssh_new_ref, soa_ref, sob_ref,
                   xb_s, tri_s, ga_s, qg_s, kg_s, bgl_s, qh_s, kh_s, bh_s, h3_s, qa_s, kd_s, w_s, u_s,
                   og_s, oh_s, sgr_s, shr_s, mga_s, mgb_s, g_s, m_s, p_s):
    tb = TOK_BLOCK
    nch = tb // CHUNK
    tiles = lambda width: range(width // COL_TILE)
    col = lambda n, base=0: slice(base + n * COL_TILE, base + (n + 1) * COL_TILE)
    gla = _Branch(GLA_H, GLA_DK, GLA_DV, GLA_DK ** -0.5, qg_s, kg_s, bgl_s, og_s, sg_ref)
    hgrn = _Branch(HG_H, HG_DK, HG_DV, 1.0, qh_s, kh_s, bh_s, oh_s, sh_ref)
    mm_cost = 256.0

    @pl.when(pl.program_id(1) == 0)
    def _():
        sg_ref[...] = jnp.zeros_like(sg_ref)
        sh_ref[...] = jnp.zeros_like(sh_ref)
        tri_s[...] = _block_tri(tb)

    xb_s[...] = x_ref[0].astype(BF16)
    lb = _lower_bound(lbp_ref[...])

    def mm(dst, w_ref, wcols, lhs=xb_s, cast=None, cost=mm_cost):
        def run():
            r = _dot(lhs[...], w_ref[:, wcols])
            dst(r if cast is None else r.astype(cast))
        return _Task(cost, run)

    def store(ref, cols):
        def put(v):
            ref[:, cols] = v
        return put

    def ew(cost, fn, rows=(0, tb), step=ROW_BLOCK):
        def run():
            for r in range(rows[0], rows[1], step):
                fn(slice(r, r + step))
        return _Task(cost, run)

    def cumsum_task(b_s, n):
        def run():
            t = tri_s[...]
            _store_decay(b_s, n, _dot(t, h3_s[0, :, col(n)]) + _dot(t, h3_s[1, :, col(n)]) + _dot(t, h3_s[2, :, col(n)]))
        return _Task(192.0, run)

    def put_split3(rb, cs, g):
        hi, mid, lo = _split3(g)
        h3_s[0, rb, cs] = hi
        h3_s[1, rb, cs] = mid
        h3_s[2, rb, cs] = lo

    def gla_logit(n):
        def run():
            qg_s[:, col(n)] = _dot(ga_s[...], wglr_ref[:, col(n)]) + bg_ref[:, col(n)]
        return _Task(64.0, run)

    def gla_log(n):
        def fn(rb):
            put_split3(rb, col(n), _log_sigmoid(qg_s[rb, col(n)]) * (LOG2_E / GLA_TAU))
        return ew(300.0, fn)

    gla_in = [mm(store(ga_s, slice(None)), wga_ref, slice(None), cast=BF16, cost=128.0)]
    gla_in += [mm(store(kg_s, col(n)), win_ref, col(n, C_GK)) for n in tiles(GLA_K)]
    gla_in += [gla_logit(n) for n in tiles(GLA_K)]
    gla_in += [t for n in tiles(GLA_K) for t in (
        mm(lambda v, n=n: _store_values(w_s, n, v, GLA_DK, GLA_DV, nch), win_ref, col(n, C_GV), cast=BF16),
        gla_log(n),
        mm(lambda v, n=n: _store_values(w_s, n + GLA_K // COL_TILE, v, GLA_DK, GLA_DV, nch), win_ref,
           col(n + GLA_K // COL_TILE, C_GV), cast=BF16))]
    gla_in += [cumsum_task(bgl_s, n) for n in tiles(GLA_K)]
    gla_in += [mm(store(qg_s, col(n)), win_ref, col(n, C_GQ)) for n in tiles(GLA_K)]
    _run(gla_in)

    def hq_ew(n):
        def fn(rb):
            qh_s[rb, col(n)] = _silu(qh_s[rb, col(n)])
        return ew(130.0, fn)

    def hf_ew(n):
        def fn(rb):
            sf, sfn = _sigmoid_pair(kh_s[rb, col(n)])
            lbn = lb[:, col(n)]
            kh_s[rb, col(n)] = (1.0 - lbn) * sfn
            put_split3(rb, col(n), jnp.log2(lbn + (1.0 - lbn) * sf))
        return ew(450.0, fn)

    def gate_ew(ref, act, n):
        def fn(rb):
            ref[rb, col(n)] = act(ref[rb, col(n)])
        return ew(130.0, fn)

    hg_in = [([mm(store(kh_s, col(n)), win_ref, col(n, C_HF)),
               mm(store(qh_s, col(n)), win_ref, col(n, C_HQ))],
              [hf_ew(n), hq_ew(n), cumsum_task(bh_s, n)]) for n in tiles(HG_K)]
    gates = [([mm(store(sgr_s, col(n)), win_ref, col(n, C_GR))], [gate_ew(sgr_s, _silu, n)]) for n in tiles(GLA_V)]
    gates += [([mm(store(mga_s, col(n)), win_ref, col(n, C_MA))], [gate_ew(mga_s, _sigmoid, n)])
              for n in tiles(D_MODEL)]
    gates_b = [([mm(store(shr_s, col(n)), win_ref, col(n, C_HR))], [gate_ew(shr_s, _silu, n)]) for n in tiles(HG_V)]
    hg_v = [mm(lambda v, n=n: _store_values(w_s, n, v, HG_DK, HG_DV, nch), win_ref, col(n, C_HI), cast=BF16)
            for n in tiles(HG_V)]
    mgb = [([mm(store(mgb_s, col(n)), win_ref, col(n, C_MB))], [gate_ew(mgb_s, _sigmoid, n)]) for n in tiles(D_MODEL)]

    ra_ops, ra_rest = _recurrence_tasks(gla, qa_s, kd_s, w_s, u_s, nch)
    _run(_merge(ra_ops + ra_rest, _lagged(hg_in + gates)))

    def gla_norm(n):
        def fn(rb):
            g_s[rb, col(n)] = (_head_rmsnorm(og_s[rb, col(n)], gng_ref[:, col(n)], GLA_DV) * sgr_s[rb, col(n)]).astype(BF16)
        return ew(200.0, fn)

    def gla_scale(n):
        def fn(rb):
            p_s[rb, col(n)] = p_s[rb, col(n)] * mga_s[rb, col(n)]
        return ew(50.0, fn)

    gla_post = [gla_norm(n) for n in tiles(GLA_V)]
    for n in tiles(D_MODEL):
        gla_post += [mm(store(p_s, col(n)), wbg_ref, col(n), lhs=g_s), gla_scale(n)]
    rb_ops, rb_rest = _recurrence_tasks(hgrn, qa_s, kd_s, w_s, u_s, nch)
    _run(_merge(rb_ops, hg_v + _lagged(gates_b)))
    _run(_merge(rb_rest, gla_post))

    def hgrn_norm(n):
        def fn(rb):
            m_s[rb, col(n)] = (_head_rmsnorm(oh_s[rb, col(n)], hng_ref[:, col(n)], HG_DV) * shr_s[rb, col(n)]).astype(BF16)
        return ew(220.0, fn)

    def merge_ew(n):
        def fn(rb):
            g_s[rb, col(n)] = (p_s[rb, col(n)] + mgb_s[rb, col(n)] * sgr_s[rb, col(n)]).astype(BF16)
        return ew(60.0, fn)

    def final_ew(rows):
        def fn(rb):
            z = DEEPNORM_ALPHA * x_ref[0, rb, :] + shr_s[rb, :]
            y_ref[0, rb, :] = _layernorm(z, lng_ref[...], lnb_ref[...])
        return ew(180.0, fn, rows=rows, step=LN_ROW_BLOCK)

    sample = []
    for args in ((seg_ref, skg_ref, sqg_ref, svg_ref, ssg_ref, ssg_new_ref, soa_ref, GLA_H, GLA_DK, GLA_DV),
                 (seh_ref, skh_ref, sqh_ref, svh_ref, ssh_ref, ssh_new_ref, sob_ref, HG_H, HG_DK, HG_DV)):
        outer, update, readout = _sample_state_tasks(*args)
        sample += outer + update + readout

    _run(_merge([hgrn_norm(n) for n in tiles(HG_V)], _lagged(mgb)))
    _run([mm(store(sgr_s, col(n)), wbh_ref, col(n), lhs=m_s) for n in tiles(D_MODEL)])
    _run([merge_ew(n) for n in tiles(D_MODEL)])
    tail = [mm(store(shr_s, col(n)), wout_ref, col(n), lhs=g_s) for n in tiles(D_MODEL)]
    tail += [final_ew((r, r + ROW_BLOCK)) for r in range(0, tb, ROW_BLOCK)]
    _run(_merge(tail, sample))


def _whole(shape):
    return pl.BlockSpec(memory_space=pltpu.VMEM)


def _prompt_call(x, sample_rows, sample_states, win, wga, wglr, bg, gng, wbg, lbp, hng, wbh, wout, lng, lnb):
    bsz, seq, d = x.shape
    tb = TOK_BLOCK
    nch = tb // CHUNK
    per_row = seq // tb
    steps = bsz * per_row
    nseq = sample_states[0].shape[0] // steps
    assert nseq * steps == sample_states[0].shape[0] and all(r.shape[:2] == (steps, nseq) for r in sample_rows)
    weights = (win, wga, wglr, bg, gng, wbg, lbp, hng, wbh, wout, lng, lnb)
    lin = lambda b, t: b * per_row + t
    row_spec = lambda r: pl.BlockSpec((1, nseq, r.shape[2]), lambda b, t: (lin(b, t), 0, 0))
    state_spec = lambda s: pl.BlockSpec((nseq,) + s.shape[1:], lambda b, t: (lin(b, t), 0, 0, 0))
    return pl.pallas_call(
        _prompt_kernel,
        grid=(bsz, per_row),
        in_specs=[pl.BlockSpec((1, tb, d), lambda b, t: (b, t, 0))] + [row_spec(r) for r in sample_rows]
        + [state_spec(s) for s in sample_states] + [_whole(w.shape) for w in weights],
        out_specs=[
            pl.BlockSpec((1, tb, d), lambda b, t: (b, t, 0)),
            pl.BlockSpec((1, 1, GLA_H, GLA_DK, GLA_DV), lambda b, t: (0, b, 0, 0, 0)),
            pl.BlockSpec((1, 1, HG_H, HG_DK, HG_DV), lambda b, t: (0, b, 0, 0, 0)),
            state_spec(sample_states[0]), state_spec(sample_states[1]),
            pl.BlockSpec((1, nseq, GLA_V), lambda b, t: (lin(b, t), 0, 0)),
            pl.BlockSpec((1, nseq, HG_V), lambda b, t: (lin(b, t), 0, 0)),
        ],
        out_shape=[
            jax.ShapeDtypeStruct((bsz, seq, d), F32),
            jax.ShapeDtypeStruct((DEPTH, bsz, GLA_H, GLA_DK, GLA_DV), F32),
            jax.ShapeDtypeStruct((DEPTH, bsz, HG_H, HG_DK, HG_DV), F32),
            jax.ShapeDtypeStruct(sample_states[0].shape, F32),
            jax.ShapeDtypeStruct(sample_states[1].shape, F32),
            jax.ShapeDtypeStruct((steps, nseq, GLA_V), F32),
            jax.ShapeDtypeStruct((steps, nseq, HG_V), F32),
        ],
        scratch_shapes=[
            pltpu.VMEM((tb, d), BF16),
            pltpu.VMEM((tb, tb), BF16),
            pltpu.VMEM((tb, LANES), BF16),
            pltpu.VMEM((tb, GLA_K), F32),
            pltpu.VMEM((tb, GLA_K), F32),
            pltpu.VMEM((GLA_H, tb, LANES), F32),
            pltpu.VMEM((tb, HG_K), F32),
            pltpu.VMEM((tb, HG_K), F32),
            pltpu.VMEM((HG_H, tb, LANES), F32),
            pltpu.VMEM((3, tb, HG_K), BF16),
            pltpu.VMEM((HG_H, tb, 2 * LANES), BF16),
            pltpu.VMEM((HG_H, tb, LANES), BF16),
            pltpu.VMEM((GLA_H * nch, 2 * LANES, GLA_DV), BF16),
            pltpu.VMEM((GLA_H * nch, GLA_DK, GLA_DV), F32),
            pltpu.VMEM((tb, GLA_V), F32),
            pltpu.VMEM((tb, HG_V), F32),
            pltpu.VMEM((tb, GLA_V), F32),
            pltpu.VMEM((tb, HG_V), F32),
            pltpu.VMEM((tb, d), F32),
            pltpu.VMEM((tb, d), F32),
            pltpu.VMEM((tb, d), BF16),
            pltpu.VMEM((tb, d), BF16),
            pltpu.VMEM((tb, d), F32),
        ],
        compiler_params=pltpu.CompilerParams(
            dimension_semantics=("arbitrary", "arbitrary"), vmem_limit_bytes=VMEM_LIMIT),
        name="prompt_layer",
    )(x, *sample_rows, *sample_states, *weights)


def _sample_proj_kernel(x_ref, win_ref, wga_ref, wglr_ref, bg_ref, lbp_ref,
                        eg_ref, kg_ref, qg_ref, vg_ref, eh_ref, kh_ref, qh_ref, vh_ref,
                        sgr_ref, shr_ref, mga_ref, mgb_ref):
    xb = x_ref[...].astype(BF16)
    proj = lambda base, width: _dot(xb, win_ref[:, base:base + width])
    qg_ref[...] = proj(C_GQ, GLA_K) * (GLA_DK ** -0.5)
    kg_ref[...] = proj(C_GK, GLA_K)
    vg_ref[...] = proj(C_GV, GLA_V)
    sgr_ref[...] = _silu(proj(C_GR, GLA_V))
    ga = _dot(xb, wga_ref[...]).astype(BF16)
    a_logit = _dot(ga, wglr_ref[...]) + bg_ref[...]
    eg_ref[...] = jnp.exp(_log_sigmoid(a_logit) * (1.0 / GLA_TAU))
    lb = _lower_bound(lbp_ref[...])
    qh_ref[...] = _silu(proj(C_HQ, HG_K))
    sf, sfn = _sigmoid_pair(proj(C_HF, HG_K))
    eh_ref[...] = jnp.exp(jnp.log(lb + (1.0 - lb) * sf))
    kh_ref[...] = (1.0 - lb) * sfn
    vh_ref[...] = proj(C_HI, HG_V)
    shr_ref[...] = _silu(proj(C_HR, HG_V))
    mga_ref[...] = _sigmoid(proj(C_MA, D_MODEL))
    mgb_ref[...] = _sigmoid(proj(C_MB, D_MODEL))


def _sample_proj_call(x, win, wga, wglr, bg, lbp):
    n = x.shape[0]
    widths = (GLA_K, GLA_K, GLA_K, GLA_V, HG_K, HG_K, HG_K, HG_V, GLA_V, HG_V, D_MODEL, D_MODEL)
    args = (x, win, wga, wglr, bg, lbp)
    return pl.pallas_call(
        _sample_proj_kernel,
        in_specs=[_whole(a.shape) for a in args],
        out_specs=[_whole((n, w)) for w in widths],
        out_shape=[jax.ShapeDtypeStruct((n, w), F32) for w in widths],
        compiler_params=pltpu.CompilerParams(vmem_limit_bytes=VMEM_LIMIT),
        name="sample_proj",
    )(*args)


def _sample_post_kernel(x_ref, oa_ref, ob_ref, sgr_ref, shr_ref, mga_ref, mgb_ref,
                        gng_ref, wbg_ref, hng_ref, wbh_ref, wout_ref, lng_ref, lnb_ref, y_ref):
    ga = (_head_rmsnorm(oa_ref[...], gng_ref[...], GLA_DV) * sgr_ref[...]).astype(BF16)
    gb = (_head_rmsnorm(ob_ref[...], hng_ref[...], HG_DV) * shr_ref[...]).astype(BF16)
    merged = mga_ref[...] * _dot(ga, wbg_ref[...]) + mgb_ref[...] * _dot(gb, wbh_ref[...])
    z = DEEPNORM_ALPHA * x_ref[...] + _dot(merged.astype(BF16), wout_ref[...])
    y_ref[...] = _layernorm(z, lng_ref[...], lnb_ref[...])


def _sample_post_call(*args):
    n = args[0].shape[0]
    return pl.pallas_call(
        _sample_post_kernel,
        in_specs=[_whole(a.shape) for a in args],
        out_specs=_whole((n, D_MODEL)),
        out_shape=jax.ShapeDtypeStruct((n, D_MODEL), F32),
        compiler_params=pltpu.CompilerParams(vmem_limit_bytes=VMEM_LIMIT),
        name="sample_post",
    )(*args)


def _weight_prep_kernel(wt_ref, gat_ref, win_ref, wga_ref):
    win_ref[...] = wt_ref[...].T.astype(BF16)

    @pl.when(pl.program_id(0) == 0)
    def _():
        d = gat_ref.shape[1]
        ga = jnp.concatenate([gat_ref[...], jnp.zeros((LANES - GLA_RANK, d), F32)], axis=0)
        wga_ref[...] = ga.T.astype(BF16)


def _weight_prep_call(w_in):
    _, d, width = w_in.shape
    wt = jnp.swapaxes(w_in, 1, 2)[0]
    blk = WEIGHT_PREP_COLS
    assert C_HQ % blk == 0 and (width - GLA_RANK) % blk == 0
    src_row = lambda i: (i * (blk // GLA_RANK) + (i >= C_HQ // blk).astype(jnp.int32)) * GLA_RANK
    return pl.pallas_call(
        _weight_prep_kernel,
        grid=((width - GLA_RANK) // blk,),
        in_specs=[pl.BlockSpec((pl.Element(blk), pl.Element(d)), lambda i: (src_row(i), 0)),
                  pl.BlockSpec((pl.Element(GLA_RANK), pl.Element(d)), lambda i: (C_HQ, 0))],
        out_specs=[pl.BlockSpec((d, blk), lambda i: (0, i)),
                   pl.BlockSpec((d, LANES), lambda i: (0, 0))],
        out_shape=[jax.ShapeDtypeStruct((d, width - GLA_RANK), BF16), jax.ShapeDtypeStruct((d, LANES), BF16)],
        compiler_params=pltpu.CompilerParams(dimension_semantics=("arbitrary",), vmem_limit_bytes=VMEM_LIMIT),
        name="weight_prep",
    )(wt, wt)


def kernel(x_prompt, x_sample, state_gla, state_hgrn, w_in, w_gate_lr, b_gate_lr, gla_norm_g, w_br_gla,
           hgrn_lb_param, hgrn_norm_g, w_br_hgrn, w_out, ln_g, ln_b):
    assert w_in.shape[0] == DEPTH and x_sample.shape[1] == 1
    win, wga = _weight_prep_call(w_in)
    wglr = jnp.pad(w_gate_lr[0], ((0, LANES - GLA_RANK), (0, 0))).astype(BF16)
    bg = b_gate_lr[0].reshape(1, GLA_K)
    gng = gla_norm_g[0].reshape(1, GLA_V)
    hng = hgrn_norm_g[0].reshape(1, HG_V)
    wbg = w_br_gla[0].astype(BF16)
    wbh = w_br_hgrn[0].astype(BF16)
    wout = w_out[0].astype(BF16)
    lng = ln_g[0].reshape(1, D_MODEL)
    lnb = ln_b[0].reshape(1, D_MODEL)
    lbp = hgrn_lb_param

    xs = x_sample[:, 0, :]
    nsample = xs.shape[0]
    steps = x_prompt.shape[0] * (x_prompt.shape[1] // TOK_BLOCK)
    *rows, sgr, shr, mga, mgb = _sample_proj_call(xs, win, wga, wglr, bg, lbp)
    rows = [r.reshape(steps, nsample // steps, r.shape[1]) for r in rows]

    y_prompt, gla_p, hgrn_p, gla_s, hgrn_s, oa, ob = _prompt_call(
        x_prompt, rows, (state_gla[0], state_hgrn[0]), win, wga, wglr, bg, gng, wbg, lbp, hng, wbh, wout, lng, lnb)

    ys = _sample_post_call(xs, oa.reshape(nsample, GLA_V), ob.reshape(nsample, HG_V), sgr, shr, mga, mgb,
                           gng, wbg, hng, wbh, wout, lng, lnb)
    return (y_prompt, ys[:, None, :], gla_p, hgrn_p, gla_s[None], hgrn_s[None])
```

```python
from typing import Any, NamedTuple

import jax
import jax.numpy as jnp
from jax import lax
from jax.experimental import pallas as pl
from jax.experimental.pallas import tpu as pltpu

F32 = jnp.float32
BF16 = jnp.bfloat16

D_MODEL = 1024
GLA_H, GLA_DK, GLA_DV = 4, 128, 256
HG_H, HG_DK, HG_DV = 8, 128, 128
GLA_RANK = 16
GLA_TAU = 16.0
LOG2_E = 1.4426950408889634
GLA_K = GLA_H * GLA_DK
GLA_V = GLA_H * GLA_DV
HG_K = HG_H * HG_DK
HG_V = HG_H * HG_DV
C_GQ, C_GK, C_GV, C_GR = 0, GLA_K, 2 * GLA_K, 2 * GLA_K + GLA_V
C_HQ = C_GR + GLA_V
C_HF, C_HI, C_HR = C_HQ + HG_K, C_HQ + 2 * HG_K, C_HQ + 2 * HG_K + HG_V
C_MA = C_HR + HG_V
C_MB = C_MA + D_MODEL
CHUNK = 64
SUB = 16
NORM_EPS = 1e-5
DEPTH = 1
DEEPNORM_ALPHA = (2.0 * DEPTH) ** 0.25
LANES = 128
BF16_SUBLANES = 16
TOK_BLOCK = 256
COL_TILE = 256
ROW_BLOCK = 64
LN_ROW_BLOCK = 16
WEIGHT_PREP_COLS = 1024
VMEM_LIMIT = 60 * 1024 * 1024


def _dot(a, b):
    return jnp.dot(a, b, preferred_element_type=F32)


def _dot_nt(a, b):
    return lax.dot_general(a, b, (((1,), (1,)), ((), ())), preferred_element_type=F32)


def _dot_tn(a, b):
    return lax.dot_general(a, b, (((0,), (0,)), ((), ())), preferred_element_type=F32)


def _sigmoid(x):
    return 1.0 / (1.0 + jnp.exp(-x))


def _sigmoid_pair(x):
    e = jnp.exp(-jnp.abs(x))
    s = 1.0 / (1.0 + e)
    t = e * s
    pos = x >= 0.0
    return jnp.where(pos, s, t), jnp.where(pos, t, s)


def _silu(x):
    return x * _sigmoid(x)


def _log_sigmoid(x):
    return jnp.minimum(x, 0.0) - jnp.log1p(jnp.exp(-jnp.abs(x)))


def _split3(x):
    hi = x.astype(BF16)
    r = x - hi.astype(F32)
    mid = r.astype(BF16)
    lo = (r - mid.astype(F32)).astype(BF16)
    return hi, mid, lo


def _block_tri(n):
    r = lax.broadcasted_iota(jnp.int32, (n, n), 0)
    c = lax.broadcasted_iota(jnp.int32, (n, n), 1)
    shift = CHUNK.bit_length() - 1
    keep = (c <= r) & (jnp.right_shift(r, shift) == jnp.right_shift(c, shift))
    return jnp.where(keep, 1.0, 0.0).astype(BF16)


def _lower_bound(lbp):
    rows = [lbp[i:i + 1, :] for i in range(lbp.shape[0])]
    m = rows[0]
    for r in rows[1:]:
        m = jnp.maximum(m, r)
    es = [jnp.exp(r - m) for r in rows]
    tot = es[0]
    for e in es[1:]:
        tot = tot + e
    return es[0] / tot


def _score_mask():
    t = lax.broadcasted_iota(jnp.int32, (CHUNK, LANES), 0)
    lane = lax.broadcasted_iota(jnp.int32, (CHUNK, LANES), 1)
    s = lane - CHUNK * (jnp.right_shift(t, SUB.bit_length() - 1) & 1)
    return (s >= 0) & (s <= t)


def _intra_scores(q, k, b, keep):
    dk = q.shape[1]
    nsub = CHUNK // SUB
    qs, ks = [], []
    for i in range(nsub):
        lo, hi = i * SUB, (i + 1) * SUB
        bs = b[lo:lo + 1, :]
        qs.append(q[lo:hi] * jnp.exp2(b[lo:hi] - bs))
        ks.append((k[0:hi] * jnp.exp2(bs - b[0:hi])).astype(BF16))
        if hi < CHUNK:
            ks.append(jnp.zeros((CHUNK - hi, dk), BF16))
    prod = _dot_nt(jnp.concatenate(qs, axis=0).astype(BF16), jnp.concatenate(ks, axis=0))
    per = LANES // CHUNK
    a = jnp.concatenate([prod[i * SUB:(i + 1) * SUB, (i // per) * LANES:(i // per + 1) * LANES]
                         for i in range(nsub)], axis=0)
    return jnp.where(keep, a, 0.0)


def _columns(rows):
    r = rows.shape[0]
    if r < LANES:
        rows = jnp.concatenate([rows, jnp.zeros((LANES - r, LANES), F32)], axis=0)
    return rows.T


def _head_rmsnorm(o, gain, dv):
    outs = []
    for j in range(o.shape[1] // dv):
        oj = o[:, j * dv:(j + 1) * dv]
        ms = jnp.mean(oj * oj, axis=-1, keepdims=True)
        outs.append(oj * lax.rsqrt(ms + NORM_EPS))
    on = outs[0] if len(outs) == 1 else jnp.concatenate(outs, axis=1)
    return on * gain


def _layernorm(z, g, b):
    mu = jnp.mean(z, axis=-1, keepdims=True)
    zc = z - mu
    var = jnp.mean(zc * zc, axis=-1, keepdims=True)
    return zc * lax.rsqrt(var + NORM_EPS) * g + b


class _Task(NamedTuple):
    cost: float
    run: Any


def _merge(*lists):
    totals = [sum(t.cost for t in l) or 1.0 for l in lists]
    pos = [0] * len(lists)
    done = [0.0] * len(lists)
    out = []
    for _ in range(sum(len(l) for l in lists)):
        live = [i for i in range(len(lists)) if pos[i] < len(lists[i])]
        j = min(live, key=lambda i: (done[i] + 0.5 * lists[i][pos[i]].cost) / totals[i])
        out.append(lists[j][pos[j]])
        done[j] += lists[j][pos[j]].cost
        pos[j] += 1
    return out


def _lagged(groups, lag=1):
    out = []
    for i, (mms, _) in enumerate(groups):
        out += mms
        if i >= lag:
            out += groups[i - lag][1]
    for _, ews in groups[max(len(groups) - lag, 0):]:
        out += ews
    return out


def _run(tasks):
    for t in tasks:
        t.run()


def _row_blocks(nrows, step=ROW_BLOCK):
    return [slice(r, r + step) for r in range(0, nrows, step)]


class _Branch(NamedTuple):
    heads: int
    dk: int
    dv: int
    scale: float
    q: Any
    k: Any
    b: Any
    o: Any
    state: Any


def _slab(buf, hc, dv):
    per = buf.shape[2] // dv
    return hc // per, slice((hc % per) * dv, (hc % per + 1) * dv)


def _store_decay(b_s, n, bt):
    per = COL_TILE // LANES
    for j in range(per):
        b_s[n * per + j] = bt[:, j * LANES:(j + 1) * LANES]


def _store_values(w_s, n, vt, dk, dv, nchunks):
    per = COL_TILE // dv
    for j in range(per):
        for ci in range(nchunks):
            i, cs = _slab(w_s, (n * per + j) * nchunks + ci, dv)
            v = vt[ci * CHUNK:(ci + 1) * CHUNK, j * dv:(j + 1) * dv]
            w_s[i, dk:dk + CHUNK, cs] = v
            w_s[i, dk + CHUNK:dk + 2 * CHUNK, cs] = v


def _recurrence_tasks(br, qa_s, kd_s, w_s, u_s, nchunks):
    nheads, dk, dv = br.heads, br.dk, br.dv
    keep = _score_mask()
    pairs = [(ci, h) for ci in range(nchunks) for h in range(nheads)]
    rows_of = lambda ci: slice(ci * CHUNK, (ci + 1) * CHUNK)
    cols = []

    def operands(ci, h):
        rows, kcols = rows_of(ci), slice(h * dk, (h + 1) * dk)
        q, k, b = br.q[rows, kcols], br.k[rows, kcols], br.b[h, rows, :]
        if br.scale != 1.0:
            q = q * br.scale
        qa_s[h, rows, 0:dk] = (q * jnp.exp2(b)).astype(BF16)
        kd_s[h, rows, :] = (k * jnp.exp2(b[CHUNK - 1:CHUNK, :] - b)).astype(BF16)
        qa_s[h, rows, dk:dk + LANES] = _intra_scores(q, k, b, keep).astype(BF16)

    def outer(ci, h):
        i, cs = _slab(w_s, h * nchunks + ci, dv)
        u_s[i, :, cs] = _dot_tn(kd_s[h, rows_of(ci), :], w_s[i, dk:dk + CHUNK, cs])

    def decay_columns():
        half = CHUNK // 2
        for h in range(nheads):
            cols.append(_columns(jnp.exp2(br.b[h, pl.ds(half - 1, 2 * nchunks, stride=half), :])))

    def chain(ci, h):
        i, cs = _slab(w_s, h * nchunks + ci, dv)
        s = br.state[0, 0, h]
        w_s[i, 0:dk, cs] = s.astype(BF16)
        br.state[0, 0, h] = s * cols[h][:, 2 * ci + 1:2 * ci + 2] + u_s[i, :, cs]

    def output(ci, h):
        i, cs = _slab(w_s, h * nchunks + ci, dv)
        br.o[rows_of(ci), h * dv:(h + 1) * dv] = _dot(qa_s[h, rows_of(ci), :], w_s[i, :, cs])

    bind = lambda cost, fn: [_Task(cost, lambda ci=ci, h=h: fn(ci, h)) for ci, h in pairs]
    return bind(70, operands), (bind(32, outer) + [_Task(16 * nheads, decay_columns)]
                                + bind(8 + dv // 8, chain) + bind(16, output))


def _sample_state_tasks(e_ref, k_ref, q_ref, v_ref, s_ref, snew_ref, o_ref, nheads, dk, dv):
    nseq = s_ref.shape[0]
    first = lax.broadcasted_iota(jnp.int32, (BF16_SUBLANES, dk), 0) == 0
    ks = lambda h: slice(h * dk, (h + 1) * dk)
    vs = lambda h: slice(h * dv, (h + 1) * dv)
    tile = lambda ref, j, cols: jnp.broadcast_to(ref[0, j:j + 1, cols], (BF16_SUBLANES, cols.stop - cols.start))
    pairs = [(h, j) for h in range(nheads) for j in range(nseq)]

    def outer(h, j):
        kj = jnp.where(first, tile(k_ref, j, ks(h)), 0.0).astype(BF16)
        snew_ref[j, h] = _dot_tn(kj, tile(v_ref, j, vs(h)).astype(BF16))

    def update(h, j):
        ecol = jnp.broadcast_to(e_ref[0, j:j + 1, ks(h)], (dk, LANES)).T
        ecol = ecol if dv == LANES else jnp.concatenate([ecol] * (dv // LANES), axis=1)
        snew_ref[j, h] = s_ref[j, h] * ecol + snew_ref[j, h]

    def readout(h, j):
        o = _dot(tile(q_ref, j, ks(h)).astype(BF16), snew_ref[j, h].astype(BF16))
        o_ref[0, j:j + 1, vs(h)] = o[0:1, :]

    bind = lambda cost, fn: [_Task(cost, lambda h=h, j=j: fn(h, j)) for h, j in pairs]
    return bind(40.0, outer), bind(30.0, update), bind(20.0, readout)


def _prompt_kernel(x_ref, seg_ref, skg_ref, sqg_ref, svg_ref, seh_ref, skh_ref, sqh_ref, svh_ref, ssg_ref, ssh_ref,
                   win_ref, wga_ref, wglr_ref, bg_ref, gng_ref, wbg_ref,
                   lbp_ref, hng_ref, wbh_ref, wout_ref, lng_ref, lnb_ref,
                   y_ref, sg_ref, sh_ref, ssg_new_ref, ssh_new_ref, soa_ref, sob_ref,
                   xb_s, tri_s, ga_s, qg_s, kg_s, bgl_s, qh_s, kh_s, bh_s, h3_s, qa_s, kd_s, w_s, u_s,
                   og_s, oh_s, sgr_s, shr_s, mga_s, mgb_s, g_s, m_s, p_s):
    tb = TOK_BLOCK
    nch = tb // CHUNK
    tiles = lambda width: range(width // COL_TILE)
    col = lambda n, base=0: slice(base + n * COL_TILE, base + (n + 1) * COL_TILE)
    gla = _Branch(GLA_H, GLA_DK, GLA_DV, GLA_DK ** -0.5, qg_s, kg_s, bgl_s, og_s, sg_ref)
    hgrn = _Branch(HG_H, HG_DK, HG_DV, 1.0, qh_s, kh_s, bh_s, oh_s, sh_ref)
    mm_cost = 256.0

    @pl.when(pl.program_id(1) == 0)
    def _():
        sg_ref[...] = jnp.zeros_like(sg_ref)
        sh_ref[...] = jnp.zeros_like(sh_ref)
        tri_s[...] = _block_tri(tb)

    xb_s[...] = x_ref[0].astype(BF16)
    lb = _lower_bound(lbp_ref[...])

    def mm(dst, w_ref, wcols, lhs=xb_s, cast=None, cost=mm_cost):
        def run():
            r = _dot(lhs[...], w_ref[:, wcols])
            dst(r if cast is None else r.astype(cast))
        return _Task(cost, run)

    def store(ref, cols):
        def put(v):
            ref[:, cols] = v
        return put

    def ew(cost, fn, rows=(0, tb), step=ROW_BLOCK):
        def run():
            for r in range(rows[0], rows[1], step):
                fn(slice(r, r + step))
        return _Task(cost, run)

    def cumsum_task(b_s, n):
        def run():
            t = tri_s[...]
            _store_decay(b_s, n, _dot(t, h3_s[0, :, col(n)]) + _dot(t, h3_s[1, :, col(n)]) + _dot(t, h3_s[2, :, col(n)]))
        return _Task(192.0, run)

    def put_split3(rb, cs, g):
        hi, mid, lo = _split3(g)
        h3_s[0, rb, cs] = hi
        h3_s[1, rb, cs] = mid
        h3_s[2, rb, cs] = lo

    def gla_logit(n):
        def run():
            qg_s[:, col(n)] = _dot(ga_s[...], wglr_ref[:, col(n)]) + bg_ref[:, col(n)]
        return _Task(64.0, run)

    def gla_log(n):
        def fn(rb):
            put_split3(rb, col(n), _log_sigmoid(qg_s[rb, col(n)]) * (LOG2_E / GLA_TAU))
        return ew(300.0, fn)

    gla_in = [mm(store(ga_s, slice(None)), wga_ref, slice(None), cast=BF16, cost=128.0)]
    gla_in += [mm(store(kg_s, col(n)), win_ref, col(n, C_GK)) for n in tiles(GLA_K)]
    gla_in += [gla_logit(n) for n in tiles(GLA_K)]
    gla_in += [t for n in tiles(GLA_K) for t in (
        mm(lambda v, n=n: _store_values(w_s, n, v, GLA_DK, GLA_DV, nch), win_ref, col(n, C_GV), cast=BF16),
        gla_log(n),
        mm(lambda v, n=n: _store_values(w_s, n + GLA_K // COL_TILE, v, GLA_DK, GLA_DV, nch), win_ref,
           col(n + GLA_K // COL_TILE, C_GV), cast=BF16))]
    gla_in += [cumsum_task(bgl_s, n) for n in tiles(GLA_K)]
    gla_in += [mm(store(qg_s, col(n)), win_ref, col(n, C_GQ)) for n in tiles(GLA_K)]
    _run(gla_in)

    def hq_ew(n):
        def fn(rb):
            qh_s[rb, col(n)] = _silu(qh_s[rb, col(n)])
        return ew(130.0, fn)

    def hf_ew(n):
        def fn(rb):
            sf, sfn = _sigmoid_pair(kh_s[rb, col(n)])
            lbn = lb[:, col(n)]
            kh_s[rb, col(n)] = (1.0 - lbn) * sfn
            put_split3(rb, col(n), jnp.log2(lbn + (1.0 - lbn) * sf))
        return ew(450.0, fn)

    def gate_ew(ref, act, n):
        def fn(rb):
            ref[rb, col(n)] = act(ref[rb, col(n)])
        return ew(130.0, fn)

    hg_in = [([mm(store(kh_s, col(n)), win_ref, col(n, C_HF)),
               mm(store(qh_s, col(n)), win_ref, col(n, C_HQ))],
              [hf_ew(n), hq_ew(n), cumsum_task(bh_s, n)]) for n in tiles(HG_K)]
    gates = [([mm(store(sgr_s, col(n)), win_ref, col(n, C_GR))], [gate_ew(sgr_s, _silu, n)]) for n in tiles(GLA_V)]
    gates += [([mm(store(mga_s, col(n)), win_ref, col(n, C_MA))], [gate_ew(mga_s, _sigmoid, n)])
              for n in tiles(D_MODEL)]
    gates_b = [([mm(store(shr_s, col(n)), win_ref, col(n, C_HR))], [gate_ew(shr_s, _silu, n)]) for n in tiles(HG_V)]
    hg_v = [mm(lambda v, n=n: _store_values(w_s, n, v, HG_DK, HG_DV, nch), win_ref, col(n, C_HI), cast=BF16)
            for n in tiles(HG_V)]
    mgb = [([mm(store(mgb_s, col(n)), win_ref, col(n, C_MB))], [gate_ew(mgb_s, _sigmoid, n)]) for n in tiles(D_MODEL)]

    ra_ops, ra_rest = _recurrence_tasks(gla, qa_s, kd_s, w_s, u_s, nch)
    s_outer, s_update, s_readout = [], [], []
    for args in ((seg_ref, skg_ref, sqg_ref, svg_ref, ssg_ref, ssg_new_ref, soa_ref, GLA_H, GLA_DK, GLA_DV),
                 (seh_ref, skh_ref, sqh_ref, svh_ref, ssh_ref, ssh_new_ref, sob_ref, HG_H, HG_DK, HG_DV)):
        outer, update, readout = _sample_state_tasks(*args)
        s_outer += outer
        s_update += update
        s_readout += readout

    _run(_merge(ra_ops + ra_rest, _lagged(hg_in + gates), s_outer))

    def gla_norm(n):
        def fn(rb):
            g_s[rb, col(n)] = (_head_rmsnorm(og_s[rb, col(n)], gng_ref[:, col(n)], GLA_DV) * sgr_s[rb, col(n)]).astype(BF16)
        return ew(200.0, fn)

    def gla_scale(n):
        def fn(rb):
            p_s[rb, col(n)] = p_s[rb, col(n)] * mga_s[rb, col(n)]
        return ew(50.0, fn)

    gla_post = [gla_norm(n) for n in tiles(GLA_V)]
    for n in tiles(D_MODEL):
        gla_post += [mm(store(p_s, col(n)), wbg_ref, col(n), lhs=g_s), gla_scale(n)]
    rb_ops, rb_rest = _recurrence_tasks(hgrn, qa_s, kd_s, w_s, u_s, nch)
    _run(_merge(rb_ops, hg_v + _lagged(gates_b)))
    _run(_merge(rb_rest, gla_post, s_update))

    def hgrn_norm(n):
        def fn(rb):
            m_s[rb, col(n)] = (_head_rmsnorm(oh_s[rb, col(n)], hng_ref[:, col(n)], HG_DV) * shr_s[rb, col(n)]).astype(BF16)
        return ew(220.0, fn)

    def merge_ew(n):
        def fn(rb):
            g_s[rb, col(n)] = (p_s[rb, col(n)] + mgb_s[rb, col(n)] * sgr_s[rb, col(n)]).astype(BF16)
        return ew(60.0, fn)

    def final_ew(rows):
        def fn(rb):
            z = DEEPNORM_ALPHA * x_ref[0, rb, :] + shr_s[rb, :]
            y_ref[0, rb, :] = _layernorm(z, lng_ref[...], lnb_ref[...])
        return ew(180.0, fn, rows=rows, step=LN_ROW_BLOCK)

    _run(_merge([hgrn_norm(n) for n in tiles(HG_V)], _lagged(mgb)))
    _run([mm(store(sgr_s, col(n)), wbh_ref, col(n), lhs=m_s) for n in tiles(D_MODEL)])
    _run([merge_ew(n) for n in tiles(D_MODEL)])
    tail = [mm(store(shr_s, col(n)), wout_ref, col(n), lhs=g_s) for n in tiles(D_MODEL)]
    tail += [final_ew((r, r + ROW_BLOCK)) for r in range(0, tb, ROW_BLOCK)]
    _run(_merge(tail, s_readout))


def _whole(shape):
    return pl.BlockSpec(memory_space=pltpu.VMEM)


def _prompt_call(x, sample_rows, sample_states, win, wga, wglr, bg, gng, wbg, lbp, hng, wbh, wout, lng, lnb):
    bsz, seq, d = x.shape
    tb = TOK_BLOCK
    nch = tb // CHUNK
    per_row = seq // tb
    steps = bsz * per_row
    nseq = sample_states[0].shape[0] // steps
    assert nseq * steps == sample_states[0].shape[0] and all(r.shape[:2] == (steps, nseq) for r in sample_rows)
    weights = (win, wga, wglr, bg, gng, wbg, lbp, hng, wbh, wout, lng, lnb)
    lin = lambda b, t: b * per_row + t
    row_spec = lambda r: pl.BlockSpec((1, nseq, r.shape[2]), lambda b, t: (lin(b, t), 0, 0))
    state_spec = lambda s: pl.BlockSpec((nseq,) + s.shape[1:], lambda b, t: (lin(b, t), 0, 0, 0))
    return pl.pallas_call(
        _prompt_kernel,
        grid=(bsz, per_row),
        in_specs=[pl.BlockSpec((1, tb, d), lambda b, t: (b, t, 0))] + [row_spec(r) for r in sample_rows]
        + [state_spec(s) for s in sample_states] + [_whole(w.shape) for w in weights],
        out_specs=[
            pl.BlockSpec((1, tb, d), lambda b, t: (b, t, 0)),
            pl.BlockSpec((1, 1, GLA_H, GLA_DK, GLA_DV), lambda b, t: (0, b, 0, 0, 0)),
            pl.BlockSpec((1, 1, HG_H, HG_DK, HG_DV), lambda b, t: (0, b, 0, 0, 0)),
            state_spec(sample_states[0]), state_spec(sample_states[1]),
            pl.BlockSpec((1, nseq, GLA_V), lambda b, t: (lin(b, t), 0, 0)),
            pl.BlockSpec((1, nseq, HG_V), lambda b, t: (lin(b, t), 0, 0)),
        ],
        out_shape=[
            jax.ShapeDtypeStruct((bsz, seq, d), F32),
            jax.ShapeDtypeStruct((DEPTH, bsz, GLA_H, GLA_DK, GLA_DV), F32),
            jax.ShapeDtypeStruct((DEPTH, bsz, HG_H, HG_DK, HG_DV), F32),
            jax.ShapeDtypeStruct(sample_states[0].shape, F32),
            jax.ShapeDtypeStruct(sample_states[1].shape, F32),
            jax.ShapeDtypeStruct((steps, nseq, GLA_V), F32),
            jax.ShapeDtypeStruct((steps, nseq, HG_V), F32),
        ],
        scratch_shapes=[
            pltpu.VMEM((tb, d), BF16),
            pltpu.VMEM((tb, tb), BF16),
            pltpu.VMEM((tb, LANES), BF16),
            pltpu.VMEM((tb, GLA_K), F32),
            pltpu.VMEM((tb, GLA_K), F32),
            pltpu.VMEM((GLA_H, tb, LANES), F32),
            pltpu.VMEM((tb, HG_K), F32),
            pltpu.VMEM((tb, HG_K), F32),
            pltpu.VMEM((HG_H, tb, LANES), F32),
            pltpu.VMEM((3, tb, HG_K), BF16),
            pltpu.VMEM((HG_H, tb, 2 * LANES), BF16),
            pltpu.VMEM((HG_H, tb, LANES), BF16),
            pltpu.VMEM((GLA_H * nch, 2 * LANES, GLA_DV), BF16),
            pltpu.VMEM((GLA_H * nch, GLA_DK, GLA_DV), F32),
            pltpu.VMEM((tb, GLA_V), F32),
            pltpu.VMEM((tb, HG_V), F32),
            pltpu.VMEM((tb, GLA_V), F32),
            pltpu.VMEM((tb, HG_V), F32),
            pltpu.VMEM((tb, d), F32),
            pltpu.VMEM((tb, d), F32),
            pltpu.VMEM((tb, d), BF16),
            pltpu.VMEM((tb, d), BF16),
            pltpu.VMEM((tb, d), F32),
        ],
        compiler_params=pltpu.CompilerParams(
            dimension_semantics=("arbitrary", "arbitrary"), vmem_limit_bytes=VMEM_LIMIT),
        name="prompt_layer",
    )(x, *sample_rows, *sample_states, *weights)


def _sample_proj_kernel(x_ref, win_ref, wga_ref, wglr_ref, bg_ref, lbp_ref,
                        eg_ref, kg_ref, qg_ref, vg_ref, eh_ref, kh_ref, qh_ref, vh_ref,
                        sgr_ref, shr_ref, mga_ref, mgb_ref):
    xb = x_ref[...].astype(BF16)
    proj = lambda base, width: _dot(xb, win_ref[:, base:base + width])
    qg_ref[...] = proj(C_GQ, GLA_K) * (GLA_DK ** -0.5)
    kg_ref[...] = proj(C_GK, GLA_K)
    vg_ref[...] = proj(C_GV, GLA_V)
    sgr_ref[...] = _silu(proj(C_GR, GLA_V))
    ga = _dot(xb, wga_ref[...]).astype(BF16)
    a_logit = _dot(ga, wglr_ref[...]) + bg_ref[...]
    eg_ref[...] = jnp.exp(_log_sigmoid(a_logit) * (1.0 / GLA_TAU))
    lb = _lower_bound(lbp_ref[...])
    qh_ref[...] = _silu(proj(C_HQ, HG_K))
    sf, sfn = _sigmoid_pair(proj(C_HF, HG_K))
    eh_ref[...] = jnp.exp(jnp.log(lb + (1.0 - lb) * sf))
    kh_ref[...] = (1.0 - lb) * sfn
    vh_ref[...] = proj(C_HI, HG_V)
    shr_ref[...] = _silu(proj(C_HR, HG_V))
    mga_ref[...] = _sigmoid(proj(C_MA, D_MODEL))
    mgb_ref[...] = _sigmoid(proj(C_MB, D_MODEL))


def _sample_proj_call(x, win, wga, wglr, bg, lbp):
    n = x.shape[0]
    widths = (GLA_K, GLA_K, GLA_K, GLA_V, HG_K, HG_K, HG_K, HG_V, GLA_V, HG_V, D_MODEL, D_MODEL)
    args = (x, win, wga, wglr, bg, lbp)
    return pl.pallas_call(
        _sample_proj_kernel,
        in_specs=[_whole(a.shape) for a in args],
        out_specs=[_whole((n, w)) for w in widths],
        out_shape=[jax.ShapeDtypeStruct((n, w), F32) for w in widths],
        compiler_params=pltpu.CompilerParams(vmem_limit_bytes=VMEM_LIMIT),
        name="sample_proj",
    )(*args)


def _sample_post_kernel(x_ref, oa_ref, ob_ref, sgr_ref, shr_ref, mga_ref, mgb_ref,
                        gng_ref, wbg_ref, hng_ref, wbh_ref, wout_ref, lng_ref, lnb_ref, y_ref):
    ga = (_head_rmsnorm(oa_ref[...], gng_ref[...], GLA_DV) * sgr_ref[...]).astype(BF16)
    gb = (_head_rmsnorm(ob_ref[...], hng_ref[...], HG_DV) * shr_ref[...]).astype(BF16)
    merged = mga_ref[...] * _dot(ga, wbg_ref[...]) + mgb_ref[...] * _dot(gb, wbh_ref[...])
    z = DEEPNORM_ALPHA * x_ref[...] + _dot(merged.astype(BF16), wout_ref[...])
    y_ref[...] = _layernorm(z, lng_ref[...], lnb_ref[...])


def _sample_post_call(*args):
    n = args[0].shape[0]
    return pl.pallas_call(
        _sample_post_kernel,
        in_specs=[_whole(a.shape) for a in args],
        out_specs=_whole((n, D_MODEL)),
        out_shape=jax.ShapeDtypeStruct((n, D_MODEL), F32),
        compiler_params=pltpu.CompilerParams(vmem_limit_bytes=VMEM_LIMIT),
        name="sample_post",
    )(*args)


def _weight_prep_kernel(wt_ref, gat_ref, win_ref, wga_ref):
    win_ref[...] = wt_ref[...].T.astype(BF16)

    @pl.when(pl.program_id(0) == 0)
    def _():
        d = gat_ref.shape[1]
        ga = jnp.concatenate([gat_ref[...], jnp.zeros((LANES - GLA_RANK, d), F32)], axis=0)
        wga_ref[...] = ga.T.astype(BF16)


def _weight_prep_call(w_in):
    _, d, width = w_in.shape
    wt = jnp.swapaxes(w_in, 1, 2)[0]
    blk = WEIGHT_PREP_COLS
    assert C_HQ % blk == 0 and (width - GLA_RANK) % blk == 0
    src_row = lambda i: (i * (blk // GLA_RANK) + (i >= C_HQ // blk).astype(jnp.int32)) * GLA_RANK
    return pl.pallas_call(
        _weight_prep_kernel,
        grid=((width - GLA_RANK) // blk,),
        in_specs=[pl.BlockSpec((pl.Element(blk), pl.Element(d)), lambda i: (src_row(i), 0)),
                  pl.BlockSpec((pl.Element(GLA_RANK), pl.Element(d)), lambda i: (C_HQ, 0))],
        out_specs=[pl.BlockSpec((d, blk), lambda i: (0, i)),
                   pl.BlockSpec((d, LANES), lambda i: (0, 0))],
        out_shape=[jax.ShapeDtypeStruct((d, width - GLA_RANK), BF16), jax.ShapeDtypeStruct((d, LANES), BF16)],
        compiler_params=pltpu.CompilerParams(dimension_semantics=("arbitrary",), vmem_limit_bytes=VMEM_LIMIT),
        name="weight_prep",
    )(wt, wt)


def kernel(x_prompt, x_sample, state_gla, state_hgrn, w_in, w_gate_lr, b_gate_lr, gla_norm_g, w_br_gla,
           hgrn_lb_param, hgrn_norm_g, w_br_hgrn, w_out, ln_g, ln_b):
    assert w_in.shape[0] == DEPTH and x_sample.shape[1] == 1
    win, wga = _weight_prep_call(w_in)
    wglr = jnp.pad(w_gate_lr[0], ((0, LANES - GLA_RANK), (0, 0))).astype(BF16)
    bg = b_gate_lr[0].reshape(1, GLA_K)
    gng = gla_norm_g[0].reshape(1, GLA_V)
    hng = hgrn_norm_g[0].reshape(1, HG_V)
    wbg = w_br_gla[0].astype(BF16)
    wbh = w_br_hgrn[0].astype(BF16)
    wout = w_out[0].astype(BF16)
    lng = ln_g[0].reshape(1, D_MODEL)
    lnb = ln_b[0].reshape(1, D_MODEL)
    lbp = hgrn_lb_param

    xs = x_sample[:, 0, :]
    nsample = xs.shape[0]
    steps = x_prompt.shape[0] * (x_prompt.shape[1] // TOK_BLOCK)
    *rows, sgr, shr, mga, mgb = _sample_proj_call(xs, win, wga, wglr, bg, lbp)
    rows = [r.reshape(steps, nsample // steps, r.shape[1]) for r in rows]

    y_prompt, gla_p, hgrn_p, gla_s, hgrn_s, oa, ob = _prompt_call(
        x_prompt, rows, (state_gla[0], state_hgrn[0]), win, wga, wglr, bg, gng, wbg, lbp, hng, wbh, wout, lng, lnb)

    ys = _sample_post_call(xs, oa.reshape(nsample, GLA_V), ob.reshape(nsample, HG_V), sgr, shr, mga, mgb,
                           gng, wbg, hng, wbh, wout, lng, lnb)
    return (y_prompt, ys[:, None, :], gla_p, hgrn_p, gla_s[None], hgrn_s[None])
```

```python
from typing import Any, NamedTuple

import jax
import jax.numpy as jnp
from jax import lax
from jax.experimental import pallas as pl
from jax.experimental.pallas import tpu as pltpu

F32 = jnp.float32
BF16 = jnp.bfloat16

D_MODEL = 1024
GLA_H, GLA_DK, GLA_DV = 4, 128, 256
HG_H, HG_DK, HG_DV = 8, 128, 128
GLA_RANK = 16
GLA_TAU = 16.0
LOG2_E = 1.4426950408889634
GLA_K = GLA_H * GLA_DK
GLA_V = GLA_H * GLA_DV
HG_K = HG_H * HG_DK
HG_V = HG_H * HG_DV
C_GQ, C_GK, C_GV, C_GR = 0, GLA_K, 2 * GLA_K, 2 * GLA_K + GLA_V
C_HQ = C_GR + GLA_V
C_HF, C_HI, C_HR = C_HQ + HG_K, C_HQ + 2 * HG_K, C_HQ + 2 * HG_K + HG_V
C_MA = C_HR + HG_V
C_MB = C_MA + D_MODEL
CHUNK = 64
SUB = 16
NORM_EPS = 1e-5
DEPTH = 1
DEEPNORM_ALPHA = (2.0 * DEPTH) ** 0.25
LANES = 128
SUBLANES = 8
BF16_SUBLANES = 16
TOK_BLOCK = 256
COL_TILE = 256
ROW_BLOCK = 64
LN_ROW_BLOCK = 16
WEIGHT_PREP_COLS = 1024
VMEM_LIMIT = 60 * 1024 * 1024


def _dot(a, b):
    return jnp.dot(a, b, preferred_element_type=F32)


def _dot_nt(a, b):
    return lax.dot_general(a, b, (((1,), (1,)), ((), ())), preferred_element_type=F32)


def _dot_tn(a, b):
    return lax.dot_general(a, b, (((0,), (0,)), ((), ())), preferred_element_type=F32)


def _sigmoid(x):
    return 1.0 / (1.0 + jnp.exp(-x))


def _sigmoid_pair(x):
    e = jnp.exp(-jnp.abs(x))
    s = 1.0 / (1.0 + e)
    t = e * s
    pos = x >= 0.0
    return jnp.where(pos, s, t), jnp.where(pos, t, s)


def _silu(x):
    return x * _sigmoid(x)


def _log_sigmoid(x):
    return jnp.minimum(x, 0.0) - jnp.log1p(jnp.exp(-jnp.abs(x)))


def _split3(x):
    hi = x.astype(BF16)
    r = x - hi.astype(F32)
    mid = r.astype(BF16)
    lo = (r - mid.astype(F32)).astype(BF16)
    return hi, mid, lo


def _block_tri(n):
    r = lax.broadcasted_iota(jnp.int32, (n, n), 0)
    c = lax.broadcasted_iota(jnp.int32, (n, n), 1)
    shift = CHUNK.bit_length() - 1
    keep = (c <= r) & (jnp.right_shift(r, shift) == jnp.right_shift(c, shift))
    return jnp.where(keep, 1.0, 0.0).astype(BF16)


def _lower_bound(lbp):
    rows = [lbp[i:i + 1, :] for i in range(lbp.shape[0])]
    m = rows[0]
    for r in rows[1:]:
        m = jnp.maximum(m, r)
    es = [jnp.exp(r - m) for r in rows]
    tot = es[0]
    for e in es[1:]:
        tot = tot + e
    return es[0] / tot


def _score_mask():
    t = lax.broadcasted_iota(jnp.int32, (CHUNK, LANES), 0)
    lane = lax.broadcasted_iota(jnp.int32, (CHUNK, LANES), 1)
    s = lane - CHUNK * (jnp.right_shift(t, SUB.bit_length() - 1) & 1)
    return (s >= 0) & (s <= t)


def _intra_scores(q, k, b, keep):
    dk = q.shape[1]
    nsub = CHUNK // SUB
    qs, ks = [], []
    for i in range(nsub):
        lo, hi = i * SUB, (i + 1) * SUB
        bs = b[lo:lo + 1, :]
        qs.append(q[lo:hi] * jnp.exp2(b[lo:hi] - bs))
        ks.append((k[0:hi] * jnp.exp2(bs - b[0:hi])).astype(BF16))
        if hi < CHUNK:
            ks.append(jnp.zeros((CHUNK - hi, dk), BF16))
    prod = _dot_nt(jnp.concatenate(qs, axis=0).astype(BF16), jnp.concatenate(ks, axis=0))
    per = LANES // CHUNK
    a = jnp.concatenate([prod[i * SUB:(i + 1) * SUB, (i // per) * LANES:(i // per + 1) * LANES]
                         for i in range(nsub)], axis=0)
    return jnp.where(keep, a, 0.0)


def _columns(rows):
    r = rows.shape[0]
    if r < LANES:
        rows = jnp.concatenate([rows, jnp.zeros((LANES - r, LANES), F32)], axis=0)
    return rows.T


def _head_rmsnorm(o, gain, dv):
    outs = []
    for j in range(o.shape[1] // dv):
        oj = o[:, j * dv:(j + 1) * dv]
        ms = jnp.mean(oj * oj, axis=-1, keepdims=True)
        outs.append(oj * lax.rsqrt(ms + NORM_EPS))
    on = outs[0] if len(outs) == 1 else jnp.concatenate(outs, axis=1)
    return on * gain


def _layernorm(z, g, b):
    mu = jnp.mean(z, axis=-1, keepdims=True)
    zc = z - mu
    var = jnp.mean(zc * zc, axis=-1, keepdims=True)
    return zc * lax.rsqrt(var + NORM_EPS) * g + b


class _Task(NamedTuple):
    cost: float
    run: Any


def _merge(*lists):
    totals = [sum(t.cost for t in l) or 1.0 for l in lists]
    pos = [0] * len(lists)
    done = [0.0] * len(lists)
    out = []
    for _ in range(sum(len(l) for l in lists)):
        live = [i for i in range(len(lists)) if pos[i] < len(lists[i])]
        j = min(live, key=lambda i: (done[i] + 0.5 * lists[i][pos[i]].cost) / totals[i])
        out.append(lists[j][pos[j]])
        done[j] += lists[j][pos[j]].cost
        pos[j] += 1
    return out


def _lagged(groups, lag=1):
    out = []
    for i, (mms, _) in enumerate(groups):
        out += mms
        if i >= lag:
            out += groups[i - lag][1]
    for _, ews in groups[max(len(groups) - lag, 0):]:
        out += ews
    return out


def _run(tasks):
    for t in tasks:
        t.run()


class _Branch(NamedTuple):
    heads: int
    dk: int
    dv: int
    scale: float
    q: Any
    k: Any
    b: Any
    o: Any
    state: Any


def _slab(buf, hc, dv):
    per = buf.shape[2] // dv
    return hc // per, slice((hc % per) * dv, (hc % per + 1) * dv)


def _store_decay(b_s, n, bt):
    per = COL_TILE // LANES
    for j in range(per):
        b_s[n * per + j] = bt[:, j * LANES:(j + 1) * LANES]


def _store_values(w_s, n, vt, dk, dv, nchunks):
    per = COL_TILE // dv
    for j in range(per):
        for ci in range(nchunks):
            i, cs = _slab(w_s, (n * per + j) * nchunks + ci, dv)
            v = vt[ci * CHUNK:(ci + 1) * CHUNK, j * dv:(j + 1) * dv]
            w_s[i, dk:dk + CHUNK, cs] = v
            w_s[i, dk + CHUNK:dk + 2 * CHUNK, cs] = v


def _recurrence_tasks(br, qa_s, kd_s, w_s, u_s, nchunks):
    nheads, dk, dv = br.heads, br.dk, br.dv
    keep = _score_mask()
    pairs = [(ci, h) for ci in range(nchunks) for h in range(nheads)]
    rows_of = lambda ci: slice(ci * CHUNK, (ci + 1) * CHUNK)
    cols = []

    def operands(ci, h):
        rows, kcols = rows_of(ci), slice(h * dk, (h + 1) * dk)
        q, k, b = br.q[rows, kcols], br.k[rows, kcols], br.b[h, rows, :]
        if br.scale != 1.0:
            q = q * br.scale
        qa_s[h, rows, 0:dk] = (q * jnp.exp2(b)).astype(BF16)
        kd_s[h, rows, :] = (k * jnp.exp2(b[CHUNK - 1:CHUNK, :] - b)).astype(BF16)
        qa_s[h, rows, dk:dk + LANES] = _intra_scores(q, k, b, keep).astype(BF16)

    def outer(ci, h):
        i, cs = _slab(w_s, h * nchunks + ci, dv)
        u_s[i, :, cs] = _dot_tn(kd_s[h, rows_of(ci), :], w_s[i, dk:dk + CHUNK, cs])

    def decay_columns():
        half = CHUNK // 2
        for h in range(nheads):
            cols.append(_columns(jnp.exp2(br.b[h, pl.ds(half - 1, 2 * nchunks, stride=half), :])))

    def chain(ci, h):
        i, cs = _slab(w_s, h * nchunks + ci, dv)
        s = br.state[0, 0, h]
        w_s[i, 0:dk, cs] = s.astype(BF16)
        br.state[0, 0, h] = s * cols[h][:, 2 * ci + 1:2 * ci + 2] + u_s[i, :, cs]

    def output(ci, h):
        i, cs = _slab(w_s, h * nchunks + ci, dv)
        br.o[rows_of(ci), h * dv:(h + 1) * dv] = _dot(qa_s[h, rows_of(ci), :], w_s[i, :, cs])

    bind = lambda cost, fn: [_Task(cost, lambda ci=ci, h=h: fn(ci, h)) for ci, h in pairs]
    return bind(70, operands), (bind(32, outer) + [_Task(16 * nheads, decay_columns)]
                                + bind(8 + dv // 8, chain) + bind(16, output))


def _sample_state_tasks(row0, e_ref, k_ref, q_ref, v_ref, s_ref, snew_ref, o_ref, nheads, dk, dv):
    nseq = s_ref.shape[0]
    first = lax.broadcasted_iota(jnp.int32, (BF16_SUBLANES, dk), 0) == 0
    ks = lambda h: slice(h * dk, (h + 1) * dk)
    vs = lambda h: slice(h * dv, (h + 1) * dv)
    mine = lambda j, width: lax.broadcasted_iota(jnp.int32, (SUBLANES, width), 0) == row0 + j

    def vec(ref, j, cols):
        return jnp.sum(jnp.where(mine(j, cols.stop - cols.start), ref[:, cols], 0.0), axis=0, keepdims=True)

    tile = lambda ref, j, cols: jnp.broadcast_to(vec(ref, j, cols), (BF16_SUBLANES, cols.stop - cols.start))
    pairs = [(h, j) for h in range(nheads) for j in range(nseq)]

    def outer(h, j):
        kj = jnp.where(first, tile(k_ref, j, ks(h)), 0.0).astype(BF16)
        snew_ref[j, h] = _dot_tn(kj, tile(v_ref, j, vs(h)).astype(BF16))

    def update(h, j):
        ecol = jnp.broadcast_to(vec(e_ref, j, ks(h)), (dk, LANES)).T
        ecol = ecol if dv == LANES else jnp.concatenate([ecol] * (dv // LANES), axis=1)
        snew_ref[j, h] = s_ref[j, h] * ecol + snew_ref[j, h]

    def readout(h, j):
        o = _dot(tile(q_ref, j, ks(h)).astype(BF16), snew_ref[j, h].astype(BF16))
        o_ref[:, vs(h)] = jnp.where(mine(j, dv), o[0:SUBLANES, :], o_ref[:, vs(h)])

    bind = lambda cost, fn: [_Task(cost, lambda h=h, j=j: fn(h, j)) for h, j in pairs]
    return bind(40.0, outer), bind(30.0, update), bind(20.0, readout)


def _prompt_kernel(x_ref, seg_ref, skg_ref, sqg_ref, svg_ref, seh_ref, skh_ref, sqh_ref, svh_ref, ssg_ref, ssh_ref,
                   win_ref, wga_ref, wglr_ref, bg_ref, gng_ref, wbg_ref,
                   lbp_ref, hng_ref, wbh_ref, wout_ref, lng_ref, lnb_ref,
                   y_ref, sg_ref, sh_ref, ssg_new_ref, ssh_new_ref, soa_ref, sob_ref,
                   xb_s, tri_s, ga_s, qg_s, kg_s, bgl_s, qh_s, kh_s, bh_s, h3_s, qa_s, kd_s, w_s, u_s,
                   og_s, oh_s, sgr_s, shr_s, mga_s, mgb_s, g_s, m_s, p_s):
    tb = TOK_BLOCK
    nch = tb // CHUNK
    tiles = lambda width: range(width // COL_TILE)
    col = lambda n, base=0: slice(base + n * COL_TILE, base + (n + 1) * COL_TILE)
    gla = _Branch(GLA_H, GLA_DK, GLA_DV, GLA_DK ** -0.5, qg_s, kg_s, bgl_s, og_s, sg_ref)
    hgrn = _Branch(HG_H, HG_DK, HG_DV, 1.0, qh_s, kh_s, bh_s, oh_s, sh_ref)
    mm_cost = 256.0

    @pl.when(pl.program_id(1) == 0)
    def _():
        sg_ref[...] = jnp.zeros_like(sg_ref)
        sh_ref[...] = jnp.zeros_like(sh_ref)
        tri_s[...] = _block_tri(tb)

    nseq = ssg_ref.shape[0]
    lin = pl.program_id(0) * pl.num_programs(1) + pl.program_id(1)
    row0 = (lin % (SUBLANES // nseq)) * nseq

    @pl.when(row0 == 0)
    def _():
        soa_ref[...] = jnp.zeros_like(soa_ref)
        sob_ref[...] = jnp.zeros_like(sob_ref)

    xb_s[...] = x_ref[0].astype(BF16)
    lb = _lower_bound(lbp_ref[...])

    def mm(dst, w_ref, wcols, lhs=xb_s, cast=None, cost=mm_cost):
        def run():
            r = _dot(lhs[...], w_ref[:, wcols])
            dst(r if cast is None else r.astype(cast))
        return _Task(cost, run)

    def store(ref, cols):
        def put(v):
            ref[:, cols] = v
        return put

    def ew(cost, fn, rows=(0, tb), step=ROW_BLOCK):
        def run():
            for r in range(rows[0], rows[1], step):
                fn(slice(r, r + step))
        return _Task(cost, run)

    def cumsum_task(b_s, n):
        def run():
            t = tri_s[...]
            _store_decay(b_s, n, _dot(t, h3_s[0, :, col(n)]) + _dot(t, h3_s[1, :, col(n)]) + _dot(t, h3_s[2, :, col(n)]))
        return _Task(192.0, run)

    def put_split3(rb, cs, g):
        hi, mid, lo = _split3(g)
        h3_s[0, rb, cs] = hi
        h3_s[1, rb, cs] = mid
        h3_s[2, rb, cs] = lo

    def gla_logit(n):
        def run():
            qg_s[:, col(n)] = _dot(ga_s[...], wglr_ref[:, col(n)]) + bg_ref[:, col(n)]
        return _Task(64.0, run)

    def gla_log(n):
        def fn(rb):
            put_split3(rb, col(n), _log_sigmoid(qg_s[rb, col(n)]) * (LOG2_E / GLA_TAU))
        return ew(300.0, fn)

    gla_in = [mm(store(ga_s, slice(None)), wga_ref, slice(None), cast=BF16, cost=128.0)]
    gla_in += [mm(store(kg_s, col(n)), win_ref, col(n, C_GK)) for n in tiles(GLA_K)]
    gla_in += [gla_logit(n) for n in tiles(GLA_K)]
    gla_in += [t for n in tiles(GLA_K) for t in (
        mm(lambda v, n=n: _store_values(w_s, n, v, GLA_DK, GLA_DV, nch), win_ref, col(n, C_GV), cast=BF16),
        gla_log(n),
        mm(lambda v, n=n: _store_values(w_s, n + GLA_K // COL_TILE, v, GLA_DK, GLA_DV, nch), win_ref,
           col(n + GLA_K // COL_TILE, C_GV), cast=BF16))]
    gla_in += [cumsum_task(bgl_s, n) for n in tiles(GLA_K)]
    gla_in += [mm(store(qg_s, col(n)), win_ref, col(n, C_GQ)) for n in tiles(GLA_K)]
    _run(gla_in)

    def hq_ew(n):
        def fn(rb):
            qh_s[rb, col(n)] = _silu(qh_s[rb, col(n)])
        return ew(130.0, fn)

    def hf_ew(n):
        def fn(rb):
            sf, sfn = _sigmoid_pair(kh_s[rb, col(n)])
            lbn = lb[:, col(n)]
            kh_s[rb, col(n)] = (1.0 - lbn) * sfn
            put_split3(rb, col(n), jnp.log2(lbn + (1.0 - lbn) * sf))
        return ew(450.0, fn)

    def gate_ew(ref, act, n):
        def fn(rb):
            ref[rb, col(n)] = act(ref[rb, col(n)])
        return ew(130.0, fn)

    hg_in = [([mm(store(kh_s, col(n)), win_ref, col(n, C_HF)),
               mm(store(qh_s, col(n)), win_ref, col(n, C_HQ))],
              [hf_ew(n), hq_ew(n), cumsum_task(bh_s, n)]) for n in tiles(HG_K)]
    gates = [([mm(store(sgr_s, col(n)), win_ref, col(n, C_GR))], [gate_ew(sgr_s, _silu, n)]) for n in tiles(GLA_V)]
    gates += [([mm(store(mga_s, col(n)), win_ref, col(n, C_MA))], [gate_ew(mga_s, _sigmoid, n)])
              for n in tiles(D_MODEL)]
    gates_b = [([mm(store(shr_s, col(n)), win_ref, col(n, C_HR))], [gate_ew(shr_s, _silu, n)]) for n in tiles(HG_V)]
    hg_v = [mm(lambda v, n=n: _store_values(w_s, n, v, HG_DK, HG_DV, nch), win_ref, col(n, C_HI), cast=BF16)
            for n in tiles(HG_V)]
    mgb = [([mm(store(mgb_s, col(n)), win_ref, col(n, C_MB))], [gate_ew(mgb_s, _sigmoid, n)]) for n in tiles(D_MODEL)]

    ra_ops, ra_rest = _recurrence_tasks(gla, qa_s, kd_s, w_s, u_s, nch)
    s_outer, s_update, s_readout = [], [], []
    for args in ((seg_ref, skg_ref, sqg_ref, svg_ref, ssg_ref, ssg_new_ref, soa_ref, GLA_H, GLA_DK, GLA_DV),
                 (seh_ref, skh_ref, sqh_ref, svh_ref, ssh_ref, ssh_new_ref, sob_ref, HG_H, HG_DK, HG_DV)):
        outer, update, readout = _sample_state_tasks(row0, *args)
        s_outer += outer
        s_update += update
        s_readout += readout

    _run(_merge(ra_ops + ra_rest, _lagged(hg_in + gates), s_outer))

    def gla_norm(n):
        def fn(rb):
            g_s[rb, col(n)] = (_head_rmsnorm(og_s[rb, col(n)], gng_ref[:, col(n)], GLA_DV) * sgr_s[rb, col(n)]).astype(BF16)
        return ew(200.0, fn)

    def gla_scale(n):
        def fn(rb):
            p_s[rb, col(n)] = p_s[rb, col(n)] * mga_s[rb, col(n)]
        return ew(50.0, fn)

    gla_post = [gla_norm(n) for n in tiles(GLA_V)]
    for n in tiles(D_MODEL):
        gla_post += [mm(store(p_s, col(n)), wbg_ref, col(n), lhs=g_s), gla_scale(n)]
    rb_ops, rb_rest = _recurrence_tasks(hgrn, qa_s, kd_s, w_s, u_s, nch)
    _run(_merge(rb_ops, hg_v + _lagged(gates_b)))
    _run(_merge(rb_rest, gla_post, s_update))

    def hgrn_norm(n):
        def fn(rb):
            m_s[rb, col(n)] = (_head_rmsnorm(oh_s[rb, col(n)], hng_ref[:, col(n)], HG_DV) * shr_s[rb, col(n)]).astype(BF16)
        return ew(220.0, fn)

    def merge_ew(n):
        def fn(rb):
            g_s[rb, col(n)] = (p_s[rb, col(n)] + mgb_s[rb, col(n)] * sgr_s[rb, col(n)]).astype(BF16)
        return ew(60.0, fn)

    def final_ew(rows):
        def fn(rb):
            z = DEEPNORM_ALPHA * x_ref[0, rb, :] + shr_s[rb, :]
            y_ref[0, rb, :] = _layernorm(z, lng_ref[...], lnb_ref[...])
        return ew(180.0, fn, rows=rows, step=LN_ROW_BLOCK)

    _run(_merge([hgrn_norm(n) for n in tiles(HG_V)], _lagged(mgb)))
    _run([mm(store(sgr_s, col(n)), wbh_ref, col(n), lhs=m_s) for n in tiles(D_MODEL)])
    _run([merge_ew(n) for n in tiles(D_MODEL)])
    tail = [mm(store(shr_s, col(n)), wout_ref, col(n), lhs=g_s) for n in tiles(D_MODEL)]
    tail += [final_ew((r, r + ROW_BLOCK)) for r in range(0, tb, ROW_BLOCK)]
    _run(_merge(tail, s_readout))


def _whole(shape):
    return pl.BlockSpec(memory_space=pltpu.VMEM)


def _prompt_call(x, sample_rows, sample_states, win, wga, wglr, bg, gng, wbg, lbp, hng, wbh, wout, lng, lnb):
    bsz, seq, d = x.shape
    tb = TOK_BLOCK
    nch = tb // CHUNK
    per_row = seq // tb
    steps = bsz * per_row
    nsample = sample_states[0].shape[0]
    nseq = nsample // steps
    assert nseq * steps == nsample and SUBLANES % nseq == 0 and all(r.shape[0] == nsample for r in sample_rows)
    weights = (win, wga, wglr, bg, gng, wbg, lbp, hng, wbh, wout, lng, lnb)
    lin = lambda b, t: b * per_row + t
    row_block = lambda b, t: (lin(b, t) // (SUBLANES // nseq), 0)
    row_spec = lambda r: pl.BlockSpec((SUBLANES, r.shape[1]), row_block)
    state_spec = lambda s: pl.BlockSpec((nseq,) + s.shape[1:], lambda b, t: (lin(b, t), 0, 0, 0))
    return pl.pallas_call(
        _prompt_kernel,
        grid=(bsz, per_row),
        in_specs=[pl.BlockSpec((1, tb, d), lambda b, t: (b, t, 0))] + [row_spec(r) for r in sample_rows]
        + [state_spec(s) for s in sample_states] + [_whole(w.shape) for w in weights],
        out_specs=[
            pl.BlockSpec((1, tb, d), lambda b, t: (b, t, 0)),
            pl.BlockSpec((1, 1, GLA_H, GLA_DK, GLA_DV), lambda b, t: (0, b, 0, 0, 0)),
            pl.BlockSpec((1, 1, HG_H, HG_DK, HG_DV), lambda b, t: (0, b, 0, 0, 0)),
            state_spec(sample_states[0]), state_spec(sample_states[1]),
            pl.BlockSpec((SUBLANES, GLA_V), row_block),
            pl.BlockSpec((SUBLANES, HG_V), row_block),
        ],
        out_shape=[
            jax.ShapeDtypeStruct((bsz, seq, d), F32),
            jax.ShapeDtypeStruct((DEPTH, bsz, GLA_H, GLA_DK, GLA_DV), F32),
            jax.ShapeDtypeStruct((DEPTH, bsz, HG_H, HG_DK, HG_DV), F32),
            jax.ShapeDtypeStruct(sample_states[0].shape, F32),
            jax.ShapeDtypeStruct(sample_states[1].shape, F32),
            jax.ShapeDtypeStruct((nsample, GLA_V), F32),
            jax.ShapeDtypeStruct((nsample, HG_V), F32),
        ],
        scratch_shapes=[
            pltpu.VMEM((tb, d), BF16),
            pltpu.VMEM((tb, tb), BF16),
            pltpu.VMEM((tb, LANES), BF16),
            pltpu.VMEM((tb, GLA_K), F32),
            pltpu.VMEM((tb, GLA_K), F32),
            pltpu.VMEM((GLA_H, tb, LANES), F32),
            pltpu.VMEM((tb, HG_K), F32),
            pltpu.VMEM((tb, HG_K), F32),
            pltpu.VMEM((HG_H, tb, LANES), F32),
            pltpu.VMEM((3, tb, HG_K), BF16),
            pltpu.VMEM((HG_H, tb, 2 * LANES), BF16),
            pltpu.VMEM((HG_H, tb, LANES), BF16),
            pltpu.VMEM((GLA_H * nch, 2 * LANES, GLA_DV), BF16),
            pltpu.VMEM((GLA_H * nch, GLA_DK, GLA_DV), F32),
            pltpu.VMEM((tb, GLA_V), F32),
            pltpu.VMEM((tb, HG_V), F32),
            pltpu.VMEM((tb, GLA_V), F32),
            pltpu.VMEM((tb, HG_V), F32),
            pltpu.VMEM((tb, d), F32),
            pltpu.VMEM((tb, d), F32),
            pltpu.VMEM((tb, d), BF16),
            pltpu.VMEM((tb, d), BF16),
            pltpu.VMEM((tb, d), F32),
        ],
        compiler_params=pltpu.CompilerParams(
            dimension_semantics=("arbitrary", "arbitrary"), vmem_limit_bytes=VMEM_LIMIT),
        name="prompt_layer",
    )(x, *sample_rows, *sample_states, *weights)


def _sample_proj_kernel(x_ref, win_ref, wga_ref, wglr_ref, bg_ref, lbp_ref,
                        eg_ref, kg_ref, qg_ref, vg_ref, eh_ref, kh_ref, qh_ref, vh_ref,
                        sgr_ref, shr_ref, mga_ref, mgb_ref):
    xb = x_ref[...].astype(BF16)
    proj = lambda base, width: _dot(xb, win_ref[:, base:base + width])
    qg_ref[...] = proj(C_GQ, GLA_K) * (GLA_DK ** -0.5)
    kg_ref[...] = proj(C_GK, GLA_K)
    vg_ref[...] = proj(C_GV, GLA_V)
    sgr_ref[...] = _silu(proj(C_GR, GLA_V))
    ga = _dot(xb, wga_ref[...]).astype(BF16)
    a_logit = _dot(ga, wglr_ref[...]) + bg_ref[...]
    eg_ref[...] = jnp.exp(_log_sigmoid(a_logit) * (1.0 / GLA_TAU))
    lb = _lower_bound(lbp_ref[...])
    qh_ref[...] = _silu(proj(C_HQ, HG_K))
    sf, sfn = _sigmoid_pair(proj(C_HF, HG_K))
    eh_ref[...] = jnp.exp(jnp.log(lb + (1.0 - lb) * sf))
    kh_ref[...] = (1.0 - lb) * sfn
    vh_ref[...] = proj(C_HI, HG_V)
    shr_ref[...] = _silu(proj(C_HR, HG_V))
    mga_ref[...] = _sigmoid(proj(C_MA, D_MODEL))
    mgb_ref[...] = _sigmoid(proj(C_MB, D_MODEL))


def _sample_proj_call(x, win, wga, wglr, bg, lbp):
    n = x.shape[0]
    widths = (GLA_K, GLA_K, GLA_K, GLA_V, HG_K, HG_K, HG_K, HG_V, GLA_V, HG_V, D_MODEL, D_MODEL)
    args = (x, win, wga, wglr, bg, lbp)
    return pl.pallas_call(
        _sample_proj_kernel,
        in_specs=[_whole(a.shape) for a in args],
        out_specs=[_whole((n, w)) for w in widths],
        out_shape=[jax.ShapeDtypeStruct((n, w), F32) for w in widths],
        compiler_params=pltpu.CompilerParams(vmem_limit_bytes=VMEM_LIMIT),
        name="sample_proj",
    )(*args)


def _sample_post_kernel(x_ref, oa_ref, ob_ref, sgr_ref, shr_ref, mga_ref, mgb_ref,
                        gng_ref, wbg_ref, hng_ref, wbh_ref, wout_ref, lng_ref, lnb_ref, y_ref):
    ga = (_head_rmsnorm(oa_ref[...], gng_ref[...], GLA_DV) * sgr_ref[...]).astype(BF16)
    gb = (_head_rmsnorm(ob_ref[...], hng_ref[...], HG_DV) * shr_ref[...]).astype(BF16)
    merged = mga_ref[...] * _dot(ga, wbg_ref[...]) + mgb_ref[...] * _dot(gb, wbh_ref[...])
    z = DEEPNORM_ALPHA * x_ref[...] + _dot(merged.astype(BF16), wout_ref[...])
    y_ref[...] = _layernorm(z, lng_ref[...], lnb_ref[...])


def _sample_post_call(*args):
    n = args[0].shape[0]
    return pl.pallas_call(
        _sample_post_kernel,
        in_specs=[_whole(a.shape) for a in args],
        out_specs=_whole((n, D_MODEL)),
        out_shape=jax.ShapeDtypeStruct((n, D_MODEL), F32),
        compiler_params=pltpu.CompilerParams(vmem_limit_bytes=VMEM_LIMIT),
        name="sample_post",
    )(*args)


def _weight_prep_kernel(wt_ref, gat_ref, win_ref, wga_ref):
    win_ref[...] = wt_ref[...].T.astype(BF16)

    @pl.when(pl.program_id(0) == 0)
    def _():
        d = gat_ref.shape[1]
        ga = jnp.concatenate([gat_ref[...], jnp.zeros((LANES - GLA_RANK, d), F32)], axis=0)
        wga_ref[...] = ga.T.astype(BF16)


def _weight_prep_call(w_in):
    _, d, width = w_in.shape
    wt = jnp.swapaxes(w_in, 1, 2)[0]
    blk = WEIGHT_PREP_COLS
    assert C_HQ % blk == 0 and (width - GLA_RANK) % blk == 0
    src_row = lambda i: (i * (blk // GLA_RANK) + (i >= C_HQ // blk).astype(jnp.int32)) * GLA_RANK
    return pl.pallas_call(
        _weight_prep_kernel,
        grid=((width - GLA_RANK) // blk,),
        in_specs=[pl.BlockSpec((pl.Element(blk), pl.Element(d)), lambda i: (src_row(i), 0)),
                  pl.BlockSpec((pl.Element(GLA_RANK), pl.Element(d)), lambda i: (C_HQ, 0))],
        out_specs=[pl.BlockSpec((d, blk), lambda i: (0, i)),
                   pl.BlockSpec((d, LANES), lambda i: (0, 0))],
        out_shape=[jax.ShapeDtypeStruct((d, width - GLA_RANK), BF16), jax.ShapeDtypeStruct((d, LANES), BF16)],
        compiler_params=pltpu.CompilerParams(dimension_semantics=("arbitrary",), vmem_limit_bytes=VMEM_LIMIT),
        name="weight_prep",
    )(wt, wt)


def kernel(x_prompt, x_sample, state_gla, state_hgrn, w_in, w_gate_lr, b_gate_lr, gla_norm_g, w_br_gla,
           hgrn_lb_param, hgrn_norm_g, w_br_hgrn, w_out, ln_g, ln_b):
    assert w_in.shape[0] == DEPTH and x_sample.shape[1] == 1
    win, wga = _weight_prep_call(w_in)
    wglr = jnp.pad(w_gate_lr[0], ((0, LANES - GLA_RANK), (0, 0))).astype(BF16)
    bg = b_gate_lr[0].reshape(1, GLA_K)
    gng = gla_norm_g[0].reshape(1, GLA_V)
    hng = hgrn_norm_g[0].reshape(1, HG_V)
    wbg = w_br_gla[0].astype(BF16)
    wbh = w_br_hgrn[0].astype(BF16)
    wout = w_out[0].astype(BF16)
    lng = ln_g[0].reshape(1, D_MODEL)
    lnb = ln_b[0].reshape(1, D_MODEL)
    lbp = hgrn_lb_param

    xs = x_sample[:, 0, :]
    *rows, sgr, shr, mga, mgb = _sample_proj_call(xs, win, wga, wglr, bg, lbp)

    y_prompt, gla_p, hgrn_p, gla_s, hgrn_s, oa, ob = _prompt_call(
        x_prompt, rows, (state_gla[0], state_hgrn[0]), win, wga, wglr, bg, gng, wbg, lbp, hng, wbh, wout, lng, lnb)

    ys = _sample_post_call(xs, oa, ob, sgr, shr, mga, mgb, gng, wbg, hng, wbh, wout, lng, lnb)
    return (y_prompt, ys[:, None, :], gla_p, hgrn_p, gla_s[None], hgrn_s[None])
```

```python
from typing import Any, NamedTuple

import jax
import jax.numpy as jnp
from jax import lax
from jax.experimental import pallas as pl
from jax.experimental.pallas import tpu as pltpu

F32 = jnp.float32
BF16 = jnp.bfloat16

D_MODEL = 1024
GLA_H, GLA_DK, GLA_DV = 4, 128, 256
HG_H, HG_DK, HG_DV = 8, 128, 128
GLA_RANK = 16
GLA_TAU = 16.0
LOG2_E = 1.4426950408889634
GLA_K = GLA_H * GLA_DK
GLA_V = GLA_H * GLA_DV
HG_K = HG_H * HG_DK
HG_V = HG_H * HG_DV
C_GQ, C_GK, C_GV, C_GR = 0, GLA_K, 2 * GLA_K, 2 * GLA_K + GLA_V
C_HQ = C_GR + GLA_V
C_HF, C_HI, C_HR = C_HQ + HG_K, C_HQ + 2 * HG_K, C_HQ + 2 * HG_K + HG_V
C_MA = C_HR + HG_V
C_MB = C_MA + D_MODEL
CHUNK = 64
SUB = 16
NORM_EPS = 1e-5
DEPTH = 1
DEEPNORM_ALPHA = (2.0 * DEPTH) ** 0.25
LANES = 128
SUBLANES = 8
BF16_SUBLANES = 16
TOK_BLOCK = 256
COL_TILE = 256
ROW_BLOCK = 64
LN_ROW_BLOCK = 16
WEIGHT_PREP_COLS = 1024
WEIGHT_PREP_ROWS = 128
VMEM_LIMIT = 60 * 1024 * 1024


def _dot(a, b):
    return jnp.dot(a, b, preferred_element_type=F32)


def _dot_nt(a, b):
    return lax.dot_general(a, b, (((1,), (1,)), ((), ())), preferred_element_type=F32)


def _dot_tn(a, b):
    return lax.dot_general(a, b, (((0,), (0,)), ((), ())), preferred_element_type=F32)


def _sigmoid(x):
    return 1.0 / (1.0 + jnp.exp(-x))


def _sigmoid_pair(x):
    e = jnp.exp(-jnp.abs(x))
    s = 1.0 / (1.0 + e)
    t = e * s
    pos = x >= 0.0
    return jnp.where(pos, s, t), jnp.where(pos, t, s)


def _silu(x):
    return x * _sigmoid(x)


def _log_sigmoid(x):
    return jnp.minimum(x, 0.0) - jnp.log(1.0 + jnp.exp(-jnp.abs(x)))


def _split3(x):
    hi = x.astype(BF16)
    r = x - hi.astype(F32)
    mid = r.astype(BF16)
    lo = (r - mid.astype(F32)).astype(BF16)
    return hi, mid, lo


def _block_tri(n):
    r = lax.broadcasted_iota(jnp.int32, (n, n), 0)
    c = lax.broadcasted_iota(jnp.int32, (n, n), 1)
    shift = CHUNK.bit_length() - 1
    keep = (c <= r) & (jnp.right_shift(r, shift) == jnp.right_shift(c, shift))
    return jnp.where(keep, 1.0, 0.0).astype(BF16)


def _lower_bound(lbp):
    rows = [lbp[i:i + 1, :] for i in range(lbp.shape[0])]
    m = rows[0]
    for r in rows[1:]:
        m = jnp.maximum(m, r)
    es = [jnp.exp(r - m) for r in rows]
    tot = es[0]
    for e in es[1:]:
        tot = tot + e
    return es[0] / tot


def _score_mask():
    t = lax.broadcasted_iota(jnp.int32, (CHUNK, LANES), 0)
    lane = lax.broadcasted_iota(jnp.int32, (CHUNK, LANES), 1)
    s = lane - CHUNK * (jnp.right_shift(t, SUB.bit_length() - 1) & 1)
    return (s >= 0) & (s <= t)


def _intra_scores(q, k, b, keep):
    dk = q.shape[1]
    nsub = CHUNK // SUB
    qs, ks = [], []
    for i in range(nsub):
        lo, hi = i * SUB, (i + 1) * SUB
        bs = b[lo:lo + 1, :]
        qs.append(q[lo:hi] * jnp.exp2(b[lo:hi] - bs))
        ks.append((k[0:hi] * jnp.exp2(bs - b[0:hi])).astype(BF16))
        if hi < CHUNK:
            ks.append(jnp.zeros((CHUNK - hi, dk), BF16))
    prod = _dot_nt(jnp.concatenate(qs, axis=0).astype(BF16), jnp.concatenate(ks, axis=0))
    per = LANES // CHUNK
    a = jnp.concatenate([prod[i * SUB:(i + 1) * SUB, (i // per) * LANES:(i // per + 1) * LANES]
                         for i in range(nsub)], axis=0)
    return jnp.where(keep, a, 0.0)


def _columns(rows):
    r = rows.shape[0]
    if r < LANES:
        rows = jnp.concatenate([rows, jnp.zeros((LANES - r, LANES), F32)], axis=0)
    return rows.T


def _head_rmsnorm(o, gain, dv):
    outs = []
    for j in range(o.shape[1] // dv):
        oj = o[:, j * dv:(j + 1) * dv]
        ms = jnp.mean(oj * oj, axis=-1, keepdims=True)
        outs.append(oj * lax.rsqrt(ms + NORM_EPS))
    on = outs[0] if len(outs) == 1 else jnp.concatenate(outs, axis=1)
    return on * gain


def _layernorm(z, g, b):
    mu = jnp.mean(z, axis=-1, keepdims=True)
    zc = z - mu
    var = jnp.mean(zc * zc, axis=-1, keepdims=True)
    return zc * lax.rsqrt(var + NORM_EPS) * g + b


class _Task(NamedTuple):
    cost: float
    run: Any


def _merge(*lists):
    totals = [sum(t.cost for t in l) or 1.0 for l in lists]
    pos = [0] * len(lists)
    done = [0.0] * len(lists)
    out = []
    for _ in range(sum(len(l) for l in lists)):
        live = [i for i in range(len(lists)) if pos[i] < len(lists[i])]
        j = min(live, key=lambda i: (done[i] + 0.5 * lists[i][pos[i]].cost) / totals[i])
        out.append(lists[j][pos[j]])
        done[j] += lists[j][pos[j]].cost
        pos[j] += 1
    return out


def _lagged(groups, lag=1):
    out = []
    for i, (mms, _) in enumerate(groups):
        out += mms
        if i >= lag:
            out += groups[i - lag][1]
    for _, ews in groups[max(len(groups) - lag, 0):]:
        out += ews
    return out


def _run(tasks):
    for t in tasks:
        t.run()


class _Branch(NamedTuple):
    heads: int
    dk: int
    dv: int
    scale: float
    q: Any
    k: Any
    b: Any
    o: Any
    state: Any


def _slab(buf, hc, dv):
    per = buf.shape[2] // dv
    return hc // per, slice((hc % per) * dv, (hc % per + 1) * dv)


def _store_decay(b_s, n, bt):
    per = COL_TILE // LANES
    for j in range(per):
        b_s[n * per + j] = bt[:, j * LANES:(j + 1) * LANES]


def _store_values(w_s, n, vt, dk, dv, nchunks):
    per = COL_TILE // dv
    for j in range(per):
        for ci in range(nchunks):
            i, cs = _slab(w_s, (n * per + j) * nchunks + ci, dv)
            v = vt[ci * CHUNK:(ci + 1) * CHUNK, j * dv:(j + 1) * dv]
            w_s[i, dk:dk + CHUNK, cs] = v
            w_s[i, dk + CHUNK:dk + 2 * CHUNK, cs] = v


def _recurrence_tasks(br, qa_s, kd_s, w_s, u_s, nchunks):
    nheads, dk, dv = br.heads, br.dk, br.dv
    keep = _score_mask()
    pairs = [(ci, h) for ci in range(nchunks) for h in range(nheads)]
    rows_of = lambda ci: slice(ci * CHUNK, (ci + 1) * CHUNK)
    cols = []

    def operands(ci, h):
        rows, kcols = rows_of(ci), slice(h * dk, (h + 1) * dk)
        q, k, b = br.q[rows, kcols], br.k[rows, kcols], br.b[h, rows, :]
        if br.scale != 1.0:
            q = q * br.scale
        qa_s[h, rows, 0:dk] = (q * jnp.exp2(b)).astype(BF16)
        kd_s[h, rows, :] = (k * jnp.exp2(b[CHUNK - 1:CHUNK, :] - b)).astype(BF16)
        qa_s[h, rows, dk:dk + LANES] = _intra_scores(q, k, b, keep).astype(BF16)

    def outer(ci, h):
        i, cs = _slab(w_s, h * nchunks + ci, dv)
        u_s[i, :, cs] = _dot_tn(kd_s[h, rows_of(ci), :], w_s[i, dk:dk + CHUNK, cs])

    def decay_columns():
        half = CHUNK // 2
        for h in range(nheads):
            cols.append(_columns(jnp.exp2(br.b[h, pl.ds(half - 1, 2 * nchunks, stride=half), :])))

    def chain(ci, h):
        i, cs = _slab(w_s, h * nchunks + ci, dv)
        s = br.state[0, 0, h]
        w_s[i, 0:dk, cs] = s.astype(BF16)
        br.state[0, 0, h] = s * cols[h][:, 2 * ci + 1:2 * ci + 2] + u_s[i, :, cs]

    def output(ci, h):
        i, cs = _slab(w_s, h * nchunks + ci, dv)
        br.o[rows_of(ci), h * dv:(h + 1) * dv] = _dot(qa_s[h, rows_of(ci), :], w_s[i, :, cs])

    bind = lambda cost, fn: [_Task(cost, lambda ci=ci, h=h: fn(ci, h)) for ci, h in pairs]
    return bind(70, operands), (bind(32, outer) + [_Task(16 * nheads, decay_columns)]
                                + bind(8 + dv // 8, chain) + bind(16, output))


def _sample_state_tasks(row0, e_ref, k_ref, q_ref, v_ref, s_ref, snew_ref, o_ref, nheads, dk, dv):
    nseq = s_ref.shape[0]
    first = lax.broadcasted_iota(jnp.int32, (BF16_SUBLANES, dk), 0) == 0
    ks = lambda h: slice(h * dk, (h + 1) * dk)
    vs = lambda h: slice(h * dv, (h + 1) * dv)
    mine = lambda j, width: lax.broadcasted_iota(jnp.int32, (SUBLANES, width), 0) == row0 + j

    def vec(ref, j, cols):
        return jnp.sum(jnp.where(mine(j, cols.stop - cols.start), ref[:, cols], 0.0), axis=0, keepdims=True)

    tile = lambda ref, j, cols: jnp.broadcast_to(vec(ref, j, cols), (BF16_SUBLANES, cols.stop - cols.start))
    pairs = [(h, j) for h in range(nheads) for j in range(nseq)]

    def outer(h, j):
        kj = jnp.where(first, tile(k_ref, j, ks(h)), 0.0).astype(BF16)
        snew_ref[j, h] = _dot_tn(kj, tile(v_ref, j, vs(h)).astype(BF16))

    def update(h, j):
        ecol = jnp.broadcast_to(vec(e_ref, j, ks(h)), (dk, LANES)).T
        ecol = ecol if dv == LANES else jnp.concatenate([ecol] * (dv // LANES), axis=1)
        snew_ref[j, h] = s_ref[j, h] * ecol + snew_ref[j, h]

    def readout(h, j):
        o = _dot(tile(q_ref, j, ks(h)).astype(BF16), snew_ref[j, h].astype(BF16))
        o_ref[:, vs(h)] = jnp.where(mine(j, dv), o[0:SUBLANES, :], o_ref[:, vs(h)])

    bind = lambda cost, fn: [_Task(cost, lambda h=h, j=j: fn(h, j)) for h, j in pairs]
    return bind(40.0, outer), bind(30.0, update), bind(20.0, readout)


def _prompt_kernel(x_ref, seg_ref, skg_ref, sqg_ref, svg_ref, seh_ref, skh_ref, sqh_ref, svh_ref, ssg_ref, ssh_ref,
                   win_ref, wga_ref, wglr_ref, bg_ref, gng_ref, wbg_ref,
                   lbp_ref, hng_ref, wbh_ref, wout_ref, lng_ref, lnb_ref,
                   y_ref, sg_ref, sh_ref, ssg_new_ref, ssh_new_ref, soa_ref, sob_ref,
                   xb_s, tri_s, ga_s, qg_s, kg_s, bgl_s, qh_s, kh_s, bh_s, h3_s, qa_s, kd_s, w_s, u_s,
                   og_s, oh_s, sgr_s, shr_s, mga_s, mgb_s, g_s, m_s, p_s):
    tb = TOK_BLOCK
    nch = tb // CHUNK
    tiles = lambda width: range(width // COL_TILE)
    col = lambda n, base=0: slice(base + n * COL_TILE, base + (n + 1) * COL_TILE)
    gla = _Branch(GLA_H, GLA_DK, GLA_DV, GLA_DK ** -0.5, qg_s, kg_s, bgl_s, og_s, sg_ref)
    hgrn = _Branch(HG_H, HG_DK, HG_DV, 1.0, qh_s, kh_s, bh_s, oh_s, sh_ref)
    mm_cost = 256.0

    @pl.when(pl.program_id(1) == 0)
    def _():
        sg_ref[...] = jnp.zeros_like(sg_ref)
        sh_ref[...] = jnp.zeros_like(sh_ref)
        tri_s[...] = _block_tri(tb)

    nseq = ssg_ref.shape[0]
    lin = pl.program_id(0) * pl.num_programs(1) + pl.program_id(1)
    row0 = (lin % (SUBLANES // nseq)) * nseq

    @pl.when(row0 == 0)
    def _():
        soa_ref[...] = jnp.zeros_like(soa_ref)
        sob_ref[...] = jnp.zeros_like(sob_ref)

    xb_s[...] = x_ref[0].astype(BF16)
    lb = _lower_bound(lbp_ref[...])

    def mm(dst, w_ref, wcols, lhs=xb_s, cast=None, cost=mm_cost):
        def run():
            r = _dot(lhs[...], w_ref[:, wcols])
            dst(r if cast is None else r.astype(cast))
        return _Task(cost, run)

    def store(ref, cols):
        def put(v):
            ref[:, cols] = v
        return put

    def ew(cost, fn, rows=(0, tb), step=ROW_BLOCK):
        def run():
            for r in range(rows[0], rows[1], step):
                fn(slice(r, r + step))
        return _Task(cost, run)

    def cumsum_task(b_s, n):
        def run():
            t = tri_s[...]
            _store_decay(b_s, n, _dot(t, h3_s[0, :, col(n)]) + _dot(t, h3_s[1, :, col(n)]) + _dot(t, h3_s[2, :, col(n)]))
        return _Task(192.0, run)

    def put_split3(rb, cs, g):
        hi, mid, lo = _split3(g)
        h3_s[0, rb, cs] = hi
        h3_s[1, rb, cs] = mid
        h3_s[2, rb, cs] = lo

    def gla_logit(n):
        def run():
            qg_s[:, col(n)] = _dot(ga_s[...], wglr_ref[:, col(n)]) + bg_ref[:, col(n)]
        return _Task(64.0, run)

    def gla_log(n):
        def fn(rb):
            put_split3(rb, col(n), _log_sigmoid(qg_s[rb, col(n)]) * (LOG2_E / GLA_TAU))
        return ew(300.0, fn)

    gla_in = [mm(store(ga_s, slice(None)), wga_ref, slice(None), cast=BF16, cost=128.0)]
    gla_in += [mm(store(kg_s, col(n)), win_ref, col(n, C_GK)) for n in tiles(GLA_K)]
    gla_in += [gla_logit(n) for n in tiles(GLA_K)]
    gla_in += [t for n in tiles(GLA_K) for t in (
        mm(lambda v, n=n: _store_values(w_s, n, v, GLA_DK, GLA_DV, nch), win_ref, col(n, C_GV), cast=BF16),
        gla_log(n),
        mm(lambda v, n=n: _store_values(w_s, n + GLA_K // COL_TILE, v, GLA_DK, GLA_DV, nch), win_ref,
           col(n + GLA_K // COL_TILE, C_GV), cast=BF16))]
    gla_in += [cumsum_task(bgl_s, n) for n in tiles(GLA_K)]
    gla_in += [mm(store(qg_s, col(n)), win_ref, col(n, C_GQ)) for n in tiles(GLA_K)]
    _run(gla_in)

    def hq_ew(n):
        def fn(rb):
            qh_s[rb, col(n)] = _silu(qh_s[rb, col(n)])
        return ew(130.0, fn)

    def hf_ew(n):
        def fn(rb):
            sf, sfn = _sigmoid_pair(kh_s[rb, col(n)])
            lbn = lb[:, col(n)]
            kh_s[rb, col(n)] = (1.0 - lbn) * sfn
            put_split3(rb, col(n), jnp.log2(lbn + (1.0 - lbn) * sf))
        return ew(450.0, fn)

    def gate_ew(ref, act, n):
        def fn(rb):
            ref[rb, col(n)] = act(ref[rb, col(n)])
        return ew(130.0, fn)

    hg_in = [([mm(store(kh_s, col(n)), win_ref, col(n, C_HF)),
               mm(store(qh_s, col(n)), win_ref, col(n, C_HQ))],
              [hf_ew(n), hq_ew(n), cumsum_task(bh_s, n)]) for n in tiles(HG_K)]
    gates = [([mm(store(sgr_s, col(n)), win_ref, col(n, C_GR))], [gate_ew(sgr_s, _silu, n)]) for n in tiles(GLA_V)]
    gates += [([mm(store(mga_s, col(n)), win_ref, col(n, C_MA))], [gate_ew(mga_s, _sigmoid, n)])
              for n in tiles(D_MODEL)]
    gates_b = [([mm(store(shr_s, col(n)), win_ref, col(n, C_HR))], [gate_ew(shr_s, _silu, n)]) for n in tiles(HG_V)]
    hg_v = [mm(lambda v, n=n: _store_values(w_s, n, v, HG_DK, HG_DV, nch), win_ref, col(n, C_HI), cast=BF16)
            for n in tiles(HG_V)]
    mgb = [([mm(store(mgb_s, col(n)), win_ref, col(n, C_MB))], [gate_ew(mgb_s, _sigmoid, n)]) for n in tiles(D_MODEL)]

    ra_ops, ra_rest = _recurrence_tasks(gla, qa_s, kd_s, w_s, u_s, nch)
    s_outer, s_update, s_readout = [], [], []
    for args in ((seg_ref, skg_ref, sqg_ref, svg_ref, ssg_ref, ssg_new_ref, soa_ref, GLA_H, GLA_DK, GLA_DV),
                 (seh_ref, skh_ref, sqh_ref, svh_ref, ssh_ref, ssh_new_ref, sob_ref, HG_H, HG_DK, HG_DV)):
        outer, update, readout = _sample_state_tasks(row0, *args)
        s_outer += outer
        s_update += update
        s_readout += readout

    _run(_merge(ra_ops + ra_rest, _lagged(hg_in + gates), s_outer))

    def gla_norm(n):
        def fn(rb):
            g_s[rb, col(n)] = (_head_rmsnorm(og_s[rb, col(n)], gng_ref[:, col(n)], GLA_DV) * sgr_s[rb, col(n)]).astype(BF16)
        return ew(200.0, fn)

    def gla_scale(n):
        def fn(rb):
            p_s[rb, col(n)] = p_s[rb, col(n)] * mga_s[rb, col(n)]
        return ew(50.0, fn)

    gla_post = [gla_norm(n) for n in tiles(GLA_V)]
    for n in tiles(D_MODEL):
        gla_post += [mm(store(p_s, col(n)), wbg_ref, col(n), lhs=g_s), gla_scale(n)]
    rb_ops, rb_rest = _recurrence_tasks(hgrn, qa_s, kd_s, w_s, u_s, nch)
    _run(_merge(rb_ops, hg_v + _lagged(gates_b)))
    _run(_merge(rb_rest, gla_post, s_update))

    def hgrn_norm(n):
        def fn(rb):
            m_s[rb, col(n)] = (_head_rmsnorm(oh_s[rb, col(n)], hng_ref[:, col(n)], HG_DV) * shr_s[rb, col(n)]).astype(BF16)
        return ew(220.0, fn)

    def merge_ew(n):
        def fn(rb):
            g_s[rb, col(n)] = (p_s[rb, col(n)] + mgb_s[rb, col(n)] * sgr_s[rb, col(n)]).astype(BF16)
        return ew(60.0, fn)

    def final_ew(rows):
        def fn(rb):
            z = DEEPNORM_ALPHA * x_ref[0, rb, :] + shr_s[rb, :]
            y_ref[0, rb, :] = _layernorm(z, lng_ref[...], lnb_ref[...])
        return ew(180.0, fn, rows=rows, step=LN_ROW_BLOCK)

    _run(_merge([hgrn_norm(n) for n in tiles(HG_V)], _lagged(mgb)))
    _run([mm(store(sgr_s, col(n)), wbh_ref, col(n), lhs=m_s) for n in tiles(D_MODEL)])
    _run([merge_ew(n) for n in tiles(D_MODEL)])
    tail = [mm(store(shr_s, col(n)), wout_ref, col(n), lhs=g_s) for n in tiles(D_MODEL)]
    tail += [final_ew((r, r + ROW_BLOCK)) for r in range(0, tb, ROW_BLOCK)]
    _run(_merge(tail, s_readout))


def _whole(shape):
    return pl.BlockSpec(memory_space=pltpu.VMEM)


def _prompt_call(x, sample_rows, sample_states, win, wga, wglr, bg, gng, wbg, lbp, hng, wbh, wout, lng, lnb):
    bsz, seq, d = x.shape
    tb = TOK_BLOCK
    nch = tb // CHUNK
    per_row = seq // tb
    steps = bsz * per_row
    nsample = sample_states[0].shape[0]
    nseq = nsample // steps
    assert nseq * steps == nsample and SUBLANES % nseq == 0 and all(r.shape[0] == nsample for r in sample_rows)
    weights = (win, wga, wglr, bg, gng, wbg, lbp, hng, wbh, wout, lng, lnb)
    lin = lambda b, t: b * per_row + t
    row_block = lambda b, t: (lin(b, t) // (SUBLANES // nseq), 0)
    row_spec = lambda r: pl.BlockSpec((SUBLANES, r.shape[1]), row_block)
    state_spec = lambda s: pl.BlockSpec((nseq,) + s.shape[1:], lambda b, t: (lin(b, t), 0, 0, 0))
    return pl.pallas_call(
        _prompt_kernel,
        grid=(bsz, per_row),
        in_specs=[pl.BlockSpec((1, tb, d), lambda b, t: (b, t, 0))] + [row_spec(r) for r in sample_rows]
        + [state_spec(s) for s in sample_states] + [_whole(w.shape) for w in weights],
        out_specs=[
            pl.BlockSpec((1, tb, d), lambda b, t: (b, t, 0)),
            pl.BlockSpec((1, 1, GLA_H, GLA_DK, GLA_DV), lambda b, t: (0, b, 0, 0, 0)),
            pl.BlockSpec((1, 1, HG_H, HG_DK, HG_DV), lambda b, t: (0, b, 0, 0, 0)),
            state_spec(sample_states[0]), state_spec(sample_states[1]),
            pl.BlockSpec((SUBLANES, GLA_V), row_block),
            pl.BlockSpec((SUBLANES, HG_V), row_block),
        ],
        out_shape=[
            jax.ShapeDtypeStruct((bsz, seq, d), F32),
            jax.ShapeDtypeStruct((DEPTH, bsz, GLA_H, GLA_DK, GLA_DV), F32),
            jax.ShapeDtypeStruct((DEPTH, bsz, HG_H, HG_DK, HG_DV), F32),
            jax.ShapeDtypeStruct(sample_states[0].shape, F32),
            jax.ShapeDtypeStruct(sample_states[1].shape, F32),
            jax.ShapeDtypeStruct((nsample, GLA_V), F32),
            jax.ShapeDtypeStruct((nsample, HG_V), F32),
        ],
        scratch_shapes=[
            pltpu.VMEM((tb, d), BF16),
            pltpu.VMEM((tb, tb), BF16),
            pltpu.VMEM((tb, LANES), BF16),
            pltpu.VMEM((tb, GLA_K), F32),
            pltpu.VMEM((tb, GLA_K), F32),
            pltpu.VMEM((GLA_H, tb, LANES), F32),
            pltpu.VMEM((tb, HG_K), F32),
            pltpu.VMEM((tb, HG_K), F32),
            pltpu.VMEM((HG_H, tb, LANES), F32),
            pltpu.VMEM((3, tb, HG_K), BF16),
            pltpu.VMEM((HG_H, tb, 2 * LANES), BF16),
            pltpu.VMEM((HG_H, tb, LANES), BF16),
            pltpu.VMEM((GLA_H * nch, 2 * LANES, GLA_DV), BF16),
            pltpu.VMEM((GLA_H * nch, GLA_DK, GLA_DV), F32),
            pltpu.VMEM((tb, GLA_V), F32),
            pltpu.VMEM((tb, HG_V), F32),
            pltpu.VMEM((tb, GLA_V), F32),
            pltpu.VMEM((tb, HG_V), F32),
            pltpu.VMEM((tb, d), F32),
            pltpu.VMEM((tb, d), F32),
            pltpu.VMEM((tb, d), BF16),
            pltpu.VMEM((tb, d), BF16),
            pltpu.VMEM((tb, d), F32),
        ],
        compiler_params=pltpu.CompilerParams(
            dimension_semantics=("arbitrary", "arbitrary"), vmem_limit_bytes=VMEM_LIMIT),
        name="prompt_layer",
    )(x, *sample_rows, *sample_states, *weights)


def _sample_proj_kernel(x_ref, win_ref, wga_ref, wglr_ref, bg_ref, lbp_ref,
                        eg_ref, kg_ref, qg_ref, vg_ref, eh_ref, kh_ref, qh_ref, vh_ref,
                        sgr_ref, shr_ref, mga_ref, mgb_ref):
    xb = x_ref[...].astype(BF16)
    proj = lambda base, width: _dot(xb, win_ref[:, base:base + width])
    qg_ref[...] = proj(C_GQ, GLA_K) * (GLA_DK ** -0.5)
    kg_ref[...] = proj(C_GK, GLA_K)
    vg_ref[...] = proj(C_GV, GLA_V)
    sgr_ref[...] = _silu(proj(C_GR, GLA_V))
    ga = _dot(xb, wga_ref[...]).astype(BF16)
    a_logit = _dot(ga, wglr_ref[...]) + bg_ref[...]
    eg_ref[...] = jnp.exp(_log_sigmoid(a_logit) * (1.0 / GLA_TAU))
    lb = _lower_bound(lbp_ref[...])
    qh_ref[...] = _silu(proj(C_HQ, HG_K))
    sf, sfn = _sigmoid_pair(proj(C_HF, HG_K))
    eh_ref[...] = jnp.exp(jnp.log(lb + (1.0 - lb) * sf))
    kh_ref[...] = (1.0 - lb) * sfn
    vh_ref[...] = proj(C_HI, HG_V)
    shr_ref[...] = _silu(proj(C_HR, HG_V))
    mga_ref[...] = _sigmoid(proj(C_MA, D_MODEL))
    mgb_ref[...] = _sigmoid(proj(C_MB, D_MODEL))


def _sample_proj_call(x, win, wga, wglr, bg, lbp):
    n = x.shape[0]
    widths = (GLA_K, GLA_K, GLA_K, GLA_V, HG_K, HG_K, HG_K, HG_V, GLA_V, HG_V, D_MODEL, D_MODEL)
    args = (x, win, wga, wglr, bg, lbp)
    return pl.pallas_call(
        _sample_proj_kernel,
        in_specs=[_whole(a.shape) for a in args],
        out_specs=[_whole((n, w)) for w in widths],
        out_shape=[jax.ShapeDtypeStruct((n, w), F32) for w in widths],
        compiler_params=pltpu.CompilerParams(vmem_limit_bytes=VMEM_LIMIT),
        name="sample_proj",
    )(*args)


def _sample_post_kernel(x_ref, oa_ref, ob_ref, sgr_ref, shr_ref, mga_ref, mgb_ref,
                        gng_ref, wbg_ref, hng_ref, wbh_ref, wout_ref, lng_ref, lnb_ref, y_ref):
    ga = (_head_rmsnorm(oa_ref[...], gng_ref[...], GLA_DV) * sgr_ref[...]).astype(BF16)
    gb = (_head_rmsnorm(ob_ref[...], hng_ref[...], HG_DV) * shr_ref[...]).astype(BF16)
    merged = mga_ref[...] * _dot(ga, wbg_ref[...]) + mgb_ref[...] * _dot(gb, wbh_ref[...])
    z = DEEPNORM_ALPHA * x_ref[...] + _dot(merged.astype(BF16), wout_ref[...])
    y_ref[...] = _layernorm(z, lng_ref[...], lnb_ref[...])


def _sample_post_call(*args):
    n = args[0].shape[0]
    return pl.pallas_call(
        _sample_post_kernel,
        in_specs=[_whole(a.shape) for a in args],
        out_specs=_whole((n, D_MODEL)),
        out_shape=jax.ShapeDtypeStruct((n, D_MODEL), F32),
        compiler_params=pltpu.CompilerParams(vmem_limit_bytes=VMEM_LIMIT),
        name="sample_post",
    )(*args)


def _weight_prep_kernel(wt_ref, gat_ref, wa_ref, wb_ref, wc_ref, win_ref, wga_ref, oa_ref, ob_ref, oc_ref):
    win_ref[...] = wt_ref[...].T.astype(BF16)
    for w_ref, o_ref in ((wa_ref, oa_ref), (wb_ref, ob_ref), (wc_ref, oc_ref)):
        o_ref[...] = w_ref[0].astype(BF16)

    @pl.when(pl.program_id(0) == 0)
    def _():
        d = gat_ref.shape[1]
        ga = jnp.concatenate([gat_ref[...], jnp.zeros((LANES - GLA_RANK, d), F32)], axis=0)
        wga_ref[...] = ga.T.astype(BF16)


def _weight_prep_call(w_in, squares):
    _, d, width = w_in.shape
    wt = jnp.swapaxes(w_in, 1, 2)[0]
    blk = WEIGHT_PREP_COLS
    assert C_HQ % blk == 0 and (width - GLA_RANK) % blk == 0
    steps = (width - GLA_RANK) // blk
    rows = WEIGHT_PREP_ROWS
    assert d % rows == 0 and d // rows <= steps
    src_row = lambda i: (i * (blk // GLA_RANK) + (i >= C_HQ // blk).astype(jnp.int32)) * GLA_RANK
    row_block = lambda i: jnp.minimum(i, d // rows - 1)
    return pl.pallas_call(
        _weight_prep_kernel,
        grid=(steps,),
        in_specs=[pl.BlockSpec((pl.Element(blk), pl.Element(d)), lambda i: (src_row(i), 0)),
                  pl.BlockSpec((pl.Element(GLA_RANK), pl.Element(d)), lambda i: (C_HQ, 0))]
        + [pl.BlockSpec((1, rows, d), lambda i: (0, row_block(i), 0)) for _ in squares],
        out_specs=[pl.BlockSpec((d, blk), lambda i: (0, i)),
                   pl.BlockSpec((d, LANES), lambda i: (0, 0))]
        + [pl.BlockSpec((rows, d), lambda i: (row_block(i), 0)) for _ in squares],
        out_shape=[jax.ShapeDtypeStruct((d, width - GLA_RANK), BF16), jax.ShapeDtypeStruct((d, LANES), BF16)]
        + [jax.ShapeDtypeStruct((d, d), BF16) for _ in squares],
        compiler_params=pltpu.CompilerParams(dimension_semantics=("arbitrary",), vmem_limit_bytes=VMEM_LIMIT),
        name="weight_prep",
    )(wt, wt, *squares)


def kernel(x_prompt, x_sample, state_gla, state_hgrn, w_in, w_gate_lr, b_gate_lr, gla_norm_g, w_br_gla,
           hgrn_lb_param, hgrn_norm_g, w_br_hgrn, w_out, ln_g, ln_b):
    assert w_in.shape[0] == DEPTH and x_sample.shape[1] == 1
    win, wga, wbg, wbh, wout = _weight_prep_call(w_in, (w_br_gla, w_br_hgrn, w_out))
    wglr = jnp.pad(w_gate_lr[0], ((0, LANES - GLA_RANK), (0, 0))).astype(BF16)
    bg = b_gate_lr[0].reshape(1, GLA_K)
    gng = gla_norm_g[0].reshape(1, GLA_V)
    hng = hgrn_norm_g[0].reshape(1, HG_V)
    lng = ln_g[0].reshape(1, D_MODEL)
    lnb = ln_b[0].reshape(1, D_MODEL)
    lbp = hgrn_lb_param

    xs = x_sample[:, 0, :]
    *rows, sgr, shr, mga, mgb = _sample_proj_call(xs, win, wga, wglr, bg, lbp)

    y_prompt, gla_p, hgrn_p, gla_s, hgrn_s, oa, ob = _prompt_call(
        x_prompt, rows, (state_gla[0], state_hgrn[0]), win, wga, wglr, bg, gng, wbg, lbp, hng, wbh, wout, lng, lnb)

    ys = _sample_post_call(xs, oa, ob, sgr, shr, mga, mgb, gng, wbg, hng, wbh, wout, lng, lnb)
    return (y_prompt, ys[:, None, :], gla_p, hgrn_p, gla_s[None], hgrn_s[None])
```

```python
from typing import Any, NamedTuple

import jax
import jax.numpy as jnp
from jax import lax
from jax.experimental import pallas as pl
from jax.experimental.pallas import tpu as pltpu

F32 = jnp.float32
BF16 = jnp.bfloat16

D_MODEL = 1024
GLA_H, GLA_DK, GLA_DV = 4, 128, 256
HG_H, HG_DK, HG_DV = 8, 128, 128
GLA_RANK = 16
GLA_TAU = 16.0
LOG2_E = 1.4426950408889634
GLA_K = GLA_H * GLA_DK
GLA_V = GLA_H * GLA_DV
HG_K = HG_H * HG_DK
HG_V = HG_H * HG_DV
C_GQ, C_GK, C_GV, C_GR = 0, GLA_K, 2 * GLA_K, 2 * GLA_K + GLA_V
C_HQ = C_GR + GLA_V
C_HF, C_HI, C_HR = C_HQ + HG_K, C_HQ + 2 * HG_K, C_HQ + 2 * HG_K + HG_V
C_MA = C_HR + HG_V
C_MB = C_MA + D_MODEL
CHUNK = 64
SUB = 16
NORM_EPS = 1e-5
DEPTH = 1
DEEPNORM_ALPHA = (2.0 * DEPTH) ** 0.25
LANES = 128
SUBLANES = 8
BF16_SUBLANES = 16
TOK_BLOCK = 256
COL_TILE = 256
ROW_BLOCK = 64
LN_ROW_BLOCK = 16
WEIGHT_PREP_COLS = 1024
VMEM_LIMIT = 60 * 1024 * 1024


def _dot(a, b):
    return jnp.dot(a, b, preferred_element_type=F32)


def _dot_nt(a, b):
    return lax.dot_general(a, b, (((1,), (1,)), ((), ())), preferred_element_type=F32)


def _dot_tn(a, b):
    return lax.dot_general(a, b, (((0,), (0,)), ((), ())), preferred_element_type=F32)


def _sigmoid(x):
    return 1.0 / (1.0 + jnp.exp(-x))


def _sigmoid_pair(x):
    e = jnp.exp(-jnp.abs(x))
    s = 1.0 / (1.0 + e)
    t = e * s
    pos = x >= 0.0
    return jnp.where(pos, s, t), jnp.where(pos, t, s)


def _silu(x):
    return x * _sigmoid(x)


def _log_sigmoid(x):
    return jnp.minimum(x, 0.0) - jnp.log1p(jnp.exp(-jnp.abs(x)))


def _split3(x):
    hi = x.astype(BF16)
    r = x - hi.astype(F32)
    mid = r.astype(BF16)
    lo = (r - mid.astype(F32)).astype(BF16)
    return hi, mid, lo


def _block_tri(n):
    r = lax.broadcasted_iota(jnp.int32, (n, n), 0)
    c = lax.broadcasted_iota(jnp.int32, (n, n), 1)
    shift = CHUNK.bit_length() - 1
    keep = (c <= r) & (jnp.right_shift(r, shift) == jnp.right_shift(c, shift))
    return jnp.where(keep, 1.0, 0.0).astype(BF16)


def _lower_bound(lbp):
    rows = [lbp[i:i + 1, :] for i in range(lbp.shape[0])]
    m = rows[0]
    for r in rows[1:]:
        m = jnp.maximum(m, r)
    es = [jnp.exp(r - m) for r in rows]
    tot = es[0]
    for e in es[1:]:
        tot = tot + e
    return es[0] / tot


def _score_mask():
    t = lax.broadcasted_iota(jnp.int32, (CHUNK, LANES), 0)
    lane = lax.broadcasted_iota(jnp.int32, (CHUNK, LANES), 1)
    s = lane - CHUNK * (jnp.right_shift(t, SUB.bit_length() - 1) & 1)
    return (s >= 0) & (s <= t)


def _intra_scores(q, k, b, keep):
    dk = q.shape[1]
    nsub = CHUNK // SUB
    qs, ks = [], []
    for i in range(nsub):
        lo, hi = i * SUB, (i + 1) * SUB
        bs = b[lo:lo + 1, :]
        qs.append(q[lo:hi] * jnp.exp2(b[lo:hi] - bs))
        ks.append((k[0:hi] * jnp.exp2(bs - b[0:hi])).astype(BF16))
        if hi < CHUNK:
            ks.append(jnp.zeros((CHUNK - hi, dk), BF16))
    prod = _dot_nt(jnp.concatenate(qs, axis=0).astype(BF16), jnp.concatenate(ks, axis=0))
    per = LANES // CHUNK
    a = jnp.concatenate([prod[i * SUB:(i + 1) * SUB, (i // per) * LANES:(i // per + 1) * LANES]
                         for i in range(nsub)], axis=0)
    return jnp.where(keep, a, 0.0)


def _columns(rows):
    r = rows.shape[0]
    if r < LANES:
        rows = jnp.concatenate([rows, jnp.zeros((LANES - r, LANES), F32)], axis=0)
    return rows.T


def _head_rmsnorm(o, gain, dv):
    outs = []
    for j in range(o.shape[1] // dv):
        oj = o[:, j * dv:(j + 1) * dv]
        ms = jnp.mean(oj * oj, axis=-1, keepdims=True)
        outs.append(oj * lax.rsqrt(ms + NORM_EPS))
    on = outs[0] if len(outs) == 1 else jnp.concatenate(outs, axis=1)
    return on * gain


def _layernorm(z, g, b):
    mu = jnp.mean(z, axis=-1, keepdims=True)
    zc = z - mu
    var = jnp.mean(zc * zc, axis=-1, keepdims=True)
    return zc * lax.rsqrt(var + NORM_EPS) * g + b


class _Task(NamedTuple):
    cost: float
    run: Any


def _merge(*lists):
    totals = [sum(t.cost for t in l) or 1.0 for l in lists]
    pos = [0] * len(lists)
    done = [0.0] * len(lists)
    out = []
    for _ in range(sum(len(l) for l in lists)):
        live = [i for i in range(len(lists)) if pos[i] < len(lists[i])]
        j = min(live, key=lambda i: (done[i] + 0.5 * lists[i][pos[i]].cost) / totals[i])
        out.append(lists[j][pos[j]])
        done[j] += lists[j][pos[j]].cost
        pos[j] += 1
    return out


def _lagged(groups, lag=1):
    out = []
    for i, (mms, _) in enumerate(groups):
        out += mms
        if i >= lag:
            out += groups[i - lag][1]
    for _, ews in groups[max(len(groups) - lag, 0):]:
        out += ews
    return out


def _run(tasks):
    for t in tasks:
        t.run()


class _Branch(NamedTuple):
    heads: int
    dk: int
    dv: int
    scale: float
    q: Any
    k: Any
    b: Any
    o: Any
    state: Any


def _slab(buf, hc, dv):
    per = buf.shape[2] // dv
    return hc // per, slice((hc % per) * dv, (hc % per + 1) * dv)


def _store_decay(b_s, n, bt):
    per = COL_TILE // LANES
    for j in range(per):
        b_s[n * per + j] = bt[:, j * LANES:(j + 1) * LANES]


def _store_values(w_s, n, vt, dk, dv, nchunks):
    per = COL_TILE // dv
    for j in range(per):
        for ci in range(nchunks):
            i, cs = _slab(w_s, (n * per + j) * nchunks + ci, dv)
            v = vt[ci * CHUNK:(ci + 1) * CHUNK, j * dv:(j + 1) * dv]
            w_s[i, dk:dk + CHUNK, cs] = v
            w_s[i, dk + CHUNK:dk + 2 * CHUNK, cs] = v


def _recurrence_tasks(br, qa_s, kd_s, w_s, u_s, nchunks):
    nheads, dk, dv = br.heads, br.dk, br.dv
    keep = _score_mask()
    pairs = [(ci, h) for ci in range(nchunks) for h in range(nheads)]
    rows_of = lambda ci: slice(ci * CHUNK, (ci + 1) * CHUNK)
    cols = []

    def operands(ci, h):
        rows, kcols = rows_of(ci), slice(h * dk, (h + 1) * dk)
        q, k, b = br.q[rows, kcols], br.k[rows, kcols], br.b[h, rows, :]
        if br.scale != 1.0:
            q = q * br.scale
        qa_s[h, rows, 0:dk] = (q * jnp.exp2(b)).astype(BF16)
        kd_s[h, rows, :] = (k * jnp.exp2(b[CHUNK - 1:CHUNK, :] - b)).astype(BF16)
        qa_s[h, rows, dk:dk + LANES] = _intra_scores(q, k, b, keep).astype(BF16)

    def outer(ci, h):
        i, cs = _slab(w_s, h * nchunks + ci, dv)
        u_s[i, :, cs] = _dot_tn(kd_s[h, rows_of(ci), :], w_s[i, dk:dk + CHUNK, cs])

    def decay_columns():
        half = CHUNK // 2
        for h in range(nheads):
            cols.append(_columns(jnp.exp2(br.b[h, pl.ds(half - 1, 2 * nchunks, stride=half), :])))

    def chain(ci, h):
        i, cs = _slab(w_s, h * nchunks + ci, dv)
        s = br.state[0, 0, h]
        w_s[i, 0:dk, cs] = s.astype(BF16)
        br.state[0, 0, h] = s * cols[h][:, 2 * ci + 1:2 * ci + 2] + u_s[i, :, cs]

    def output(ci, h):
        i, cs = _slab(w_s, h * nchunks + ci, dv)
        br.o[rows_of(ci), h * dv:(h + 1) * dv] = _dot(qa_s[h, rows_of(ci), :], w_s[i, :, cs])

    bind = lambda cost, fn: [_Task(cost, lambda ci=ci, h=h: fn(ci, h)) for ci, h in pairs]
    return bind(70, operands), (bind(32, outer) + [_Task(16 * nheads, decay_columns)]
                                + bind(8 + dv // 8, chain) + bind(16, output))


def _sample_state_tasks(row0, e_ref, k_ref, q_ref, v_ref, s_ref, snew_ref, o_ref, nheads, dk, dv):
    nseq = s_ref.shape[0]
    first = lax.broadcasted_iota(jnp.int32, (BF16_SUBLANES, dk), 0) == 0
    ks = lambda h: slice(h * dk, (h + 1) * dk)
    vs = lambda h: slice(h * dv, (h + 1) * dv)
    mine = lambda j, width: lax.broadcasted_iota(jnp.int32, (SUBLANES, width), 0) == row0 + j

    def vec(ref, j, cols):
        return jnp.sum(jnp.where(mine(j, cols.stop - cols.start), ref[:, cols], 0.0), axis=0, keepdims=True)

    tile = lambda ref, j, cols: jnp.broadcast_to(vec(ref, j, cols), (BF16_SUBLANES, cols.stop - cols.start))
    pairs = [(h, j) for h in range(nheads) for j in range(nseq)]

    def outer(h, j):
        kj = jnp.where(first, tile(k_ref, j, ks(h)), 0.0).astype(BF16)
        snew_ref[j, h] = _dot_tn(kj, tile(v_ref, j, vs(h)).astype(BF16))

    def update(h, j):
        ecol = jnp.broadcast_to(vec(e_ref, j, ks(h)), (dk, LANES)).T
        ecol = ecol if dv == LANES else jnp.concatenate([ecol] * (dv // LANES), axis=1)
        snew_ref[j, h] = s_ref[j, h] * ecol + snew_ref[j, h]

    def readout(h, j):
        o = _dot(tile(q_ref, j, ks(h)).astype(BF16), snew_ref[j, h].astype(BF16))
        o_ref[:, vs(h)] = jnp.where(mine(j, dv), o[0:SUBLANES, :], o_ref[:, vs(h)])

    bind = lambda cost, fn: [_Task(cost, lambda h=h, j=j: fn(h, j)) for h, j in pairs]
    return bind(40.0, outer), bind(30.0, update), bind(20.0, readout)


def _prompt_kernel(x_ref, seg_ref, skg_ref, sqg_ref, svg_ref, seh_ref, skh_ref, sqh_ref, svh_ref, ssg_ref, ssh_ref,
                   win_ref, wga_ref, wglr_ref, bg_ref, gng_ref, wbg_ref,
                   lbp_ref, hng_ref, wbh_ref, wout_ref, lng_ref, lnb_ref,
                   y_ref, sg_ref, sh_ref, ssg_new_ref, ssh_new_ref, soa_ref, sob_ref,
                   xb_s, tri_s, ga_s, qg_s, kg_s, bgl_s, qh_s, kh_s, bh_s, h3_s, qa_s, kd_s, w_s, u_s,
                   og_s, oh_s, sgr_s, shr_s, mga_s, mgb_s, g_s, m_s, p_s):
    tb = TOK_BLOCK
    nch = tb // CHUNK
    tiles = lambda width: range(width // COL_TILE)
    col = lambda n, base=0: slice(base + n * COL_TILE, base + (n + 1) * COL_TILE)
    gla = _Branch(GLA_H, GLA_DK, GLA_DV, GLA_DK ** -0.5, qg_s, kg_s, bgl_s, og_s, sg_ref)
    hgrn = _Branch(HG_H, HG_DK, HG_DV, 1.0, qh_s, kh_s, bh_s, oh_s, sh_ref)
    mm_cost = 256.0

    @pl.when(pl.program_id(1) == 0)
    def _():
        sg_ref[...] = jnp.zeros_like(sg_ref)
        sh_ref[...] = jnp.zeros_like(sh_ref)
        tri_s[...] = _block_tri(tb)

    nseq = ssg_ref.shape[0]
    lin = pl.program_id(0) * pl.num_programs(1) + pl.program_id(1)
    row0 = (lin % (SUBLANES // nseq)) * nseq

    @pl.when(row0 == 0)
    def _():
        soa_ref[...] = jnp.zeros_like(soa_ref)
        sob_ref[...] = jnp.zeros_like(sob_ref)

    xb_s[...] = x_ref[0].astype(BF16)
    lb = _lower_bound(lbp_ref[...])

    def mm(dst, w_ref, wcols, lhs=xb_s, cast=None, cost=mm_cost):
        def run():
            r = _dot(lhs[...], w_ref[:, wcols])
            dst(r if cast is None else r.astype(cast))
        return _Task(cost, run)

    def store(ref, cols):
        def put(v):
            ref[:, cols] = v
        return put

    def ew(cost, fn, rows=(0, tb), step=ROW_BLOCK):
        def run():
            for r in range(rows[0], rows[1], step):
                fn(slice(r, r + step))
        return _Task(cost, run)

    def cumsum_task(b_s, n):
        def run():
            t = tri_s[...]
            _store_decay(b_s, n, _dot(t, h3_s[0, :, col(n)]) + _dot(t, h3_s[1, :, col(n)]) + _dot(t, h3_s[2, :, col(n)]))
        return _Task(192.0, run)

    def put_split3(rb, cs, g):
        hi, mid, lo = _split3(g)
        h3_s[0, rb, cs] = hi
        h3_s[1, rb, cs] = mid
        h3_s[2, rb, cs] = lo

    def gla_logit(n):
        def run():
            qg_s[:, col(n)] = _dot(ga_s[...], wglr_ref[:, col(n)]) + bg_ref[:, col(n)]
        return _Task(64.0, run)

    def gla_log(n):
        def fn(rb):
            put_split3(rb, col(n), _log_sigmoid(qg_s[rb, col(n)]) * (LOG2_E / GLA_TAU))
        return ew(300.0, fn)

    gla_in = [mm(store(ga_s, slice(None)), wga_ref, slice(None), cast=BF16, cost=128.0)]
    gla_in += [mm(store(kg_s, col(n)), win_ref, col(n, C_GK)) for n in tiles(GLA_K)]
    gla_in += [gla_logit(n) for n in tiles(GLA_K)]
    gla_in += [t for n in tiles(GLA_K) for t in (
        mm(lambda v, n=n: _store_values(w_s, n, v, GLA_DK, GLA_DV, nch), win_ref, col(n, C_GV), cast=BF16),
        gla_log(n),
        mm(lambda v, n=n: _store_values(w_s, n + GLA_K // COL_TILE, v, GLA_DK, GLA_DV, nch), win_ref,
           col(n + GLA_K // COL_TILE, C_GV), cast=BF16))]
    gla_in += [cumsum_task(bgl_s, n) for n in tiles(GLA_K)]
    gla_in += [mm(store(qg_s, col(n)), win_ref, col(n, C_GQ)) for n in tiles(GLA_K)]
    _run(gla_in)

    def hq_ew(n):
        def fn(rb):
            qh_s[rb, col(n)] = _silu(qh_s[rb, col(n)])
        return ew(130.0, fn)

    def hf_ew(n):
        def fn(rb):
            sf, sfn = _sigmoid_pair(kh_s[rb, col(n)])
            lbn = lb[:, col(n)]
            kh_s[rb, col(n)] = (1.0 - lbn) * sfn
            put_split3(rb, col(n), jnp.log2(lbn + (1.0 - lbn) * sf))
        return ew(450.0, fn)

    def gate_ew(ref, act, n):
        def fn(rb):
            ref[rb, col(n)] = act(ref[rb, col(n)])
        return ew(130.0, fn)

    hg_in = [([mm(store(kh_s, col(n)), win_ref, col(n, C_HF)),
               mm(store(qh_s, col(n)), win_ref, col(n, C_HQ))],
              [hf_ew(n), hq_ew(n), cumsum_task(bh_s, n)]) for n in tiles(HG_K)]
    gates = [([mm(store(sgr_s, col(n)), win_ref, col(n, C_GR))], [gate_ew(sgr_s, _silu, n)]) for n in tiles(GLA_V)]
    gates += [([mm(store(mga_s, col(n)), win_ref, col(n, C_MA))], [gate_ew(mga_s, _sigmoid, n)])
              for n in tiles(D_MODEL)]
    gates_b = [([mm(store(shr_s, col(n)), win_ref, col(n, C_HR))], [gate_ew(shr_s, _silu, n)]) for n in tiles(HG_V)]
    hg_v = [mm(lambda v, n=n: _store_values(w_s, n, v, HG_DK, HG_DV, nch), win_ref, col(n, C_HI), cast=BF16)
            for n in tiles(HG_V)]
    mgb = [([mm(store(mgb_s, col(n)), win_ref, col(n, C_MB))], [gate_ew(mgb_s, _sigmoid, n)]) for n in tiles(D_MODEL)]

    ra_ops, ra_rest = _recurrence_tasks(gla, qa_s, kd_s, w_s, u_s, nch)
    s_outer, s_update, s_readout = [], [], []
    for args in ((seg_ref, skg_ref, sqg_ref, svg_ref, ssg_ref, ssg_new_ref, soa_ref, GLA_H, GLA_DK, GLA_DV),
                 (seh_ref, skh_ref, sqh_ref, svh_ref, ssh_ref, ssh_new_ref, sob_ref, HG_H, HG_DK, HG_DV)):
        outer, update, readout = _sample_state_tasks(row0, *args)
        s_outer += outer
        s_update += update
        s_readout += readout

    _run(_merge(ra_ops + ra_rest, _lagged(hg_in + gates), s_outer))

    def gla_norm(n):
        def fn(rb):
            g_s[rb, col(n)] = (_head_rmsnorm(og_s[rb, col(n)], gng_ref[:, col(n)], GLA_DV) * sgr_s[rb, col(n)]).astype(BF16)
        return ew(200.0, fn)

    def gla_scale(n):
        def fn(rb):
            p_s[rb, col(n)] = p_s[rb, col(n)] * mga_s[rb, col(n)]
        return ew(50.0, fn)

    gla_post = [gla_norm(n) for n in tiles(GLA_V)]
    for n in tiles(D_MODEL):
        gla_post += [mm(store(p_s, col(n)), wbg_ref, col(n), lhs=g_s), gla_scale(n)]
    rb_ops, rb_rest = _recurrence_tasks(hgrn, qa_s, kd_s, w_s, u_s, nch)
    _run(_merge(rb_ops, hg_v + _lagged(gates_b)))
    _run(_merge(rb_rest, gla_post, s_update))

    def hgrn_norm(n):
        def fn(rb):
            m_s[rb, col(n)] = (_head_rmsnorm(oh_s[rb, col(n)], hng_ref[:, col(n)], HG_DV) * shr_s[rb, col(n)]).astype(BF16)
        return ew(220.0, fn)

    def merge_ew(n):
        def fn(rb):
            g_s[rb, col(n)] = (p_s[rb, col(n)] + mgb_s[rb, col(n)] * sgr_s[rb, col(n)]).astype(BF16)
        return ew(60.0, fn)

    def final_ew(rows):
        def fn(rb):
            z = DEEPNORM_ALPHA * x_ref[0, rb, :] + shr_s[rb, :]
            y_ref[0, rb, :] = _layernorm(z, lng_ref[...], lnb_ref[...])
        return ew(180.0, fn, rows=rows, step=LN_ROW_BLOCK)

    _run(_merge([hgrn_norm(n) for n in tiles(HG_V)], _lagged(mgb)))
    _run([mm(store(sgr_s, col(n)), wbh_ref, col(n), lhs=m_s) for n in tiles(D_MODEL)])
    _run([merge_ew(n) for n in tiles(D_MODEL)])
    tail = [mm(store(shr_s, col(n)), wout_ref, col(n), lhs=g_s) for n in tiles(D_MODEL)]
    tail += [final_ew((r, r + ROW_BLOCK)) for r in range(0, tb, ROW_BLOCK)]
    _run(_merge(tail, s_readout))


def _whole(shape):
    return pl.BlockSpec(memory_space=pltpu.VMEM)


def _prompt_call(x, sample_rows, sample_states, win, wga, wglr, bg, gng, wbg, lbp, hng, wbh, wout, lng, lnb):
    bsz, seq, d = x.shape
    tb = TOK_BLOCK
    nch = tb // CHUNK
    per_row = seq // tb
    steps = bsz * per_row
    nsample = sample_states[0].shape[0]
    nseq = nsample // steps
    assert nseq * steps == nsample and SUBLANES % nseq == 0 and all(r.shape[0] == nsample for r in sample_rows)
    weights = (win, wga, wglr, bg, gng, wbg, lbp, hng, wbh, wout, lng, lnb)
    lin = lambda b, t: b * per_row + t
    row_block = lambda b, t: (lin(b, t) // (SUBLANES // nseq), 0)
    row_spec = lambda r: pl.BlockSpec((SUBLANES, r.shape[1]), row_block)
    state_spec = lambda s: pl.BlockSpec((nseq,) + s.shape[1:], lambda b, t: (lin(b, t), 0, 0, 0))
    return pl.pallas_call(
        _prompt_kernel,
        grid=(bsz, per_row),
        in_specs=[pl.BlockSpec((1, tb, d), lambda b, t: (b, t, 0))] + [row_spec(r) for r in sample_rows]
        + [state_spec(s) for s in sample_states] + [_whole(w.shape) for w in weights],
        out_specs=[
            pl.BlockSpec((1, tb, d), lambda b, t: (b, t, 0)),
            pl.BlockSpec((1, 1, GLA_H, GLA_DK, GLA_DV), lambda b, t: (0, b, 0, 0, 0)),
            pl.BlockSpec((1, 1, HG_H, HG_DK, HG_DV), lambda b, t: (0, b, 0, 0, 0)),
            state_spec(sample_states[0]), state_spec(sample_states[1]),
            pl.BlockSpec((SUBLANES, GLA_V), row_block),
            pl.BlockSpec((SUBLANES, HG_V), row_block),
        ],
        out_shape=[
            jax.ShapeDtypeStruct((bsz, seq, d), F32),
            jax.ShapeDtypeStruct((DEPTH, bsz, GLA_H, GLA_DK, GLA_DV), F32),
            jax.ShapeDtypeStruct((DEPTH, bsz, HG_H, HG_DK, HG_DV), F32),
            jax.ShapeDtypeStruct(sample_states[0].shape, F32),
            jax.ShapeDtypeStruct(sample_states[1].shape, F32),
            jax.ShapeDtypeStruct((nsample, GLA_V), F32),
            jax.ShapeDtypeStruct((nsample, HG_V), F32),
        ],
        scratch_shapes=[
            pltpu.VMEM((tb, d), BF16),
            pltpu.VMEM((tb, tb), BF16),
            pltpu.VMEM((tb, LANES), BF16),
            pltpu.VMEM((tb, GLA_K), F32),
            pltpu.VMEM((tb, GLA_K), F32),
            pltpu.VMEM((GLA_H, tb, LANES), F32),
            pltpu.VMEM((tb, HG_K), F32),
            pltpu.VMEM((tb, HG_K), F32),
            pltpu.VMEM((HG_H, tb, LANES), F32),
            pltpu.VMEM((3, tb, HG_K), BF16),
            pltpu.VMEM((HG_H, tb, 2 * LANES), BF16),
            pltpu.VMEM((HG_H, tb, LANES), BF16),
            pltpu.VMEM((GLA_H * nch, 2 * LANES, GLA_DV), BF16),
            pltpu.VMEM((GLA_H * nch, GLA_DK, GLA_DV), F32),
            pltpu.VMEM((tb, GLA_V), F32),
            pltpu.VMEM((tb, HG_V), F32),
            pltpu.VMEM((tb, GLA_V), F32),
            pltpu.VMEM((tb, HG_V), F32),
            pltpu.VMEM((tb, d), F32),
            pltpu.VMEM((tb, d), F32),
            pltpu.VMEM((tb, d), BF16),
            pltpu.VMEM((tb, d), BF16),
            pltpu.VMEM((tb, d), F32),
        ],
        compiler_params=pltpu.CompilerParams(
            dimension_semantics=("arbitrary", "arbitrary"), vmem_limit_bytes=VMEM_LIMIT),
        name="prompt_layer",
    )(x, *sample_rows, *sample_states, *weights)


def _sample_proj_kernel(x_ref, win_ref, wga_ref, wglr_ref, bg_ref, lbp_ref,
                        eg_ref, kg_ref, qg_ref, vg_ref, eh_ref, kh_ref, qh_ref, vh_ref,
                        sgr_ref, shr_ref, mga_ref, mgb_ref):
    xb = x_ref[...].astype(BF16)
    proj = lambda base, width: _dot(xb, win_ref[:, base:base + width])
    qg_ref[...] = proj(C_GQ, GLA_K) * (GLA_DK ** -0.5)
    kg_ref[...] = proj(C_GK, GLA_K)
    vg_ref[...] = proj(C_GV, GLA_V)
    sgr_ref[...] = _silu(proj(C_GR, GLA_V))
    ga = _dot(xb, wga_ref[...]).astype(BF16)
    a_logit = _dot(ga, wglr_ref[...]) + bg_ref[...]
    eg_ref[...] = jnp.exp(_log_sigmoid(a_logit) * (1.0 / GLA_TAU))
    lb = _lower_bound(lbp_ref[...])
    qh_ref[...] = _silu(proj(C_HQ, HG_K))
    sf, sfn = _sigmoid_pair(proj(C_HF, HG_K))
    eh_ref[...] = jnp.exp(jnp.log(lb + (1.0 - lb) * sf))
    kh_ref[...] = (1.0 - lb) * sfn
    vh_ref[...] = proj(C_HI, HG_V)
    shr_ref[...] = _silu(proj(C_HR, HG_V))
    mga_ref[...] = _sigmoid(proj(C_MA, D_MODEL))
    mgb_ref[...] = _sigmoid(proj(C_MB, D_MODEL))


def _sample_proj_call(x, win, wga, wglr, bg, lbp):
    n = x.shape[0]
    widths = (GLA_K, GLA_K, GLA_K, GLA_V, HG_K, HG_K, HG_K, HG_V, GLA_V, HG_V, D_MODEL, D_MODEL)
    args = (x, win, wga, wglr, bg, lbp)
    return pl.pallas_call(
        _sample_proj_kernel,
        in_specs=[_whole(a.shape) for a in args],
        out_specs=[_whole((n, w)) for w in widths],
        out_shape=[jax.ShapeDtypeStruct((n, w), F32) for w in widths],
        compiler_params=pltpu.CompilerParams(vmem_limit_bytes=VMEM_LIMIT),
        name="sample_proj",
    )(*args)


def _sample_post_kernel(x_ref, oa_ref, ob_ref, sgr_ref, shr_ref, mga_ref, mgb_ref,
                        gng_ref, wbg_ref, hng_ref, wbh_ref, wout_ref, lng_ref, lnb_ref, y_ref):
    ga = (_head_rmsnorm(oa_ref[...], gng_ref[...], GLA_DV) * sgr_ref[...]).astype(BF16)
    gb = (_head_rmsnorm(ob_ref[...], hng_ref[...], HG_DV) * shr_ref[...]).astype(BF16)
    merged = mga_ref[...] * _dot(ga, wbg_ref[...]) + mgb_ref[...] * _dot(gb, wbh_ref[...])
    z = DEEPNORM_ALPHA * x_ref[...] + _dot(merged.astype(BF16), wout_ref[...])
    y_ref[...] = _layernorm(z, lng_ref[...], lnb_ref[...])


def _sample_post_call(*args):
    n = args[0].shape[0]
    return pl.pallas_call(
        _sample_post_kernel,
        in_specs=[_whole(a.shape) for a in args],
        out_specs=_whole((n, D_MODEL)),
        out_shape=jax.ShapeDtypeStruct((n, D_MODEL), F32),
        compiler_params=pltpu.CompilerParams(vmem_limit_bytes=VMEM_LIMIT),
        name="sample_post",
    )(*args)


def _weight_prep_kernel(wt_ref, gat_ref, win_ref, wga_ref):
    win_ref[...] = wt_ref[...].T.astype(BF16)

    @pl.when(pl.program_id(0) == 0)
    def _():
        d = gat_ref.shape[1]
        ga = jnp.concatenate([gat_ref[...], jnp.zeros((LANES - GLA_RANK, d), F32)], axis=0)
        wga_ref[...] = ga.T.astype(BF16)


def _weight_prep_call(w_in):
    _, d, width = w_in.shape
    wt = jnp.swapaxes(w_in, 1, 2)[0]
    blk = WEIGHT_PREP_COLS
    assert C_HQ % blk == 0 and (width - GLA_RANK) % blk == 0
    src_row = lambda i: (i * (blk // GLA_RANK) + (i >= C_HQ // blk).astype(jnp.int32)) * GLA_RANK
    return pl.pallas_call(
        _weight_prep_kernel,
        grid=((width - GLA_RANK) // blk,),
        in_specs=[pl.BlockSpec((pl.Element(blk), pl.Element(d)), lambda i: (src_row(i), 0)),
                  pl.BlockSpec((pl.Element(GLA_RANK), pl.Element(d)), lambda i: (C_HQ, 0))],
        out_specs=[pl.BlockSpec((d, blk), lambda i: (0, i)),
                   pl.BlockSpec((d, LANES), lambda i: (0, 0))],
        out_shape=[jax.ShapeDtypeStruct((d, width - GLA_RANK), BF16), jax.ShapeDtypeStruct((d, LANES), BF16)],
        compiler_params=pltpu.CompilerParams(dimension_semantics=("arbitrary",), vmem_limit_bytes=VMEM_LIMIT),
        name="weight_prep",
    )(wt, wt)


def kernel(x_prompt, x_sample, state_gla, state_hgrn, w_in, w_gate_lr, b_gate_lr, gla_norm_g, w_br_gla,
           hgrn_lb_param, hgrn_norm_g, w_br_hgrn, w_out, ln_g, ln_b):
    assert w_in.shape[0] == DEPTH and x_sample.shape[1] == 1
    win, wga = _weight_prep_call(w_in)
    wglr = jnp.pad(w_gate_lr[0], ((0, LANES - GLA_RANK), (0, 0))).astype(BF16)
    bg = b_gate_lr[0].reshape(1, GLA_K)
    gng = gla_norm_g[0].reshape(1, GLA_V)
    hng = hgrn_norm_g[0].reshape(1, HG_V)
    wbg = w_br_gla[0].astype(BF16)
    wbh = w_br_hgrn[0].astype(BF16)
    wout = w_out[0].astype(BF16)
    lng = ln_g[0].reshape(1, D_MODEL)
    lnb = ln_b[0].reshape(1, D_MODEL)
    lbp = hgrn_lb_param

    xs = x_sample[:, 0, :]
    *rows, sgr, shr, mga, mgb = _sample_proj_call(xs, win, wga, wglr, bg, lbp)

    y_prompt, gla_p, hgrn_p, gla_s, hgrn_s, oa, ob = _prompt_call(
        x_prompt, rows, (state_gla[0], state_hgrn[0]), win, wga, wglr, bg, gng, wbg, lbp, hng, wbh, wout, lng, lnb)

    ys = _sample_post_call(xs, oa, ob, sgr, shr, mga, mgb, gng, wbg, hng, wbh, wout, lng, lnb)
    return (y_prompt, ys[:, None, :], gla_p, hgrn_p, gla_s[None], hgrn_s[None])
```

```python
from typing import Any, NamedTuple

import jax
import jax.numpy as jnp
from jax import lax
from jax.experimental import pallas as pl
from jax.experimental.pallas import tpu as pltpu

F32 = jnp.float32
BF16 = jnp.bfloat16

D_MODEL = 1024
GLA_H, GLA_DK, GLA_DV = 4, 128, 256
HG_H, HG_DK, HG_DV = 8, 128, 128
GLA_RANK = 16
GLA_TAU = 16.0
LOG2_E = 1.4426950408889634
GLA_K = GLA_H * GLA_DK
GLA_V = GLA_H * GLA_DV
HG_K = HG_H * HG_DK
HG_V = HG_H * HG_DV
C_GQ, C_GK, C_GV, C_GR = 0, GLA_K, 2 * GLA_K, 2 * GLA_K + GLA_V
C_HQ = C_GR + GLA_V
C_HF, C_HI, C_HR = C_HQ + HG_K, C_HQ + 2 * HG_K, C_HQ + 2 * HG_K + HG_V
C_MA = C_HR + HG_V
C_MB = C_MA + D_MODEL
CHUNK = 64
SUB = 16
NORM_EPS = 1e-5
DEPTH = 1
DEEPNORM_ALPHA = (2.0 * DEPTH) ** 0.25
LANES = 128
SUBLANES = 8
BF16_SUBLANES = 16
TOK_BLOCK = 256
COL_TILE = 256
ROW_BLOCK = 64
LN_ROW_BLOCK = 16
WEIGHT_PREP_COLS = 1024
SAMPLE_PROJ_COLS = 1024
VMEM_LIMIT = 60 * 1024 * 1024


def _dot(a, b):
    return jnp.dot(a, b, preferred_element_type=F32)


def _dot_nt(a, b):
    return lax.dot_general(a, b, (((1,), (1,)), ((), ())), preferred_element_type=F32)


def _dot_tn(a, b):
    return lax.dot_general(a, b, (((0,), (0,)), ((), ())), preferred_element_type=F32)


def _sigmoid(x):
    return 1.0 / (1.0 + jnp.exp(-x))


def _sigmoid_pair(x):
    e = jnp.exp(-jnp.abs(x))
    s = 1.0 / (1.0 + e)
    t = e * s
    pos = x >= 0.0
    return jnp.where(pos, s, t), jnp.where(pos, t, s)


def _silu(x):
    return x * _sigmoid(x)


def _log_sigmoid(x):
    return jnp.minimum(x, 0.0) - jnp.log1p(jnp.exp(-jnp.abs(x)))


def _split3(x):
    hi = x.astype(BF16)
    r = x - hi.astype(F32)
    mid = r.astype(BF16)
    lo = (r - mid.astype(F32)).astype(BF16)
    return hi, mid, lo


def _block_tri(n):
    r = lax.broadcasted_iota(jnp.int32, (n, n), 0)
    c = lax.broadcasted_iota(jnp.int32, (n, n), 1)
    shift = CHUNK.bit_length() - 1
    keep = (c <= r) & (jnp.right_shift(r, shift) == jnp.right_shift(c, shift))
    return jnp.where(keep, 1.0, 0.0).astype(BF16)


def _lower_bound(lbp):
    rows = [lbp[i:i + 1, :] for i in range(lbp.shape[0])]
    m = rows[0]
    for r in rows[1:]:
        m = jnp.maximum(m, r)
    es = [jnp.exp(r - m) for r in rows]
    tot = es[0]
    for e in es[1:]:
        tot = tot + e
    return es[0] / tot


def _score_mask():
    t = lax.broadcasted_iota(jnp.int32, (CHUNK, LANES), 0)
    lane = lax.broadcasted_iota(jnp.int32, (CHUNK, LANES), 1)
    s = lane - CHUNK * (jnp.right_shift(t, SUB.bit_length() - 1) & 1)
    return (s >= 0) & (s <= t)


def _intra_scores(q, k, b, keep):
    dk = q.shape[1]
    nsub = CHUNK // SUB
    qs, ks = [], []
    for i in range(nsub):
        lo, hi = i * SUB, (i + 1) * SUB
        bs = b[lo:lo + 1, :]
        qs.append(q[lo:hi] * jnp.exp2(b[lo:hi] - bs))
        ks.append((k[0:hi] * jnp.exp2(bs - b[0:hi])).astype(BF16))
        if hi < CHUNK:
            ks.append(jnp.zeros((CHUNK - hi, dk), BF16))
    prod = _dot_nt(jnp.concatenate(qs, axis=0).astype(BF16), jnp.concatenate(ks, axis=0))
    per = LANES // CHUNK
    a = jnp.concatenate([prod[i * SUB:(i + 1) * SUB, (i // per) * LANES:(i // per + 1) * LANES]
                         for i in range(nsub)], axis=0)
    return jnp.where(keep, a, 0.0)


def _columns(rows):
    r = rows.shape[0]
    if r < LANES:
        rows = jnp.concatenate([rows, jnp.zeros((LANES - r, LANES), F32)], axis=0)
    return rows.T


def _head_rmsnorm(o, gain, dv):
    outs = []
    for j in range(o.shape[1] // dv):
        oj = o[:, j * dv:(j + 1) * dv]
        ms = jnp.mean(oj * oj, axis=-1, keepdims=True)
        outs.append(oj * lax.rsqrt(ms + NORM_EPS))
    on = outs[0] if len(outs) == 1 else jnp.concatenate(outs, axis=1)
    return on * gain


def _layernorm(z, g, b):
    mu = jnp.mean(z, axis=-1, keepdims=True)
    zc = z - mu
    var = jnp.mean(zc * zc, axis=-1, keepdims=True)
    return zc * lax.rsqrt(var + NORM_EPS) * g + b


class _Task(NamedTuple):
    cost: float
    run: Any


def _merge(*lists):
    totals = [sum(t.cost for t in l) or 1.0 for l in lists]
    pos = [0] * len(lists)
    done = [0.0] * len(lists)
    out = []
    for _ in range(sum(len(l) for l in lists)):
        live = [i for i in range(len(lists)) if pos[i] < len(lists[i])]
        j = min(live, key=lambda i: (done[i] + 0.5 * lists[i][pos[i]].cost) / totals[i])
        out.append(lists[j][pos[j]])
        done[j] += lists[j][pos[j]].cost
        pos[j] += 1
    return out


def _lagged(groups, lag=1):
    out = []
    for i, (mms, _) in enumerate(groups):
        out += mms
        if i >= lag:
            out += groups[i - lag][1]
    for _, ews in groups[max(len(groups) - lag, 0):]:
        out += ews
    return out


def _run(tasks):
    for t in tasks:
        t.run()


class _Branch(NamedTuple):
    heads: int
    dk: int
    dv: int
    scale: float
    q: Any
    k: Any
    b: Any
    o: Any
    state: Any


def _slab(buf, hc, dv):
    per = buf.shape[2] // dv
    return hc // per, slice((hc % per) * dv, (hc % per + 1) * dv)


def _store_decay(b_s, n, bt):
    per = COL_TILE // LANES
    for j in range(per):
        b_s[n * per + j] = bt[:, j * LANES:(j + 1) * LANES]


def _store_values(w_s, n, vt, dk, dv, nchunks):
    per = COL_TILE // dv
    for j in range(per):
        for ci in range(nchunks):
            i, cs = _slab(w_s, (n * per + j) * nchunks + ci, dv)
            v = vt[ci * CHUNK:(ci + 1) * CHUNK, j * dv:(j + 1) * dv]
            w_s[i, dk:dk + CHUNK, cs] = v
            w_s[i, dk + CHUNK:dk + 2 * CHUNK, cs] = v


def _recurrence_tasks(br, qa_s, kd_s, w_s, u_s, nchunks):
    nheads, dk, dv = br.heads, br.dk, br.dv
    keep = _score_mask()
    pairs = [(ci, h) for ci in range(nchunks) for h in range(nheads)]
    rows_of = lambda ci: slice(ci * CHUNK, (ci + 1) * CHUNK)
    cols = []

    def operands(ci, h):
        rows, kcols = rows_of(ci), slice(h * dk, (h + 1) * dk)
        q, k, b = br.q[rows, kcols], br.k[rows, kcols], br.b[h, rows, :]
        if br.scale != 1.0:
            q = q * br.scale
        qa_s[h, rows, 0:dk] = (q * jnp.exp2(b)).astype(BF16)
        kd_s[h, rows, :] = (k * jnp.exp2(b[CHUNK - 1:CHUNK, :] - b)).astype(BF16)
        qa_s[h, rows, dk:dk + LANES] = _intra_scores(q, k, b, keep).astype(BF16)

    def outer(ci, h):
        i, cs = _slab(w_s, h * nchunks + ci, dv)
        u_s[i, :, cs] = _dot_tn(kd_s[h, rows_of(ci), :], w_s[i, dk:dk + CHUNK, cs])

    def decay_columns():
        half = CHUNK // 2
        for h in range(nheads):
            cols.append(_columns(jnp.exp2(br.b[h, pl.ds(half - 1, 2 * nchunks, stride=half), :])))

    def chain(ci, h):
        i, cs = _slab(w_s, h * nchunks + ci, dv)
        s = br.state[0, 0, h]
        w_s[i, 0:dk, cs] = s.astype(BF16)
        br.state[0, 0, h] = s * cols[h][:, 2 * ci + 1:2 * ci + 2] + u_s[i, :, cs]

    def output(ci, h):
        i, cs = _slab(w_s, h * nchunks + ci, dv)
        br.o[rows_of(ci), h * dv:(h + 1) * dv] = _dot(qa_s[h, rows_of(ci), :], w_s[i, :, cs])

    bind = lambda cost, fn: [_Task(cost, lambda ci=ci, h=h: fn(ci, h)) for ci, h in pairs]
    return bind(70, operands), (bind(32, outer) + [_Task(16 * nheads, decay_columns)]
                                + bind(8 + dv // 8, chain) + bind(16, output))


def _sample_state_tasks(row0, e_ref, k_ref, q_ref, v_ref, s_ref, snew_ref, o_ref, nheads, dk, dv):
    nseq = s_ref.shape[0]
    first = lax.broadcasted_iota(jnp.int32, (BF16_SUBLANES, dk), 0) == 0
    ks = lambda h: slice(h * dk, (h + 1) * dk)
    vs = lambda h: slice(h * dv, (h + 1) * dv)
    mine = lambda j, width: lax.broadcasted_iota(jnp.int32, (SUBLANES, width), 0) == row0 + j

    def vec(ref, j, cols):
        return jnp.sum(jnp.where(mine(j, cols.stop - cols.start), ref[:, cols], 0.0), axis=0, keepdims=True)

    tile = lambda ref, j, cols: jnp.broadcast_to(vec(ref, j, cols), (BF16_SUBLANES, cols.stop - cols.start))
    pairs = [(h, j) for h in range(nheads) for j in range(nseq)]

    def outer(h, j):
        kj = jnp.where(first, tile(k_ref, j, ks(h)), 0.0).astype(BF16)
        snew_ref[j, h] = _dot_tn(kj, tile(v_ref, j, vs(h)).astype(BF16))

    def update(h, j):
        ecol = jnp.broadcast_to(vec(e_ref, j, ks(h)), (dk, LANES)).T
        ecol = ecol if dv == LANES else jnp.concatenate([ecol] * (dv // LANES), axis=1)
        snew_ref[j, h] = s_ref[j, h] * ecol + snew_ref[j, h]

    def readout(h, j):
        o = _dot(tile(q_ref, j, ks(h)).astype(BF16), snew_ref[j, h].astype(BF16))
        o_ref[:, vs(h)] = jnp.where(mine(j, dv), o[0:SUBLANES, :], o_ref[:, vs(h)])

    bind = lambda cost, fn: [_Task(cost, lambda h=h, j=j: fn(h, j)) for h, j in pairs]
    return bind(40.0, outer), bind(30.0, update), bind(20.0, readout)


def _prompt_kernel(x_ref, seg_ref, skg_ref, sqg_ref, svg_ref, seh_ref, skh_ref, sqh_ref, svh_ref, ssg_ref, ssh_ref,
                   win_ref, wga_ref, wglr_ref, bg_ref, gng_ref, wbg_ref,
                   lbp_ref, hng_ref, wbh_ref, wout_ref, lng_ref, lnb_ref,
                   y_ref, sg_ref, sh_ref, ssg_new_ref, ssh_new_ref, soa_ref, sob_ref,
                   xb_s, tri_s, ga_s, qg_s, kg_s, bgl_s, qh_s, kh_s, bh_s, h3_s, qa_s, kd_s, w_s, u_s,
                   og_s, oh_s, sgr_s, shr_s, mga_s, mgb_s, g_s, m_s, p_s):
    tb = TOK_BLOCK
    nch = tb // CHUNK
    tiles = lambda width: range(width // COL_TILE)
    col = lambda n, base=0: slice(base + n * COL_TILE, base + (n + 1) * COL_TILE)
    gla = _Branch(GLA_H, GLA_DK, GLA_DV, GLA_DK ** -0.5, qg_s, kg_s, bgl_s, og_s, sg_ref)
    hgrn = _Branch(HG_H, HG_DK, HG_DV, 1.0, qh_s, kh_s, bh_s, oh_s, sh_ref)
    mm_cost = 256.0

    @pl.when(pl.program_id(1) == 0)
    def _():
        sg_ref[...] = jnp.zeros_like(sg_ref)
        sh_ref[...] = jnp.zeros_like(sh_ref)
        tri_s[...] = _block_tri(tb)

    nseq = ssg_ref.shape[0]
    lin = pl.program_id(0) * pl.num_programs(1) + pl.program_id(1)
    row0 = (lin % (SUBLANES // nseq)) * nseq

    @pl.when(row0 == 0)
    def _():
        soa_ref[...] = jnp.zeros_like(soa_ref)
        sob_ref[...] = jnp.zeros_like(sob_ref)

    xb_s[...] = x_ref[0].astype(BF16)
    lb = _lower_bound(lbp_ref[...])

    def mm(dst, w_ref, wcols, lhs=xb_s, cast=None, cost=mm_cost):
        def run():
            r = _dot(lhs[...], w_ref[:, wcols])
            dst(r if cast is None else r.astype(cast))
        return _Task(cost, run)

    def store(ref, cols):
        def put(v):
            ref[:, cols] = v
        return put

    def ew(cost, fn, rows=(0, tb), step=ROW_BLOCK):
        def run():
            for r in range(rows[0], rows[1], step):
                fn(slice(r, r + step))
        return _Task(cost, run)

    def cumsum_task(b_s, n):
        def run():
            t = tri_s[...]
            _store_decay(b_s, n, _dot(t, h3_s[0, :, col(n)]) + _dot(t, h3_s[1, :, col(n)]) + _dot(t, h3_s[2, :, col(n)]))
        return _Task(192.0, run)

    def put_split3(rb, cs, g):
        hi, mid, lo = _split3(g)
        h3_s[0, rb, cs] = hi
        h3_s[1, rb, cs] = mid
        h3_s[2, rb, cs] = lo

    def gla_logit(n):
        def run():
            qg_s[:, col(n)] = _dot(ga_s[...], wglr_ref[:, col(n)]) + bg_ref[:, col(n)]
        return _Task(64.0, run)

    def gla_log(n):
        def fn(rb):
            put_split3(rb, col(n), _log_sigmoid(qg_s[rb, col(n)]) * (LOG2_E / GLA_TAU))
        return ew(300.0, fn)

    gla_in = [mm(store(ga_s, slice(None)), wga_ref, slice(None), cast=BF16, cost=128.0)]
    gla_in += [mm(store(kg_s, col(n)), win_ref, col(n, C_GK)) for n in tiles(GLA_K)]
    gla_in += [gla_logit(n) for n in tiles(GLA_K)]
    gla_in += [t for n in tiles(GLA_K) for t in (
        mm(lambda v, n=n: _store_values(w_s, n, v, GLA_DK, GLA_DV, nch), win_ref, col(n, C_GV), cast=BF16),
        gla_log(n),
        mm(lambda v, n=n: _store_values(w_s, n + GLA_K // COL_TILE, v, GLA_DK, GLA_DV, nch), win_ref,
           col(n + GLA_K // COL_TILE, C_GV), cast=BF16))]
    gla_in += [cumsum_task(bgl_s, n) for n in tiles(GLA_K)]
    gla_in += [mm(store(qg_s, col(n)), win_ref, col(n, C_GQ)) for n in tiles(GLA_K)]
    _run(gla_in)

    def hq_ew(n):
        def fn(rb):
            qh_s[rb, col(n)] = _silu(qh_s[rb, col(n)])
        return ew(130.0, fn)

    def hf_ew(n):
        def fn(rb):
            sf, sfn = _sigmoid_pair(kh_s[rb, col(n)])
            lbn = lb[:, col(n)]
            kh_s[rb, col(n)] = (1.0 - lbn) * sfn
            put_split3(rb, col(n), jnp.log2(lbn + (1.0 - lbn) * sf))
        return ew(450.0, fn)

    def gate_ew(ref, act, n):
        def fn(rb):
            ref[rb, col(n)] = act(ref[rb, col(n)])
        return ew(130.0, fn)

    hg_in = [([mm(store(kh_s, col(n)), win_ref, col(n, C_HF)),
               mm(store(qh_s, col(n)), win_ref, col(n, C_HQ))],
              [hf_ew(n), hq_ew(n), cumsum_task(bh_s, n)]) for n in tiles(HG_K)]
    gates = [([mm(store(sgr_s, col(n)), win_ref, col(n, C_GR))], [gate_ew(sgr_s, _silu, n)]) for n in tiles(GLA_V)]
    gates += [([mm(store(mga_s, col(n)), win_ref, col(n, C_MA))], [gate_ew(mga_s, _sigmoid, n)])
              for n in tiles(D_MODEL)]
    gates_b = [([mm(store(shr_s, col(n)), win_ref, col(n, C_HR))], [gate_ew(shr_s, _silu, n)]) for n in tiles(HG_V)]
    hg_v = [mm(lambda v, n=n: _store_values(w_s, n, v, HG_DK, HG_DV, nch), win_ref, col(n, C_HI), cast=BF16)
            for n in tiles(HG_V)]
    mgb = [([mm(store(mgb_s, col(n)), win_ref, col(n, C_MB))], [gate_ew(mgb_s, _sigmoid, n)]) for n in tiles(D_MODEL)]

    ra_ops, ra_rest = _recurrence_tasks(gla, qa_s, kd_s, w_s, u_s, nch)
    s_outer, s_update, s_readout = [], [], []
    for args in ((seg_ref, skg_ref, sqg_ref, svg_ref, ssg_ref, ssg_new_ref, soa_ref, GLA_H, GLA_DK, GLA_DV),
                 (seh_ref, skh_ref, sqh_ref, svh_ref, ssh_ref, ssh_new_ref, sob_ref, HG_H, HG_DK, HG_DV)):
        outer, update, readout = _sample_state_tasks(row0, *args)
        s_outer += outer
        s_update += update
        s_readout += readout

    _run(_merge(ra_ops + ra_rest, _lagged(hg_in + gates), s_outer))

    def gla_norm(n):
        def fn(rb):
            g_s[rb, col(n)] = (_head_rmsnorm(og_s[rb, col(n)], gng_ref[:, col(n)], GLA_DV) * sgr_s[rb, col(n)]).astype(BF16)
        return ew(200.0, fn)

    def gla_scale(n):
        def fn(rb):
            p_s[rb, col(n)] = p_s[rb, col(n)] * mga_s[rb, col(n)]
        return ew(50.0, fn)

    gla_post = [gla_norm(n) for n in tiles(GLA_V)]
    for n in tiles(D_MODEL):
        gla_post += [mm(store(p_s, col(n)), wbg_ref, col(n), lhs=g_s), gla_scale(n)]
    rb_ops, rb_rest = _recurrence_tasks(hgrn, qa_s, kd_s, w_s, u_s, nch)
    _run(_merge(rb_ops, hg_v + _lagged(gates_b)))
    _run(_merge(rb_rest, gla_post, s_update))

    def hgrn_norm(n):
        def fn(rb):
            m_s[rb, col(n)] = (_head_rmsnorm(oh_s[rb, col(n)], hng_ref[:, col(n)], HG_DV) * shr_s[rb, col(n)]).astype(BF16)
        return ew(220.0, fn)

    def merge_ew(n):
        def fn(rb):
            g_s[rb, col(n)] = (p_s[rb, col(n)] + mgb_s[rb, col(n)] * sgr_s[rb, col(n)]).astype(BF16)
        return ew(60.0, fn)

    def final_ew(rows):
        def fn(rb):
            z = DEEPNORM_ALPHA * x_ref[0, rb, :] + shr_s[rb, :]
            y_ref[0, rb, :] = _layernorm(z, lng_ref[...], lnb_ref[...])
        return ew(180.0, fn, rows=rows, step=LN_ROW_BLOCK)

    _run(_merge([hgrn_norm(n) for n in tiles(HG_V)], _lagged(mgb)))
    _run([mm(store(sgr_s, col(n)), wbh_ref, col(n), lhs=m_s) for n in tiles(D_MODEL)])
    _run([merge_ew(n) for n in tiles(D_MODEL)])
    tail = [mm(store(shr_s, col(n)), wout_ref, col(n), lhs=g_s) for n in tiles(D_MODEL)]
    tail += [final_ew((r, r + ROW_BLOCK)) for r in range(0, tb, ROW_BLOCK)]
    _run(_merge(tail, s_readout))


def _whole(shape):
    return pl.BlockSpec(memory_space=pltpu.VMEM)


def _prompt_call(x, sample_rows, sample_states, win, wga, wglr, bg, gng, wbg, lbp, hng, wbh, wout, lng, lnb):
    bsz, seq, d = x.shape
    tb = TOK_BLOCK
    nch = tb // CHUNK
    per_row = seq // tb
    steps = bsz * per_row
    nsample = sample_states[0].shape[0]
    nseq = nsample // steps
    assert nseq * steps == nsample and SUBLANES % nseq == 0 and all(r.shape[0] == nsample for r in sample_rows)
    weights = (win, wga, wglr, bg, gng, wbg, lbp, hng, wbh, wout, lng, lnb)
    lin = lambda b, t: b * per_row + t
    row_block = lambda b, t: (lin(b, t) // (SUBLANES // nseq), 0)
    row_spec = lambda r: pl.BlockSpec((SUBLANES, r.shape[1]), row_block)
    state_spec = lambda s: pl.BlockSpec((nseq,) + s.shape[1:], lambda b, t: (lin(b, t), 0, 0, 0))
    return pl.pallas_call(
        _prompt_kernel,
        grid=(bsz, per_row),
        in_specs=[pl.BlockSpec((1, tb, d), lambda b, t: (b, t, 0))] + [row_spec(r) for r in sample_rows]
        + [state_spec(s) for s in sample_states] + [_whole(w.shape) for w in weights],
        out_specs=[
            pl.BlockSpec((1, tb, d), lambda b, t: (b, t, 0)),
            pl.BlockSpec((1, 1, GLA_H, GLA_DK, GLA_DV), lambda b, t: (0, b, 0, 0, 0)),
            pl.BlockSpec((1, 1, HG_H, HG_DK, HG_DV), lambda b, t: (0, b, 0, 0, 0)),
            state_spec(sample_states[0]), state_spec(sample_states[1]),
            pl.BlockSpec((SUBLANES, GLA_V), row_block),
            pl.BlockSpec((SUBLANES, HG_V), row_block),
        ],
        out_shape=[
            jax.ShapeDtypeStruct((bsz, seq, d), F32),
            jax.ShapeDtypeStruct((DEPTH, bsz, GLA_H, GLA_DK, GLA_DV), F32),
            jax.ShapeDtypeStruct((DEPTH, bsz, HG_H, HG_DK, HG_DV), F32),
            jax.ShapeDtypeStruct(sample_states[0].shape, F32),
            jax.ShapeDtypeStruct(sample_states[1].shape, F32),
            jax.ShapeDtypeStruct((nsample, GLA_V), F32),
            jax.ShapeDtypeStruct((nsample, HG_V), F32),
        ],
        scratch_shapes=[
            pltpu.VMEM((tb, d), BF16),
            pltpu.VMEM((tb, tb), BF16),
            pltpu.VMEM((tb, LANES), BF16),
            pltpu.VMEM((tb, GLA_K), F32),
            pltpu.VMEM((tb, GLA_K), F32),
            pltpu.VMEM((GLA_H, tb, LANES), F32),
            pltpu.VMEM((tb, HG_K), F32),
            pltpu.VMEM((tb, HG_K), F32),
            pltpu.VMEM((HG_H, tb, LANES), F32),
            pltpu.VMEM((3, tb, HG_K), BF16),
            pltpu.VMEM((HG_H, tb, 2 * LANES), BF16),
            pltpu.VMEM((HG_H, tb, LANES), BF16),
            pltpu.VMEM((GLA_H * nch, 2 * LANES, GLA_DV), BF16),
            pltpu.VMEM((GLA_H * nch, GLA_DK, GLA_DV), F32),
            pltpu.VMEM((tb, GLA_V), F32),
            pltpu.VMEM((tb, HG_V), F32),
            pltpu.VMEM((tb, GLA_V), F32),
            pltpu.VMEM((tb, HG_V), F32),
            pltpu.VMEM((tb, d), F32),
            pltpu.VMEM((tb, d), F32),
            pltpu.VMEM((tb, d), BF16),
            pltpu.VMEM((tb, d), BF16),
            pltpu.VMEM((tb, d), F32),
        ],
        compiler_params=pltpu.CompilerParams(
            dimension_semantics=("arbitrary", "arbitrary"), vmem_limit_bytes=VMEM_LIMIT),
        name="prompt_layer",
    )(x, *sample_rows, *sample_states, *weights)


def _sample_proj_kernel(x_ref, win_ref, wga_ref, wglr_ref, bg_ref, lbp_ref,
                        eg_ref, kg_ref, qg_ref, vg_ref, eh_ref, kh_ref, qh_ref, vh_ref,
                        sgr_ref, shr_ref, mga_ref, mgb_ref):
    xb = x_ref[...].astype(BF16)
    h = _dot(xb, win_ref[...])
    blk = SAMPLE_PROJ_COLS
    at = lambda base: pl.when(pl.program_id(0) == base // blk)
    assert GLA_V == HG_K == HG_V == D_MODEL == blk and 2 * GLA_K == blk

    @at(C_GQ)
    def _():
        qg_ref[...] = h[:, 0:GLA_K] * (GLA_DK ** -0.5)
        kg_ref[...] = h[:, GLA_K:]
        ga = _dot(xb, wga_ref[...]).astype(BF16)
        a_logit = _dot(ga, wglr_ref[...]) + bg_ref[...]
        eg_ref[...] = jnp.exp(_log_sigmoid(a_logit) * (1.0 / GLA_TAU))

    @at(C_GV)
    def _():
        vg_ref[...] = h

    @at(C_GR)
    def _():
        sgr_ref[...] = _silu(h)

    @at(C_HQ)
    def _():
        qh_ref[...] = _silu(h)

    @at(C_HF)
    def _():
        lb = _lower_bound(lbp_ref[...])
        sf, sfn = _sigmoid_pair(h)
        eh_ref[...] = jnp.exp(jnp.log(lb + (1.0 - lb) * sf))
        kh_ref[...] = (1.0 - lb) * sfn

    @at(C_HI)
    def _():
        vh_ref[...] = h

    @at(C_HR)
    def _():
        shr_ref[...] = _silu(h)

    @at(C_MA)
    def _():
        mga_ref[...] = _sigmoid(h)

    @at(C_MB)
    def _():
        mgb_ref[...] = _sigmoid(h)


def _sample_proj_call(x, win, wga, wglr, bg, lbp):
    n = x.shape[0]
    widths = (GLA_K, GLA_K, GLA_K, GLA_V, HG_K, HG_K, HG_K, HG_V, GLA_V, HG_V, D_MODEL, D_MODEL)
    small = (wga, wglr, bg, lbp)
    blk = SAMPLE_PROJ_COLS
    return pl.pallas_call(
        _sample_proj_kernel,
        grid=(win.shape[1] // blk,),
        in_specs=[_whole(x.shape), pl.BlockSpec((win.shape[0], blk), lambda i: (0, i))]
        + [_whole(a.shape) for a in small],
        out_specs=[_whole((n, w)) for w in widths],
        out_shape=[jax.ShapeDtypeStruct((n, w), F32) for w in widths],
        compiler_params=pltpu.CompilerParams(dimension_semantics=("arbitrary",), vmem_limit_bytes=VMEM_LIMIT),
        name="sample_proj",
    )(x, win, *small)


def _sample_post_kernel(x_ref, oa_ref, ob_ref, sgr_ref, shr_ref, mga_ref, mgb_ref,
                        gng_ref, wbg_ref, hng_ref, wbh_ref, wout_ref, lng_ref, lnb_ref, y_ref):
    ga = (_head_rmsnorm(oa_ref[...], gng_ref[...], GLA_DV) * sgr_ref[...]).astype(BF16)
    gb = (_head_rmsnorm(ob_ref[...], hng_ref[...], HG_DV) * shr_ref[...]).astype(BF16)
    merged = mga_ref[...] * _dot(ga, wbg_ref[...]) + mgb_ref[...] * _dot(gb, wbh_ref[...])
    z = DEEPNORM_ALPHA * x_ref[...] + _dot(merged.astype(BF16), wout_ref[...])
    y_ref[...] = _layernorm(z, lng_ref[...], lnb_ref[...])


def _sample_post_call(*args):
    n = args[0].shape[0]
    return pl.pallas_call(
        _sample_post_kernel,
        in_specs=[_whole(a.shape) for a in args],
        out_specs=_whole((n, D_MODEL)),
        out_shape=jax.ShapeDtypeStruct((n, D_MODEL), F32),
        compiler_params=pltpu.CompilerParams(vmem_limit_bytes=VMEM_LIMIT),
        name="sample_post",
    )(*args)


def _weight_prep_kernel(wt_ref, gat_ref, win_ref, wga_ref):
    win_ref[...] = wt_ref[...].T.astype(BF16)

    @pl.when(pl.program_id(0) == 0)
    def _():
        d = gat_ref.shape[1]
        ga = jnp.concatenate([gat_ref[...], jnp.zeros((LANES - GLA_RANK, d), F32)], axis=0)
        wga_ref[...] = ga.T.astype(BF16)


def _weight_prep_call(w_in):
    _, d, width = w_in.shape
    wt = jnp.swapaxes(w_in, 1, 2)[0]
    blk = WEIGHT_PREP_COLS
    assert C_HQ % blk == 0 and (width - GLA_RANK) % blk == 0
    src_row = lambda i: (i * (blk // GLA_RANK) + (i >= C_HQ // blk).astype(jnp.int32)) * GLA_RANK
    return pl.pallas_call(
        _weight_prep_kernel,
        grid=((width - GLA_RANK) // blk,),
        in_specs=[pl.BlockSpec((pl.Element(blk), pl.Element(d)), lambda i: (src_row(i), 0)),
                  pl.BlockSpec((pl.Element(GLA_RANK), pl.Element(d)), lambda i: (C_HQ, 0))],
        out_specs=[pl.BlockSpec((d, blk), lambda i: (0, i)),
                   pl.BlockSpec((d, LANES), lambda i: (0, 0))],
        out_shape=[jax.ShapeDtypeStruct((d, width - GLA_RANK), BF16), jax.ShapeDtypeStruct((d, LANES), BF16)],
        compiler_params=pltpu.CompilerParams(dimension_semantics=("arbitrary",), vmem_limit_bytes=VMEM_LIMIT),
        name="weight_prep",
    )(wt, wt)


def kernel(x_prompt, x_sample, state_gla, state_hgrn, w_in, w_gate_lr, b_gate_lr, gla_norm_g, w_br_gla,
           hgrn_lb_param, hgrn_norm_g, w_br_hgrn, w_out, ln_g, ln_b):
    assert w_in.shape[0] == DEPTH and x_sample.shape[1] == 1
    win, wga = _weight_prep_call(w_in)
    wglr = jnp.pad(w_gate_lr[0], ((0, LANES - GLA_RANK), (0, 0))).astype(BF16)
    bg = b_gate_lr[0].reshape(1, GLA_K)
    gng = gla_norm_g[0].reshape(1, GLA_V)
    hng = hgrn_norm_g[0].reshape(1, HG_V)
    wbg = w_br_gla[0].astype(BF16)
    wbh = w_br_hgrn[0].astype(BF16)
    wout = w_out[0].astype(BF16)
    lng = ln_g[0].reshape(1, D_MODEL)
    lnb = ln_b[0].reshape(1, D_MODEL)
    lbp = hgrn_lb_param

    xs = x_sample[:, 0, :]
    *rows, sgr, shr, mga, mgb = _sample_proj_call(xs, win, wga, wglr, bg, lbp)

    y_prompt, gla_p, hgrn_p, gla_s, hgrn_s, oa, ob = _prompt_call(
        x_prompt, rows, (state_gla[0], state_hgrn[0]), win, wga, wglr, bg, gng, wbg, lbp, hng, wbh, wout, lng, lnb)

    ys = _sample_post_call(xs, oa, ob, sgr, shr, mga, mgb, gng, wbg, hng, wbh, wout, lng, lnb)
    return (y_prompt, ys[:, None, :], gla_p, hgrn_p, gla_s[None], hgrn_s[None])
```

```python
from typing import Any, NamedTuple

import jax
import jax.numpy as jnp
from jax import lax
from jax.experimental import pallas as pl
from jax.experimental.pallas import tpu as pltpu

F32 = jnp.float32
BF16 = jnp.bfloat16

D_MODEL = 1024
GLA_H, GLA_DK, GLA_DV = 4, 128, 256
HG_H, HG_DK, HG_DV = 8, 128, 128
GLA_RANK = 16
GLA_TAU = 16.0
LOG2_E = 1.4426950408889634
GLA_K = GLA_H * GLA_DK
GLA_V = GLA_H * GLA_DV
HG_K = HG_H * HG_DK
HG_V = HG_H * HG_DV
C_GQ, C_GK, C_GV, C_GR = 0, GLA_K, 2 * GLA_K, 2 * GLA_K + GLA_V
C_HQ = C_GR + GLA_V
C_HF, C_HI, C_HR = C_HQ + HG_K, C_HQ + 2 * HG_K, C_HQ + 2 * HG_K + HG_V
C_MA = C_HR + HG_V
C_MB = C_MA + D_MODEL
CHUNK = 64
SUB = 16
NORM_EPS = 1e-5
DEPTH = 1
DEEPNORM_ALPHA = (2.0 * DEPTH) ** 0.25
LANES = 128
SUBLANES = 8
BF16_SUBLANES = 16
TOK_BLOCK = 256
COL_TILE = 256
ROW_BLOCK = 64
LN_ROW_BLOCK = 16
WEIGHT_PREP_COLS = 1024
VMEM_LIMIT = 60 * 1024 * 1024


def _dot(a, b):
    return jnp.dot(a, b, preferred_element_type=F32)


def _dot_nt(a, b):
    return lax.dot_general(a, b, (((1,), (1,)), ((), ())), preferred_element_type=F32)


def _dot_tn(a, b):
    return lax.dot_general(a, b, (((0,), (0,)), ((), ())), preferred_element_type=F32)


def _sigmoid(x):
    return 1.0 / (1.0 + jnp.exp(-x))


def _sigmoid_pair(x):
    e = jnp.exp(-jnp.abs(x))
    s = 1.0 / (1.0 + e)
    t = e * s
    pos = x >= 0.0
    return jnp.where(pos, s, t), jnp.where(pos, t, s)


def _silu(x):
    return x * _sigmoid(x)


def _log_sigmoid(x):
    return jnp.minimum(x, 0.0) - jnp.log1p(jnp.exp(-jnp.abs(x)))


def _split3(x):
    hi = x.astype(BF16)
    r = x - hi.astype(F32)
    mid = r.astype(BF16)
    lo = (r - mid.astype(F32)).astype(BF16)
    return hi, mid, lo


def _block_tri(n):
    r = lax.broadcasted_iota(jnp.int32, (n, n), 0)
    c = lax.broadcasted_iota(jnp.int32, (n, n), 1)
    shift = CHUNK.bit_length() - 1
    keep = (c <= r) & (jnp.right_shift(r, shift) == jnp.right_shift(c, shift))
    return jnp.where(keep, 1.0, 0.0).astype(BF16)


def _lower_bound(lbp):
    rows = [lbp[i:i + 1, :] for i in range(lbp.shape[0])]
    m = rows[0]
    for r in rows[1:]:
        m = jnp.maximum(m, r)
    es = [jnp.exp(r - m) for r in rows]
    tot = es[0]
    for e in es[1:]:
        tot = tot + e
    return es[0] / tot


def _score_mask():
    t = lax.broadcasted_iota(jnp.int32, (CHUNK, LANES), 0)
    lane = lax.broadcasted_iota(jnp.int32, (CHUNK, LANES), 1)
    s = lane - CHUNK * (jnp.right_shift(t, SUB.bit_length() - 1) & 1)
    return (s >= 0) & (s <= t)


def _intra_scores(q, k, b, keep):
    dk = q.shape[1]
    nsub = CHUNK // SUB
    qs, ks = [], []
    for i in range(nsub):
        lo, hi = i * SUB, (i + 1) * SUB
        bs = b[lo:lo + 1, :]
        qs.append(q[lo:hi] * jnp.exp2(b[lo:hi] - bs))
        ks.append((k[0:hi] * jnp.exp2(bs - b[0:hi])).astype(BF16))
        if hi < CHUNK:
            ks.append(jnp.zeros((CHUNK - hi, dk), BF16))
    prod = _dot_nt(jnp.concatenate(qs, axis=0).astype(BF16), jnp.concatenate(ks, axis=0))
    per = LANES // CHUNK
    a = jnp.concatenate([prod[i * SUB:(i + 1) * SUB, (i // per) * LANES:(i // per + 1) * LANES]
                         for i in range(nsub)], axis=0)
    return jnp.where(keep, a, 0.0)


def _columns(rows):
    r = rows.shape[0]
    if r < LANES:
        rows = jnp.concatenate([rows, jnp.zeros((LANES - r, LANES), F32)], axis=0)
    return rows.T


def _head_rmsnorm(o, gain, dv):
    outs = []
    for j in range(o.shape[1] // dv):
        oj = o[:, j * dv:(j + 1) * dv]
        ms = jnp.mean(oj * oj, axis=-1, keepdims=True)
        outs.append(oj * lax.rsqrt(ms + NORM_EPS))
    on = outs[0] if len(outs) == 1 else jnp.concatenate(outs, axis=1)
    return on * gain


def _layernorm(z, g, b):
    mu = jnp.mean(z, axis=-1, keepdims=True)
    zc = z - mu
    var = jnp.mean(zc * zc, axis=-1, keepdims=True)
    return zc * lax.rsqrt(var + NORM_EPS) * g + b


class _Task(NamedTuple):
    cost: float
    run: Any


def _merge(*lists):
    totals = [sum(t.cost for t in l) or 1.0 for l in lists]
    pos = [0] * len(lists)
    done = [0.0] * len(lists)
    out = []
    for _ in range(sum(len(l) for l in lists)):
        live = [i for i in range(len(lists)) if pos[i] < len(lists[i])]
        j = min(live, key=lambda i: (done[i] + 0.5 * lists[i][pos[i]].cost) / totals[i])
        out.append(lists[j][pos[j]])
        done[j] += lists[j][pos[j]].cost
        pos[j] += 1
    return out


def _lagged(groups, lag=1):
    out = []
    for i, (mms, _) in enumerate(groups):
        out += mms
        if i >= lag:
            out += groups[i - lag][1]
    for _, ews in groups[max(len(groups) - lag, 0):]:
        out += ews
    return out


def _run(tasks):
    for t in tasks:
        t.run()


class _Branch(NamedTuple):
    heads: int
    dk: int
    dv: int
    scale: float
    q: Any
    k: Any
    b: Any
    o: Any
    state: Any


def _slab(buf, hc, dv):
    per = buf.shape[2] // dv
    return hc // per, slice((hc % per) * dv, (hc % per + 1) * dv)


def _store_decay(b_s, n, bt):
    per = COL_TILE // LANES
    for j in range(per):
        b_s[n * per + j] = bt[:, j * LANES:(j + 1) * LANES]


def _store_values(w_s, n, vt, dk, dv, nchunks):
    per = COL_TILE // dv
    for j in range(per):
        for ci in range(nchunks):
            i, cs = _slab(w_s, (n * per + j) * nchunks + ci, dv)
            v = vt[ci * CHUNK:(ci + 1) * CHUNK, j * dv:(j + 1) * dv]
            w_s[i, dk:dk + CHUNK, cs] = v
            w_s[i, dk + CHUNK:dk + 2 * CHUNK, cs] = v


def _recurrence_tasks(br, qa_s, kd_s, w_s, u_s, nchunks):
    nheads, dk, dv = br.heads, br.dk, br.dv
    keep = _score_mask()
    pairs = [(ci, h) for ci in range(nchunks) for h in range(nheads)]
    rows_of = lambda ci: slice(ci * CHUNK, (ci + 1) * CHUNK)
    cols = []

    def operands(ci, h):
        rows, kcols = rows_of(ci), slice(h * dk, (h + 1) * dk)
        q, k, b = br.q[rows, kcols], br.k[rows, kcols], br.b[h, rows, :]
        if br.scale != 1.0:
            q = q * br.scale
        qa_s[h, rows, 0:dk] = (q * jnp.exp2(b)).astype(BF16)
        kd_s[h, rows, :] = (k * jnp.exp2(b[CHUNK - 1:CHUNK, :] - b)).astype(BF16)
        qa_s[h, rows, dk:dk + LANES] = _intra_scores(q, k, b, keep).astype(BF16)

    def outer(ci, h):
        i, cs = _slab(w_s, h * nchunks + ci, dv)
        u_s[i, :, cs] = _dot_tn(kd_s[h, rows_of(ci), :], w_s[i, dk:dk + CHUNK, cs])

    def decay_columns():
        half = CHUNK // 2
        for h in range(nheads):
            cols.append(_columns(jnp.exp2(br.b[h, pl.ds(half - 1, 2 * nchunks, stride=half), :])))

    def chain(ci, h):
        i, cs = _slab(w_s, h * nchunks + ci, dv)
        s = br.state[0, 0, h]
        w_s[i, 0:dk, cs] = s.astype(BF16)
        br.state[0, 0, h] = s * cols[h][:, 2 * ci + 1:2 * ci + 2] + u_s[i, :, cs]

    def output(ci, h):
        i, cs = _slab(w_s, h * nchunks + ci, dv)
        br.o[rows_of(ci), h * dv:(h + 1) * dv] = _dot(qa_s[h, rows_of(ci), :], w_s[i, :, cs])

    bind = lambda cost, fn: [_Task(cost, lambda ci=ci, h=h: fn(ci, h)) for ci, h in pairs]
    return bind(70, operands), (bind(32, outer) + [_Task(16 * nheads, decay_columns)]
                                + bind(8 + dv // 8, chain) + bind(16, output))


def _sample_state_tasks(row0, e_ref, k_ref, q_ref, v_ref, s_ref, snew_ref, o_ref, nheads, dk, dv):
    nseq = s_ref.shape[0]
    first = lax.broadcasted_iota(jnp.int32, (BF16_SUBLANES, dk), 0) == 0
    ks = lambda h: slice(h * dk, (h + 1) * dk)
    vs = lambda h: slice(h * dv, (h + 1) * dv)
    mine = lambda j, width: lax.broadcasted_iota(jnp.int32, (SUBLANES, width), 0) == row0 + j

    def vec(ref, j, cols):
        return jnp.sum(jnp.where(mine(j, cols.stop - cols.start), ref[:, cols], 0.0), axis=0, keepdims=True)

    tile = lambda ref, j, cols: jnp.broadcast_to(vec(ref, j, cols), (BF16_SUBLANES, cols.stop - cols.start))
    pairs = [(h, j) for h in range(nheads) for j in range(nseq)]

    def outer(h, j):
        kj = jnp.where(first, tile(k_ref, j, ks(h)), 0.0).astype(BF16)
        snew_ref[j, h] = _dot_tn(kj, tile(v_ref, j, vs(h)).astype(BF16))

    def update(h, j):
        ecol = jnp.broadcast_to(vec(e_ref, j, ks(h)), (dk, LANES)).T
        ecol = ecol if dv == LANES else jnp.concatenate([ecol] * (dv // LANES), axis=1)
        snew_ref[j, h] = s_ref[j, h] * ecol + snew_ref[j, h]

    def readout(h, j):
        o = _dot(tile(q_ref, j, ks(h)).astype(BF16), snew_ref[j, h].astype(BF16))
        o_ref[:, vs(h)] = jnp.where(mine(j, dv), o[0:SUBLANES, :], o_ref[:, vs(h)])

    bind = lambda cost, fn: [_Task(cost, lambda h=h, j=j: fn(h, j)) for h, j in pairs]
    return bind(40.0, outer), bind(30.0, update), bind(20.0, readout)


def _prompt_kernel(x_ref, seg_ref, skg_ref, sqg_ref, svg_ref, seh_ref, skh_ref, sqh_ref, svh_ref, ssg_ref, ssh_ref,
                   win_ref, wga_ref, wglr_ref, bg_ref, gng_ref, wbg_ref,
                   lbp_ref, hng_ref, wbh_ref, wout_ref, lng_ref, lnb_ref,
                   y_ref, sg_ref, sh_ref, ssg_new_ref, ssh_new_ref, soa_ref, sob_ref,
                   xb_s, tri_s, ga_s, qg_s, kg_s, bgl_s, qh_s, kh_s, bh_s, h3_s, qa_s, kd_s, w_s, u_s,
                   og_s, oh_s, sgr_s, shr_s, mga_s, mgb_s, g_s, m_s, p_s):
    tb = TOK_BLOCK
    nch = tb // CHUNK
    tiles = lambda width: range(width // COL_TILE)
    col = lambda n, base=0: slice(base + n * COL_TILE, base + (n + 1) * COL_TILE)
    gla = _Branch(GLA_H, GLA_DK, GLA_DV, GLA_DK ** -0.5, qg_s, kg_s, bgl_s, og_s, sg_ref)
    hgrn = _Branch(HG_H, HG_DK, HG_DV, 1.0, qh_s, kh_s, bh_s, oh_s, sh_ref)
    mm_cost = 256.0

    @pl.when(pl.program_id(1) == 0)
    def _():
        sg_ref[...] = jnp.zeros_like(sg_ref)
        sh_ref[...] = jnp.zeros_like(sh_ref)
        tri_s[...] = _block_tri(tb)

    nseq = ssg_ref.shape[0]
    lin = pl.program_id(0) * pl.num_programs(1) + pl.program_id(1)
    row0 = (lin % (SUBLANES // nseq)) * nseq

    @pl.when(row0 == 0)
    def _():
        soa_ref[...] = jnp.zeros_like(soa_ref)
        sob_ref[...] = jnp.zeros_like(sob_ref)

    xb_s[...] = x_ref[0].astype(BF16)
    lb = _lower_bound(lbp_ref[...])

    def mm(dst, w_ref, wcols, lhs=xb_s, cast=None, cost=mm_cost):
        def run():
            r = _dot(lhs[...], w_ref[:, wcols])
            dst(r if cast is None else r.astype(cast))
        return _Task(cost, run)

    def store(ref, cols):
        def put(v):
            ref[:, cols] = v
        return put

    def ew(cost, fn, rows=(0, tb), step=ROW_BLOCK):
        def run():
            for r in range(rows[0], rows[1], step):
                fn(slice(r, r + step))
        return _Task(cost, run)

    def cumsum_task(b_s, n):
        def run():
            t = tri_s[...]
            _store_decay(b_s, n, _dot(t, h3_s[0, :, col(n)]) + _dot(t, h3_s[1, :, col(n)]) + _dot(t, h3_s[2, :, col(n)]))
        return _Task(192.0, run)

    def put_split3(rb, cs, g):
        hi, mid, lo = _split3(g)
        h3_s[0, rb, cs] = hi
        h3_s[1, rb, cs] = mid
        h3_s[2, rb, cs] = lo

    def gla_logit(n):
        def run():
            qg_s[:, col(n)] = _dot(ga_s[...], wglr_ref[:, col(n)]) + bg_ref[:, col(n)]
        return _Task(64.0, run)

    def gla_log(n):
        def fn(rb):
            put_split3(rb, col(n), _log_sigmoid(qg_s[rb, col(n)]) * (LOG2_E / GLA_TAU))
        return ew(300.0, fn)

    gla_in = [mm(store(ga_s, slice(None)), wga_ref, slice(None), cast=BF16, cost=128.0)]
    gla_in += [mm(store(kg_s, col(n)), win_ref, col(n, C_GK)) for n in tiles(GLA_K)]
    gla_in += [gla_logit(n) for n in tiles(GLA_K)]
    gla_in += [t for n in tiles(GLA_K) for t in (
        mm(lambda v, n=n: _store_values(w_s, n, v, GLA_DK, GLA_DV, nch), win_ref, col(n, C_GV), cast=BF16),
        gla_log(n),
        mm(lambda v, n=n: _store_values(w_s, n + GLA_K // COL_TILE, v, GLA_DK, GLA_DV, nch), win_ref,
           col(n + GLA_K // COL_TILE, C_GV), cast=BF16))]
    gla_in += [cumsum_task(bgl_s, n) for n in tiles(GLA_K)]
    gla_in += [mm(store(qg_s, col(n)), win_ref, col(n, C_GQ)) for n in tiles(GLA_K)]
    _run(gla_in)

    def hq_ew(n):
        def fn(rb):
            qh_s[rb, col(n)] = _silu(qh_s[rb, col(n)])
        return ew(130.0, fn)

    def hf_ew(n):
        def fn(rb):
            sf, sfn = _sigmoid_pair(kh_s[rb, col(n)])
            lbn = lb[:, col(n)]
            kh_s[rb, col(n)] = (1.0 - lbn) * sfn
            put_split3(rb, col(n), jnp.log2(lbn + (1.0 - lbn) * sf))
        return ew(450.0, fn)

    def gate_ew(ref, act, n):
        def fn(rb):
            ref[rb, col(n)] = act(ref[rb, col(n)])
        return ew(130.0, fn)

    hg_in = [([mm(store(kh_s, col(n)), win_ref, col(n, C_HF)),
               mm(store(qh_s, col(n)), win_ref, col(n, C_HQ))],
              [hf_ew(n), hq_ew(n), cumsum_task(bh_s, n)]) for n in tiles(HG_K)]
    gates = [([mm(store(sgr_s, col(n)), win_ref, col(n, C_GR))], [gate_ew(sgr_s, _silu, n)]) for n in tiles(GLA_V)]
    gates += [([mm(store(mga_s, col(n)), win_ref, col(n, C_MA))], [gate_ew(mga_s, _sigmoid, n)])
              for n in tiles(D_MODEL)]
    gates_b = [([mm(store(shr_s, col(n)), win_ref, col(n, C_HR))], [gate_ew(shr_s, _silu, n)]) for n in tiles(HG_V)]
    hg_v = [mm(lambda v, n=n: _store_values(w_s, n, v, HG_DK, HG_DV, nch), win_ref, col(n, C_HI), cast=BF16)
            for n in tiles(HG_V)]
    mgb = [([mm(store(mgb_s, col(n)), win_ref, col(n, C_MB))], [gate_ew(mgb_s, _sigmoid, n)]) for n in tiles(D_MODEL)]

    ra_ops, ra_rest = _recurrence_tasks(gla, qa_s, kd_s, w_s, u_s, nch)
    s_outer, s_update, s_readout = [], [], []
    for args in ((seg_ref, skg_ref, sqg_ref, svg_ref, ssg_ref, ssg_new_ref, soa_ref, GLA_H, GLA_DK, GLA_DV),
                 (seh_ref, skh_ref, sqh_ref, svh_ref, ssh_ref, ssh_new_ref, sob_ref, HG_H, HG_DK, HG_DV)):
        outer, update, readout = _sample_state_tasks(row0, *args)
        s_outer += outer
        s_update += update
        s_readout += readout

    _run(_merge(ra_ops + ra_rest, _lagged(hg_in + gates), s_outer))

    def gla_norm(n):
        def fn(rb):
            g_s[rb, col(n)] = (_head_rmsnorm(og_s[rb, col(n)], gng_ref[:, col(n)], GLA_DV) * sgr_s[rb, col(n)]).astype(BF16)
        return ew(200.0, fn)

    def gla_scale(n):
        def fn(rb):
            p_s[rb, col(n)] = p_s[rb, col(n)] * mga_s[rb, col(n)]
        return ew(50.0, fn)

    gla_post = [gla_norm(n) for n in tiles(GLA_V)]
    for n in tiles(D_MODEL):
        gla_post += [mm(store(p_s, col(n)), wbg_ref, col(n), lhs=g_s), gla_scale(n)]
    rb_ops, rb_rest = _recurrence_tasks(hgrn, qa_s, kd_s, w_s, u_s, nch)
    _run(_merge(rb_ops, hg_v + _lagged(gates_b)))
    _run(_merge(rb_rest, gla_post, s_update))

    def hgrn_norm(n):
        def fn(rb):
            m_s[rb, col(n)] = (_head_rmsnorm(oh_s[rb, col(n)], hng_ref[:, col(n)], HG_DV) * shr_s[rb, col(n)]).astype(BF16)
        return ew(220.0, fn)

    def merge_ew(n):
        def fn(rb):
            g_s[rb, col(n)] = (p_s[rb, col(n)] + mgb_s[rb, col(n)] * sgr_s[rb, col(n)]).astype(BF16)
        return ew(60.0, fn)

    def final_ew(rows):
        def fn(rb):
            z = DEEPNORM_ALPHA * x_ref[0, rb, :] + shr_s[rb, :]
            y_ref[0, rb, :] = _layernorm(z, lng_ref[...], lnb_ref[...])
        return ew(180.0, fn, rows=rows, step=LN_ROW_BLOCK)

    _run(_merge([hgrn_norm(n) for n in tiles(HG_V)], _lagged(mgb)))
    _run([mm(store(sgr_s, col(n)), wbh_ref, col(n), lhs=m_s) for n in tiles(D_MODEL)])
    _run([merge_ew(n) for n in tiles(D_MODEL)])
    tail = [mm(store(shr_s, col(n)), wout_ref, col(n), lhs=g_s) for n in tiles(D_MODEL)]
    tail += [final_ew((r, r + ROW_BLOCK)) for r in range(0, tb, ROW_BLOCK)]
    _run(_merge(tail, s_readout))


def _whole(shape):
    return pl.BlockSpec(memory_space=pltpu.VMEM)


def _prompt_call(x, sample_rows, sample_states, win, wga, wglr, bg, gng, wbg, lbp, hng, wbh, wout, lng, lnb):
    bsz, seq, d = x.shape
    tb = TOK_BLOCK
    nch = tb // CHUNK
    per_row = seq // tb
    steps = bsz * per_row
    nsample = sample_states[0].shape[0]
    nseq = nsample // steps
    assert nseq * steps == nsample and SUBLANES % nseq == 0 and all(r.shape[0] == nsample for r in sample_rows)
    weights = (win, wga, wglr, bg, gng, wbg, lbp, hng, wbh, wout, lng, lnb)
    lin = lambda b, t: b * per_row + t
    row_block = lambda b, t: (lin(b, t) // (SUBLANES // nseq), 0)
    row_spec = lambda r: pl.BlockSpec((SUBLANES, r.shape[1]), row_block)
    state_spec = lambda s: pl.BlockSpec((nseq,) + s.shape[1:], lambda b, t: (lin(b, t), 0, 0, 0))
    return pl.pallas_call(
        _prompt_kernel,
        grid=(bsz, per_row),
        in_specs=[pl.BlockSpec((1, tb, d), lambda b, t: (b, t, 0))] + [row_spec(r) for r in sample_rows]
        + [state_spec(s) for s in sample_states] + [_whole(w.shape) for w in weights],
        out_specs=[
            pl.BlockSpec((1, tb, d), lambda b, t: (b, t, 0)),
            pl.BlockSpec((1, 1, GLA_H, GLA_DK, GLA_DV), lambda b, t: (0, b, 0, 0, 0)),
            pl.BlockSpec((1, 1, HG_H, HG_DK, HG_DV), lambda b, t: (0, b, 0, 0, 0)),
            state_spec(sample_states[0]), state_spec(sample_states[1]),
            pl.BlockSpec((SUBLANES, GLA_V), row_block),
            pl.BlockSpec((SUBLANES, HG_V), row_block),
        ],
        out_shape=[
            jax.ShapeDtypeStruct((bsz, seq, d), F32),
            jax.ShapeDtypeStruct((DEPTH, bsz, GLA_H, GLA_DK, GLA_DV), F32),
            jax.ShapeDtypeStruct((DEPTH, bsz, HG_H, HG_DK, HG_DV), F32),
            jax.ShapeDtypeStruct(sample_states[0].shape, F32),
            jax.ShapeDtypeStruct(sample_states[1].shape, F32),
            jax.ShapeDtypeStruct((nsample, GLA_V), F32),
            jax.ShapeDtypeStruct((nsample, HG_V), F32),
        ],
        scratch_shapes=[
            pltpu.VMEM((tb, d), BF16),
            pltpu.VMEM((tb, tb), BF16),
            pltpu.VMEM((tb, LANES), BF16),
            pltpu.VMEM((tb, GLA_K), F32),
            pltpu.VMEM((tb, GLA_K), F32),
            pltpu.VMEM((GLA_H, tb, LANES), F32),
            pltpu.VMEM((tb, HG_K), F32),
            pltpu.VMEM((tb, HG_K), F32),
            pltpu.VMEM((HG_H, tb, LANES), F32),
            pltpu.VMEM((3, tb, HG_K), BF16),
            pltpu.VMEM((HG_H, tb, 2 * LANES), BF16),
            pltpu.VMEM((HG_H, tb, LANES), BF16),
            pltpu.VMEM((GLA_H * nch, 2 * LANES, GLA_DV), BF16),
            pltpu.VMEM((GLA_H * nch, GLA_DK, GLA_DV), F32),
            pltpu.VMEM((tb, GLA_V), F32),
            pltpu.VMEM((tb, HG_V), F32),
            pltpu.VMEM((tb, GLA_V), F32),
            pltpu.VMEM((tb, HG_V), F32),
            pltpu.VMEM((tb, d), F32),
            pltpu.VMEM((tb, d), F32),
            pltpu.VMEM((tb, d), BF16),
            pltpu.VMEM((tb, d), BF16),
            pltpu.VMEM((tb, d), F32),
        ],
        compiler_params=pltpu.CompilerParams(
            dimension_semantics=("arbitrary", "arbitrary"), vmem_limit_bytes=VMEM_LIMIT),
        name="prompt_layer",
    )(x, *sample_rows, *sample_states, *weights)


def _sample_proj_kernel(x_ref, win_ref, wga_ref, wglr_ref, bg_ref, lbp_ref,
                        eg_ref, kg_ref, qg_ref, vg_ref, eh_ref, kh_ref, qh_ref, vh_ref,
                        sgr_ref, shr_ref, mga_ref, mgb_ref):
    xb = x_ref[:, 0, :].astype(BF16)
    proj = lambda base, width: _dot(xb, win_ref[:, base:base + width])
    qg_ref[...] = proj(C_GQ, GLA_K) * (GLA_DK ** -0.5)
    kg_ref[...] = proj(C_GK, GLA_K)
    vg_ref[...] = proj(C_GV, GLA_V)
    sgr_ref[...] = _silu(proj(C_GR, GLA_V))
    ga = _dot(xb, wga_ref[...]).astype(BF16)
    a_logit = _dot(ga, wglr_ref[...]) + bg_ref[...]
    eg_ref[...] = jnp.exp(_log_sigmoid(a_logit) * (1.0 / GLA_TAU))
    lb = _lower_bound(lbp_ref[...])
    qh_ref[...] = _silu(proj(C_HQ, HG_K))
    sf, sfn = _sigmoid_pair(proj(C_HF, HG_K))
    eh_ref[...] = jnp.exp(jnp.log(lb + (1.0 - lb) * sf))
    kh_ref[...] = (1.0 - lb) * sfn
    vh_ref[...] = proj(C_HI, HG_V)
    shr_ref[...] = _silu(proj(C_HR, HG_V))
    mga_ref[...] = _sigmoid(proj(C_MA, D_MODEL))
    mgb_ref[...] = _sigmoid(proj(C_MB, D_MODEL))


def _sample_proj_call(x, win, wga, wglr, bg, lbp):
    n = x.shape[0]
    widths = (GLA_K, GLA_K, GLA_K, GLA_V, HG_K, HG_K, HG_K, HG_V, GLA_V, HG_V, D_MODEL, D_MODEL)
    args = (x, win, wga, wglr, bg, lbp)
    return pl.pallas_call(
        _sample_proj_kernel,
        in_specs=[_whole(a.shape) for a in args],
        out_specs=[_whole((n, w)) for w in widths],
        out_shape=[jax.ShapeDtypeStruct((n, w), F32) for w in widths],
        compiler_params=pltpu.CompilerParams(vmem_limit_bytes=VMEM_LIMIT),
        name="sample_proj",
    )(*args)


def _sample_post_kernel(x_ref, oa_ref, ob_ref, sgr_ref, shr_ref, mga_ref, mgb_ref,
                        gng_ref, wbg_ref, hng_ref, wbh_ref, wout_ref, lng_ref, lnb_ref, y_ref):
    ga = (_head_rmsnorm(oa_ref[...], gng_ref[...], GLA_DV) * sgr_ref[...]).astype(BF16)
    gb = (_head_rmsnorm(ob_ref[...], hng_ref[...], HG_DV) * shr_ref[...]).astype(BF16)
    merged = mga_ref[...] * _dot(ga, wbg_ref[...]) + mgb_ref[...] * _dot(gb, wbh_ref[...])
    z = DEEPNORM_ALPHA * x_ref[:, 0, :] + _dot(merged.astype(BF16), wout_ref[...])
    y_ref[:, 0, :] = _layernorm(z, lng_ref[...], lnb_ref[...])


def _sample_post_call(*args):
    n = args[0].shape[0]
    return pl.pallas_call(
        _sample_post_kernel,
        in_specs=[_whole(a.shape) for a in args],
        out_specs=_whole((n, 1, D_MODEL)),
        out_shape=jax.ShapeDtypeStruct((n, 1, D_MODEL), F32),
        compiler_params=pltpu.CompilerParams(vmem_limit_bytes=VMEM_LIMIT),
        name="sample_post",
    )(*args)


def _weight_prep_kernel(wt_ref, gat_ref, win_ref, wga_ref):
    win_ref[...] = wt_ref[...].T.astype(BF16)

    @pl.when(pl.program_id(0) == 0)
    def _():
        d = gat_ref.shape[1]
        ga = jnp.concatenate([gat_ref[...], jnp.zeros((LANES - GLA_RANK, d), F32)], axis=0)
        wga_ref[...] = ga.T.astype(BF16)


def _weight_prep_call(w_in):
    _, d, width = w_in.shape
    wt = jnp.swapaxes(w_in, 1, 2)[0]
    blk = WEIGHT_PREP_COLS
    assert C_HQ % blk == 0 and (width - GLA_RANK) % blk == 0
    src_row = lambda i: (i * (blk // GLA_RANK) + (i >= C_HQ // blk).astype(jnp.int32)) * GLA_RANK
    return pl.pallas_call(
        _weight_prep_kernel,
        grid=((width - GLA_RANK) // blk,),
        in_specs=[pl.BlockSpec((pl.Element(blk), pl.Element(d)), lambda i: (src_row(i), 0)),
                  pl.BlockSpec((pl.Element(GLA_RANK), pl.Element(d)), lambda i: (C_HQ, 0))],
        out_specs=[pl.BlockSpec((d, blk), lambda i: (0, i)),
                   pl.BlockSpec((d, LANES), lambda i: (0, 0))],
        out_shape=[jax.ShapeDtypeStruct((d, width - GLA_RANK), BF16), jax.ShapeDtypeStruct((d, LANES), BF16)],
        compiler_params=pltpu.CompilerParams(dimension_semantics=("arbitrary",), vmem_limit_bytes=VMEM_LIMIT),
        name="weight_prep",
    )(wt, wt)


def kernel(x_prompt, x_sample, state_gla, state_hgrn, w_in, w_gate_lr, b_gate_lr, gla_norm_g, w_br_gla,
           hgrn_lb_param, hgrn_norm_g, w_br_hgrn, w_out, ln_g, ln_b):
    assert w_in.shape[0] == DEPTH and x_sample.shape[1] == 1
    win, wga = _weight_prep_call(w_in)
    wglr = jnp.pad(w_gate_lr[0], ((0, LANES - GLA_RANK), (0, 0))).astype(BF16)
    bg = b_gate_lr[0].reshape(1, GLA_K)
    gng = gla_norm_g[0].reshape(1, GLA_V)
    hng = hgrn_norm_g[0].reshape(1, HG_V)
    wbg = w_br_gla[0].astype(BF16)
    wbh = w_br_hgrn[0].astype(BF16)
    wout = w_out[0].astype(BF16)
    lng = ln_g[0].reshape(1, D_MODEL)
    lnb = ln_b[0].reshape(1, D_MODEL)
    lbp = hgrn_lb_param

    *rows, sgr, shr, mga, mgb = _sample_proj_call(x_sample, win, wga, wglr, bg, lbp)

    y_prompt, gla_p, hgrn_p, gla_s, hgrn_s, oa, ob = _prompt_call(
        x_prompt, rows, (state_gla[0], state_hgrn[0]), win, wga, wglr, bg, gng, wbg, lbp, hng, wbh, wout, lng, lnb)

    ys = _sample_post_call(x_sample, oa, ob, sgr, shr, mga, mgb, gng, wbg, hng, wbh, wout, lng, lnb)
    return (y_prompt, ys, gla_p, hgrn_p, gla_s[None], hgrn_s[None])
```

```python
from typing import Any, NamedTuple

import jax
import jax.numpy as jnp
from jax import lax
from jax.experimental import pallas as pl
from jax.experimental.pallas import tpu as pltpu

F32 = jnp.float32
BF16 = jnp.bfloat16

D_MODEL = 1024
GLA_H, GLA_DK, GLA_DV = 4, 128, 256
HG_H, HG_DK, HG_DV = 8, 128, 128
GLA_RANK = 16
GLA_TAU = 16.0
LOG2_E = 1.4426950408889634
GLA_K = GLA_H * GLA_DK
GLA_V = GLA_H * GLA_DV
HG_K = HG_H * HG_DK
HG_V = HG_H * HG_DV
C_GQ, C_GK, C_GV, C_GR = 0, GLA_K, 2 * GLA_K, 2 * GLA_K + GLA_V
C_HQ = C_GR + GLA_V
C_HF, C_HI, C_HR = C_HQ + HG_K, C_HQ + 2 * HG_K, C_HQ + 2 * HG_K + HG_V
C_MA = C_HR + HG_V
C_MB = C_MA + D_MODEL
CHUNK = 64
SUB = 16
NORM_EPS = 1e-5
DEPTH = 1
DEEPNORM_ALPHA = (2.0 * DEPTH) ** 0.25
LANES = 128
SUBLANES = 8
BF16_SUBLANES = 16
TOK_BLOCK = 256
COL_TILE = 256
ROW_BLOCK = 64
LN_ROW_BLOCK = 16
WEIGHT_PREP_COLS = 1024
VMEM_LIMIT = 60 * 1024 * 1024


def _dot(a, b):
    return jnp.dot(a, b, preferred_element_type=F32)


def _dot_nt(a, b):
    return lax.dot_general(a, b, (((1,), (1,)), ((), ())), preferred_element_type=F32)


def _dot_tn(a, b):
    return lax.dot_general(a, b, (((0,), (0,)), ((), ())), preferred_element_type=F32)


def _sigmoid(x):
    return 1.0 / (1.0 + jnp.exp(-x))


def _sigmoid_pair(x):
    e = jnp.exp(-jnp.abs(x))
    s = 1.0 / (1.0 + e)
    t = e * s
    pos = x >= 0.0
    return jnp.where(pos, s, t), jnp.where(pos, t, s)


def _silu(x):
    return x * _sigmoid(x)


def _log_sigmoid(x):
    return jnp.minimum(x, 0.0) - jnp.log1p(jnp.exp(-jnp.abs(x)))


def _split3(x):
    hi = x.astype(BF16)
    r = x - hi.astype(F32)
    mid = r.astype(BF16)
    lo = (r - mid.astype(F32)).astype(BF16)
    return hi, mid, lo


def _block_tri(n):
    r = lax.broadcasted_iota(jnp.int32, (n, n), 0)
    c = lax.broadcasted_iota(jnp.int32, (n, n), 1)
    shift = CHUNK.bit_length() - 1
    keep = (c <= r) & (jnp.right_shift(r, shift) == jnp.right_shift(c, shift))
    return jnp.where(keep, 1.0, 0.0).astype(BF16)


def _lower_bound(lbp):
    rows = [lbp[i:i + 1, :] for i in range(lbp.shape[0])]
    m = rows[0]
    for r in rows[1:]:
        m = jnp.maximum(m, r)
    es = [jnp.exp(r - m) for r in rows]
    tot = es[0]
    for e in es[1:]:
        tot = tot + e
    return es[0] / tot


def _score_mask():
    t = lax.broadcasted_iota(jnp.int32, (CHUNK, LANES), 0)
    lane = lax.broadcasted_iota(jnp.int32, (CHUNK, LANES), 1)
    s = lane - CHUNK * (jnp.right_shift(t, SUB.bit_length() - 1) & 1)
    return (s >= 0) & (s <= t)


def _intra_scores(q, k, b, keep):
    dk = q.shape[1]
    nsub = CHUNK // SUB
    qs, ks = [], []
    for i in range(nsub):
        lo, hi = i * SUB, (i + 1) * SUB
        bs = b[lo + SUB // 2:lo + SUB // 2 + 1, :]
        qs.append(q[lo:hi] * jnp.exp2(b[lo:hi] - bs))
        ks.append((k[0:hi] * jnp.exp2(bs - b[0:hi])).astype(BF16))
        if hi < CHUNK:
            ks.append(jnp.zeros((CHUNK - hi, dk), BF16))
    prod = _dot_nt(jnp.concatenate(qs, axis=0).astype(BF16), jnp.concatenate(ks, axis=0))
    per = LANES // CHUNK
    a = jnp.concatenate([prod[i * SUB:(i + 1) * SUB, (i // per) * LANES:(i // per + 1) * LANES]
                         for i in range(nsub)], axis=0)
    return jnp.where(keep, a, 0.0)


def _columns(rows):
    r = rows.shape[0]
    if r < LANES:
        rows = jnp.concatenate([rows, jnp.zeros((LANES - r, LANES), F32)], axis=0)
    return rows.T


def _head_rmsnorm(o, gain, dv):
    outs = []
    for j in range(o.shape[1] // dv):
        oj = o[:, j * dv:(j + 1) * dv]
        ms = jnp.mean(oj * oj, axis=-1, keepdims=True)
        outs.append(oj * lax.rsqrt(ms + NORM_EPS))
    on = outs[0] if len(outs) == 1 else jnp.concatenate(outs, axis=1)
    return on * gain


def _layernorm(z, g, b):
    mu = jnp.mean(z, axis=-1, keepdims=True)
    zc = z - mu
    var = jnp.mean(zc * zc, axis=-1, keepdims=True)
    return zc * lax.rsqrt(var + NORM_EPS) * g + b


class _Task(NamedTuple):
    cost: float
    run: Any


def _merge(*lists):
    totals = [sum(t.cost for t in l) or 1.0 for l in lists]
    pos = [0] * len(lists)
    done = [0.0] * len(lists)
    out = []
    for _ in range(sum(len(l) for l in lists)):
        live = [i for i in range(len(lists)) if pos[i] < len(lists[i])]
        j = min(live, key=lambda i: (done[i] + 0.5 * lists[i][pos[i]].cost) / totals[i])
        out.append(lists[j][pos[j]])
        done[j] += lists[j][pos[j]].cost
        pos[j] += 1
    return out


def _lagged(groups, lag=1):
    out = []
    for i, (mms, _) in enumerate(groups):
        out += mms
        if i >= lag:
            out += groups[i - lag][1]
    for _, ews in groups[max(len(groups) - lag, 0):]:
        out += ews
    return out


def _run(tasks):
    for t in tasks:
        t.run()


class _Branch(NamedTuple):
    heads: int
    dk: int
    dv: int
    scale: float
    q: Any
    k: Any
    b: Any
    o: Any
    state: Any


def _slab(buf, hc, dv):
    per = buf.shape[2] // dv
    return hc // per, slice((hc % per) * dv, (hc % per + 1) * dv)


def _store_decay(b_s, n, bt):
    per = COL_TILE // LANES
    for j in range(per):
        b_s[n * per + j] = bt[:, j * LANES:(j + 1) * LANES]


def _store_values(w_s, n, vt, dk, dv, nchunks):
    per = COL_TILE // dv
    for j in range(per):
        for ci in range(nchunks):
            i, cs = _slab(w_s, (n * per + j) * nchunks + ci, dv)
            v = vt[ci * CHUNK:(ci + 1) * CHUNK, j * dv:(j + 1) * dv]
            w_s[i, dk:dk + CHUNK, cs] = v
            w_s[i, dk + CHUNK:dk + 2 * CHUNK, cs] = v


def _recurrence_tasks(br, qa_s, kd_s, w_s, u_s, nchunks):
    nheads, dk, dv = br.heads, br.dk, br.dv
    keep = _score_mask()
    pairs = [(ci, h) for ci in range(nchunks) for h in range(nheads)]
    rows_of = lambda ci: slice(ci * CHUNK, (ci + 1) * CHUNK)
    cols = []

    def operands(ci, h):
        rows, kcols = rows_of(ci), slice(h * dk, (h + 1) * dk)
        q, k, b = br.q[rows, kcols], br.k[rows, kcols], br.b[h, rows, :]
        if br.scale != 1.0:
            q = q * br.scale
        qa_s[h, rows, 0:dk] = (q * jnp.exp2(b)).astype(BF16)
        kd_s[h, rows, :] = (k * jnp.exp2(b[CHUNK - 1:CHUNK, :] - b)).astype(BF16)
        qa_s[h, rows, dk:dk + LANES] = _intra_scores(q, k, b, keep).astype(BF16)

    def outer(ci, h):
        i, cs = _slab(w_s, h * nchunks + ci, dv)
        u_s[i, :, cs] = _dot_tn(kd_s[h, rows_of(ci), :], w_s[i, dk:dk + CHUNK, cs])

    def decay_columns():
        half = CHUNK // 2
        for h in range(nheads):
            cols.append(_columns(jnp.exp2(br.b[h, pl.ds(half - 1, 2 * nchunks, stride=half), :])))

    def chain(ci, h):
        i, cs = _slab(w_s, h * nchunks + ci, dv)
        s = br.state[0, 0, h]
        w_s[i, 0:dk, cs] = s.astype(BF16)
        br.state[0, 0, h] = s * cols[h][:, 2 * ci + 1:2 * ci + 2] + u_s[i, :, cs]

    def output(ci, h):
        i, cs = _slab(w_s, h * nchunks + ci, dv)
        br.o[rows_of(ci), h * dv:(h + 1) * dv] = _dot(qa_s[h, rows_of(ci), :], w_s[i, :, cs])

    bind = lambda cost, fn: [_Task(cost, lambda ci=ci, h=h: fn(ci, h)) for ci, h in pairs]
    return bind(70, operands), (bind(32, outer) + [_Task(16 * nheads, decay_columns)]
                                + bind(8 + dv // 8, chain) + bind(16, output))


def _sample_state_tasks(row0, e_ref, k_ref, q_ref, v_ref, s_ref, snew_ref, o_ref, nheads, dk, dv):
    nseq = s_ref.shape[0]
    first = lax.broadcasted_iota(jnp.int32, (BF16_SUBLANES, dk), 0) == 0
    ks = lambda h: slice(h * dk, (h + 1) * dk)
    vs = lambda h: slice(h * dv, (h + 1) * dv)
    mine = lambda j, width: lax.broadcasted_iota(jnp.int32, (SUBLANES, width), 0) == row0 + j

    def vec(ref, j, cols):
        return jnp.sum(jnp.where(mine(j, cols.stop - cols.start), ref[:, cols], 0.0), axis=0, keepdims=True)

    tile = lambda ref, j, cols: jnp.broadcast_to(vec(ref, j, cols), (BF16_SUBLANES, cols.stop - cols.start))
    pairs = [(h, j) for h in range(nheads) for j in range(nseq)]

    def outer(h, j):
        kj = jnp.where(first, tile(k_ref, j, ks(h)), 0.0).astype(BF16)
        snew_ref[j, h] = _dot_tn(kj, tile(v_ref, j, vs(h)).astype(BF16))

    def update(h, j):
        ecol = jnp.broadcast_to(vec(e_ref, j, ks(h)), (dk, LANES)).T
        ecol = ecol if dv == LANES else jnp.concatenate([ecol] * (dv // LANES), axis=1)
        snew_ref[j, h] = s_ref[j, h] * ecol + snew_ref[j, h]

    def readout(h, j):
        o = _dot(tile(q_ref, j, ks(h)).astype(BF16), snew_ref[j, h].astype(BF16))
        o_ref[:, vs(h)] = jnp.where(mine(j, dv), o[0:SUBLANES, :], o_ref[:, vs(h)])

    bind = lambda cost, fn: [_Task(cost, lambda h=h, j=j: fn(h, j)) for h, j in pairs]
    return bind(40.0, outer), bind(30.0, update), bind(20.0, readout)


def _prompt_kernel(x_ref, seg_ref, skg_ref, sqg_ref, svg_ref, seh_ref, skh_ref, sqh_ref, svh_ref, ssg_ref, ssh_ref,
                   win_ref, wga_ref, wglr_ref, bg_ref, gng_ref, wbg_ref,
                   lbp_ref, hng_ref, wbh_ref, wout_ref, lng_ref, lnb_ref,
                   y_ref, sg_ref, sh_ref, ssg_new_ref, ssh_new_ref, soa_ref, sob_ref,
                   xb_s, tri_s, ga_s, qg_s, kg_s, bgl_s, qh_s, kh_s, bh_s, h3_s, qa_s, kd_s, w_s, u_s,
                   og_s, oh_s, sgr_s, shr_s, mga_s, mgb_s, g_s, m_s, p_s):
    tb = TOK_BLOCK
    nch = tb // CHUNK
    tiles = lambda width: range(width // COL_TILE)
    col = lambda n, base=0: slice(base + n * COL_TILE, base + (n + 1) * COL_TILE)
    gla = _Branch(GLA_H, GLA_DK, GLA_DV, GLA_DK ** -0.5, qg_s, kg_s, bgl_s, og_s, sg_ref)
    hgrn = _Branch(HG_H, HG_DK, HG_DV, 1.0, qh_s, kh_s, bh_s, oh_s, sh_ref)
    mm_cost = 256.0

    @pl.when(pl.program_id(1) == 0)
    def _():
        sg_ref[...] = jnp.zeros_like(sg_ref)
        sh_ref[...] = jnp.zeros_like(sh_ref)
        tri_s[...] = _block_tri(tb)

    nseq = ssg_ref.shape[0]
    lin = pl.program_id(0) * pl.num_programs(1) + pl.program_id(1)
    row0 = (lin % (SUBLANES // nseq)) * nseq

    @pl.when(row0 == 0)
    def _():
        soa_ref[...] = jnp.zeros_like(soa_ref)
        sob_ref[...] = jnp.zeros_like(sob_ref)

    xb_s[...] = x_ref[0].astype(BF16)
    lb = _lower_bound(lbp_ref[...])

    def mm(dst, w_ref, wcols, lhs=xb_s, cast=None, cost=mm_cost):
        def run():
            r = _dot(lhs[...], w_ref[:, wcols])
            dst(r if cast is None else r.astype(cast))
        return _Task(cost, run)

    def store(ref, cols):
        def put(v):
            ref[:, cols] = v
        return put

    def ew(cost, fn, rows=(0, tb), step=ROW_BLOCK):
        def run():
            for r in range(rows[0], rows[1], step):
                fn(slice(r, r + step))
        return _Task(cost, run)

    def cumsum_task(b_s, n):
        def run():
            t = tri_s[...]
            _store_decay(b_s, n, _dot(t, h3_s[0, :, col(n)]) + _dot(t, h3_s[1, :, col(n)]) + _dot(t, h3_s[2, :, col(n)]))
        return _Task(192.0, run)

    def put_split3(rb, cs, g):
        hi, mid, lo = _split3(g)
        h3_s[0, rb, cs] = hi
        h3_s[1, rb, cs] = mid
        h3_s[2, rb, cs] = lo

    def gla_logit(n):
        def run():
            qg_s[:, col(n)] = _dot(ga_s[...], wglr_ref[:, col(n)]) + bg_ref[:, col(n)]
        return _Task(64.0, run)

    def gla_log(n):
        def fn(rb):
            put_split3(rb, col(n), _log_sigmoid(qg_s[rb, col(n)]) * (LOG2_E / GLA_TAU))
        return ew(300.0, fn)

    gla_in = [mm(store(ga_s, slice(None)), wga_ref, slice(None), cast=BF16, cost=128.0)]
    gla_in += [mm(store(kg_s, col(n)), win_ref, col(n, C_GK)) for n in tiles(GLA_K)]
    gla_in += [gla_logit(n) for n in tiles(GLA_K)]
    gla_in += [t for n in tiles(GLA_K) for t in (
        mm(lambda v, n=n: _store_values(w_s, n, v, GLA_DK, GLA_DV, nch), win_ref, col(n, C_GV), cast=BF16),
        gla_log(n),
        mm(lambda v, n=n: _store_values(w_s, n + GLA_K // COL_TILE, v, GLA_DK, GLA_DV, nch), win_ref,
           col(n + GLA_K // COL_TILE, C_GV), cast=BF16))]
    gla_in += [cumsum_task(bgl_s, n) for n in tiles(GLA_K)]
    gla_in += [mm(store(qg_s, col(n)), win_ref, col(n, C_GQ)) for n in tiles(GLA_K)]
    _run(gla_in)

    def hq_ew(n):
        def fn(rb):
            qh_s[rb, col(n)] = _silu(qh_s[rb, col(n)])
        return ew(130.0, fn)

    def hf_ew(n):
        def fn(rb):
            sf, sfn = _sigmoid_pair(kh_s[rb, col(n)])
            lbn = lb[:, col(n)]
            kh_s[rb, col(n)] = (1.0 - lbn) * sfn
            put_split3(rb, col(n), jnp.log2(lbn + (1.0 - lbn) * sf))
        return ew(450.0, fn)

    def gate_ew(ref, act, n):
        def fn(rb):
            ref[rb, col(n)] = act(ref[rb, col(n)])
        return ew(130.0, fn)

    hg_in = [([mm(store(kh_s, col(n)), win_ref, col(n, C_HF)),
               mm(store(qh_s, col(n)), win_ref, col(n, C_HQ))],
              [hf_ew(n), hq_ew(n), cumsum_task(bh_s, n)]) for n in tiles(HG_K)]
    gates = [([mm(store(sgr_s, col(n)), win_ref, col(n, C_GR))], [gate_ew(sgr_s, _silu, n)]) for n in tiles(GLA_V)]
    gates += [([mm(store(mga_s, col(n)), win_ref, col(n, C_MA))], [gate_ew(mga_s, _sigmoid, n)])
              for n in tiles(D_MODEL)]
    gates_b = [([mm(store(shr_s, col(n)), win_ref, col(n, C_HR))], [gate_ew(shr_s, _silu, n)]) for n in tiles(HG_V)]
    hg_v = [mm(lambda v, n=n: _store_values(w_s, n, v, HG_DK, HG_DV, nch), win_ref, col(n, C_HI), cast=BF16)
            for n in tiles(HG_V)]
    mgb = [([mm(store(mgb_s, col(n)), win_ref, col(n, C_MB))], [gate_ew(mgb_s, _sigmoid, n)]) for n in tiles(D_MODEL)]

    ra_ops, ra_rest = _recurrence_tasks(gla, qa_s, kd_s, w_s, u_s, nch)
    s_outer, s_update, s_readout = [], [], []
    for args in ((seg_ref, skg_ref, sqg_ref, svg_ref, ssg_ref, ssg_new_ref, soa_ref, GLA_H, GLA_DK, GLA_DV),
                 (seh_ref, skh_ref, sqh_ref, svh_ref, ssh_ref, ssh_new_ref, sob_ref, HG_H, HG_DK, HG_DV)):
        outer, update, readout = _sample_state_tasks(row0, *args)
        s_outer += outer
        s_update += update
        s_readout += readout

    _run(_merge(ra_ops + ra_rest, _lagged(hg_in + gates), s_outer))

    def gla_norm(n):
        def fn(rb):
            g_s[rb, col(n)] = (_head_rmsnorm(og_s[rb, col(n)], gng_ref[:, col(n)], GLA_DV) * sgr_s[rb, col(n)]).astype(BF16)
        return ew(200.0, fn)

    def gla_scale(n):
        def fn(rb):
            p_s[rb, col(n)] = p_s[rb, col(n)] * mga_s[rb, col(n)]
        return ew(50.0, fn)

    gla_post = [gla_norm(n) for n in tiles(GLA_V)]
    for n in tiles(D_MODEL):
        gla_post += [mm(store(p_s, col(n)), wbg_ref, col(n), lhs=g_s), gla_scale(n)]
    rb_ops, rb_rest = _recurrence_tasks(hgrn, qa_s, kd_s, w_s, u_s, nch)
    _run(_merge(rb_ops, hg_v + _lagged(gates_b)))
    _run(_merge(rb_rest, gla_post, s_update))

    def hgrn_norm(n):
        def fn(rb):
            m_s[rb, col(n)] = (_head_rmsnorm(oh_s[rb, col(n)], hng_ref[:, col(n)], HG_DV) * shr_s[rb, col(n)]).astype(BF16)
        return ew(220.0, fn)

    def merge_ew(n):
        def fn(rb):
            g_s[rb, col(n)] = (p_s[rb, col(n)] + mgb_s[rb, col(n)] * sgr_s[rb, col(n)]).astype(BF16)
        return ew(60.0, fn)

    def final_ew(rows):
        def fn(rb):
            z = DEEPNORM_ALPHA * x_ref[0, rb, :] + shr_s[rb, :]
            y_ref[0, rb, :] = _layernorm(z, lng_ref[...], lnb_ref[...])
        return ew(180.0, fn, rows=rows, step=LN_ROW_BLOCK)

    _run(_merge([hgrn_norm(n) for n in tiles(HG_V)], _lagged(mgb)))
    _run([mm(store(sgr_s, col(n)), wbh_ref, col(n), lhs=m_s) for n in tiles(D_MODEL)])
    _run([merge_ew(n) for n in tiles(D_MODEL)])
    tail = [mm(store(shr_s, col(n)), wout_ref, col(n), lhs=g_s) for n in tiles(D_MODEL)]
    tail += [final_ew((r, r + ROW_BLOCK)) for r in range(0, tb, ROW_BLOCK)]
    _run(_merge(tail, s_readout))


def _whole(shape):
    return pl.BlockSpec(memory_space=pltpu.VMEM)


def _prompt_call(x, sample_rows, sample_states, win, wga, wglr, bg, gng, wbg, lbp, hng, wbh, wout, lng, lnb):
    bsz, seq, d = x.shape
    tb = TOK_BLOCK
    nch = tb // CHUNK
    per_row = seq // tb
    steps = bsz * per_row
    nsample = sample_states[0].shape[0]
    nseq = nsample // steps
    assert nseq * steps == nsample and SUBLANES % nseq == 0 and all(r.shape[0] == nsample for r in sample_rows)
    weights = (win, wga, wglr, bg, gng, wbg, lbp, hng, wbh, wout, lng, lnb)
    lin = lambda b, t: b * per_row + t
    row_block = lambda b, t: (lin(b, t) // (SUBLANES // nseq), 0)
    row_spec = lambda r: pl.BlockSpec((SUBLANES, r.shape[1]), row_block)
    state_spec = lambda s: pl.BlockSpec((nseq,) + s.shape[1:], lambda b, t: (lin(b, t), 0, 0, 0))
    return pl.pallas_call(
        _prompt_kernel,
        grid=(bsz, per_row),
        in_specs=[pl.BlockSpec((1, tb, d), lambda b, t: (b, t, 0))] + [row_spec(r) for r in sample_rows]
        + [state_spec(s) for s in sample_states] + [_whole(w.shape) for w in weights],
        out_specs=[
            pl.BlockSpec((1, tb, d), lambda b, t: (b, t, 0)),
            pl.BlockSpec((1, 1, GLA_H, GLA_DK, GLA_DV), lambda b, t: (0, b, 0, 0, 0)),
            pl.BlockSpec((1, 1, HG_H, HG_DK, HG_DV), lambda b, t: (0, b, 0, 0, 0)),
            state_spec(sample_states[0]), state_spec(sample_states[1]),
            pl.BlockSpec((SUBLANES, GLA_V), row_block),
            pl.BlockSpec((SUBLANES, HG_V), row_block),
        ],
        out_shape=[
            jax.ShapeDtypeStruct((bsz, seq, d), F32),
            jax.ShapeDtypeStruct((DEPTH, bsz, GLA_H, GLA_DK, GLA_DV), F32),
            jax.ShapeDtypeStruct((DEPTH, bsz, HG_H, HG_DK, HG_DV), F32),
            jax.ShapeDtypeStruct(sample_states[0].shape, F32),
            jax.ShapeDtypeStruct(sample_states[1].shape, F32),
            jax.ShapeDtypeStruct((nsample, GLA_V), F32),
            jax.ShapeDtypeStruct((nsample, HG_V), F32),
        ],
        scratch_shapes=[
            pltpu.VMEM((tb, d), BF16),
            pltpu.VMEM((tb, tb), BF16),
            pltpu.VMEM((tb, LANES), BF16),
            pltpu.VMEM((tb, GLA_K), F32),
            pltpu.VMEM((tb, GLA_K), F32),
            pltpu.VMEM((GLA_H, tb, LANES), F32),
            pltpu.VMEM((tb, HG_K), F32),
            pltpu.VMEM((tb, HG_K), F32),
            pltpu.VMEM((HG_H, tb, LANES), F32),
            pltpu.VMEM((3, tb, HG_K), BF16),
            pltpu.VMEM((HG_H, tb, 2 * LANES), BF16),
            pltpu.VMEM((HG_H, tb, LANES), BF16),
            pltpu.VMEM((GLA_H * nch, 2 * LANES, GLA_DV), BF16),
            pltpu.VMEM((GLA_H * nch, GLA_DK, GLA_DV), F32),
            pltpu.VMEM((tb, GLA_V), F32),
            pltpu.VMEM((tb, HG_V), F32),
            pltpu.VMEM((tb, GLA_V), F32),
            pltpu.VMEM((tb, HG_V), F32),
            pltpu.VMEM((tb, d), F32),
            pltpu.VMEM((tb, d), F32),
            pltpu.VMEM((tb, d), BF16),
            pltpu.VMEM((tb, d), BF16),
            pltpu.VMEM((tb, d), F32),
        ],
        compiler_params=pltpu.CompilerParams(
            dimension_semantics=("arbitrary", "arbitrary"), vmem_limit_bytes=VMEM_LIMIT),
        name="prompt_layer",
    )(x, *sample_rows, *sample_states, *weights)


def _sample_proj_kernel(x_ref, win_ref, wga_ref, wglr_ref, bg_ref, lbp_ref,
                        eg_ref, kg_ref, qg_ref, vg_ref, eh_ref, kh_ref, qh_ref, vh_ref,
                        sgr_ref, shr_ref, mga_ref, mgb_ref):
    xb = x_ref[:, 0, :].astype(BF16)
    proj = lambda base, width: _dot(xb, win_ref[:, base:base + width])
    qg_ref[...] = proj(C_GQ, GLA_K) * (GLA_DK ** -0.5)
    kg_ref[...] = proj(C_GK, GLA_K)
    vg_ref[...] = proj(C_GV, GLA_V)
    sgr_ref[...] = _silu(proj(C_GR, GLA_V))
    ga = _dot(xb, wga_ref[...]).astype(BF16)
    a_logit = _dot(ga, wglr_ref[...]) + bg_ref[...]
    eg_ref[...] = jnp.exp(_log_sigmoid(a_logit) * (1.0 / GLA_TAU))
    lb = _lower_bound(lbp_ref[...])
    qh_ref[...] = _silu(proj(C_HQ, HG_K))
    sf, sfn = _sigmoid_pair(proj(C_HF, HG_K))
    eh_ref[...] = jnp.exp(jnp.log(lb + (1.0 - lb) * sf))
    kh_ref[...] = (1.0 - lb) * sfn
    vh_ref[...] = proj(C_HI, HG_V)
    shr_ref[...] = _silu(proj(C_HR, HG_V))
    mga_ref[...] = _sigmoid(proj(C_MA, D_MODEL))
    mgb_ref[...] = _sigmoid(proj(C_MB, D_MODEL))


def _sample_proj_call(x, win, wga, wglr, bg, lbp):
    n = x.shape[0]
    widths = (GLA_K, GLA_K, GLA_K, GLA_V, HG_K, HG_K, HG_K, HG_V, GLA_V, HG_V, D_MODEL, D_MODEL)
    args = (x, win, wga, wglr, bg, lbp)
    return pl.pallas_call(
        _sample_proj_kernel,
        in_specs=[_whole(a.shape) for a in args],
        out_specs=[_whole((n, w)) for w in widths],
        out_shape=[jax.ShapeDtypeStruct((n, w), F32) for w in widths],
        compiler_params=pltpu.CompilerParams(vmem_limit_bytes=VMEM_LIMIT),
        name="sample_proj",
    )(*args)


def _sample_post_kernel(x_ref, oa_ref, ob_ref, sgr_ref, shr_ref, mga_ref, mgb_ref,
                        gng_ref, wbg_ref, hng_ref, wbh_ref, wout_ref, lng_ref, lnb_ref, y_ref):
    ga = (_head_rmsnorm(oa_ref[...], gng_ref[...], GLA_DV) * sgr_ref[...]).astype(BF16)
    gb = (_head_rmsnorm(ob_ref[...], hng_ref[...], HG_DV) * shr_ref[...]).astype(BF16)
    merged = mga_ref[...] * _dot(ga, wbg_ref[...]) + mgb_ref[...] * _dot(gb, wbh_ref[...])
    z = DEEPNORM_ALPHA * x_ref[:, 0, :] + _dot(merged.astype(BF16), wout_ref[...])
    y_ref[:, 0, :] = _layernorm(z, lng_ref[...], lnb_ref[...])


def _sample_post_call(*args):
    n = args[0].shape[0]
    return pl.pallas_call(
        _sample_post_kernel,
        in_specs=[_whole(a.shape) for a in args],
        out_specs=_whole((n, 1, D_MODEL)),
        out_shape=jax.ShapeDtypeStruct((n, 1, D_MODEL), F32),
        compiler_params=pltpu.CompilerParams(vmem_limit_bytes=VMEM_LIMIT),
        name="sample_post",
    )(*args)


def _weight_prep_kernel(wt_ref, gat_ref, win_ref, wga_ref):
    win_ref[...] = wt_ref[...].T.astype(BF16)

    @pl.when(pl.program_id(0) == 0)
    def _():
        d = gat_ref.shape[1]
        ga = jnp.concatenate([gat_ref[...], jnp.zeros((LANES - GLA_RANK, d), F32)], axis=0)
        wga_ref[...] = ga.T.astype(BF16)


def _weight_prep_call(w_in):
    _, d, width = w_in.shape
    wt = jnp.swapaxes(w_in, 1, 2)[0]
    blk = WEIGHT_PREP_COLS
    assert C_HQ % blk == 0 and (width - GLA_RANK) % blk == 0
    src_row = lambda i: (i * (blk // GLA_RANK) + (i >= C_HQ // blk).astype(jnp.int32)) * GLA_RANK
    return pl.pallas_call(
        _weight_prep_kernel,
        grid=((width - GLA_RANK) // blk,),
        in_specs=[pl.BlockSpec((pl.Element(blk), pl.Element(d)), lambda i: (src_row(i), 0)),
                  pl.BlockSpec((pl.Element(GLA_RANK), pl.Element(d)), lambda i: (C_HQ, 0))],
        out_specs=[pl.BlockSpec((d, blk), lambda i: (0, i)),
                   pl.BlockSpec((d, LANES), lambda i: (0, 0))],
        out_shape=[jax.ShapeDtypeStruct((d, width - GLA_RANK), BF16), jax.ShapeDtypeStruct((d, LANES), BF16)],
        compiler_params=pltpu.CompilerParams(dimension_semantics=("arbitrary",), vmem_limit_bytes=VMEM_LIMIT),
        name="weight_prep",
    )(wt, wt)


def kernel(x_prompt, x_sample, state_gla, state_hgrn, w_in, w_gate_lr, b_gate_lr, gla_norm_g, w_br_gla,
           hgrn_lb_param, hgrn_norm_g, w_br_hgrn, w_out, ln_g, ln_b):
    assert w_in.shape[0] == DEPTH and x_sample.shape[1] == 1
    win, wga = _weight_prep_call(w_in)
    wglr = jnp.pad(w_gate_lr[0], ((0, LANES - GLA_RANK), (0, 0))).astype(BF16)
    bg = b_gate_lr[0].reshape(1, GLA_K)
    gng = gla_norm_g[0].reshape(1, GLA_V)
    hng = hgrn_norm_g[0].reshape(1, HG_V)
    wbg = w_br_gla[0].astype(BF16)
    wbh = w_br_hgrn[0].astype(BF16)
    wout = w_out[0].astype(BF16)
    lng = ln_g[0].reshape(1, D_MODEL)
    lnb = ln_b[0].reshape(1, D_MODEL)
    lbp = hgrn_lb_param

    *rows, sgr, shr, mga, mgb = _sample_proj_call(x_sample, win, wga, wglr, bg, lbp)

    y_prompt, gla_p, hgrn_p, gla_s, hgrn_s, oa, ob = _prompt_call(
        x_prompt, rows, (state_gla[0], state_hgrn[0]), win, wga, wglr, bg, gng, wbg, lbp, hng, wbh, wout, lng, lnb)

    ys = _sample_post_call(x_sample, oa, ob, sgr, shr, mga, mgb, gng, wbg, hng, wbh, wout, lng, lnb)
    return (y_prompt, ys, gla_p, hgrn_p, gla_s[None], hgrn_s[None])
```

```python
from typing import Any, NamedTuple

import jax
import jax.numpy as jnp
from jax import lax
from jax.experimental import pallas as pl
from jax.experimental.pallas import tpu as pltpu

F32 = jnp.float32
BF16 = jnp.bfloat16

D_MODEL = 1024
GLA_H, GLA_DK, GLA_DV = 4, 128, 256
HG_H, HG_DK, HG_DV = 8, 128, 128
GLA_RANK = 16
GLA_TAU = 16.0
LOG2_E = 1.4426950408889634
GLA_K = GLA_H * GLA_DK
GLA_V = GLA_H * GLA_DV
HG_K = HG_H * HG_DK
HG_V = HG_H * HG_DV
C_GQ, C_GK, C_GV, C_GR = 0, GLA_K, 2 * GLA_K, 2 * GLA_K + GLA_V
C_HQ = C_GR + GLA_V
C_HF, C_HI, C_HR = C_HQ + HG_K, C_HQ + 2 * HG_K, C_HQ + 2 * HG_K + HG_V
C_MA = C_HR + HG_V
C_MB = C_MA + D_MODEL
CHUNK = 64
SUB = 16
NORM_EPS = 1e-5
DEPTH = 1
DEEPNORM_ALPHA = (2.0 * DEPTH) ** 0.25
LANES = 128
SUBLANES = 8
BF16_SUBLANES = 16
TOK_BLOCK = 256
COL_TILE = 256
ROW_BLOCK = 64
LN_ROW_BLOCK = 16
WEIGHT_PREP_COLS = 1024
VMEM_LIMIT = 60 * 1024 * 1024


def _dot(a, b):
    return jnp.dot(a, b, preferred_element_type=F32)


def _dot_nt(a, b):
    return lax.dot_general(a, b, (((1,), (1,)), ((), ())), preferred_element_type=F32)


def _dot_tn(a, b):
    return lax.dot_general(a, b, (((0,), (0,)), ((), ())), preferred_element_type=F32)


def _sigmoid(x):
    return 1.0 / (1.0 + jnp.exp(-x))


def _sigmoid_pair(x):
    e = jnp.exp(-jnp.abs(x))
    s = 1.0 / (1.0 + e)
    t = e * s
    pos = x >= 0.0
    return jnp.where(pos, s, t), jnp.where(pos, t, s)


def _silu(x):
    return x * _sigmoid(x)


def _log_sigmoid(x):
    return jnp.minimum(x, 0.0) - jnp.log1p(jnp.exp(-jnp.abs(x)))


def _prefix_sum_rows(g):
    row = lax.broadcasted_iota(jnp.int32, (SUBLANES, g.shape[1]), 0)
    parts, carry = [], None
    for j in range(g.shape[0] // SUBLANES):
        p = g[j * SUBLANES:(j + 1) * SUBLANES]
        shift = 1
        while shift < SUBLANES:
            p = p + jnp.where(row >= shift, pltpu.roll(p, shift, axis=0), 0.0)
            shift *= 2
        if carry is not None:
            p = p + carry
        carry = jnp.broadcast_to(p[SUBLANES - 1:SUBLANES, :], p.shape)
        parts.append(p)
    return jnp.concatenate(parts, axis=0)


def _split3(x):
    hi = x.astype(BF16)
    r = x - hi.astype(F32)
    mid = r.astype(BF16)
    lo = (r - mid.astype(F32)).astype(BF16)
    return hi, mid, lo


def _block_tri(n):
    r = lax.broadcasted_iota(jnp.int32, (n, n), 0)
    c = lax.broadcasted_iota(jnp.int32, (n, n), 1)
    shift = CHUNK.bit_length() - 1
    keep = (c <= r) & (jnp.right_shift(r, shift) == jnp.right_shift(c, shift))
    return jnp.where(keep, 1.0, 0.0).astype(BF16)


def _lower_bound(lbp):
    rows = [lbp[i:i + 1, :] for i in range(lbp.shape[0])]
    m = rows[0]
    for r in rows[1:]:
        m = jnp.maximum(m, r)
    es = [jnp.exp(r - m) for r in rows]
    tot = es[0]
    for e in es[1:]:
        tot = tot + e
    return es[0] / tot


def _score_mask():
    t = lax.broadcasted_iota(jnp.int32, (CHUNK, LANES), 0)
    lane = lax.broadcasted_iota(jnp.int32, (CHUNK, LANES), 1)
    s = lane - CHUNK * (jnp.right_shift(t, SUB.bit_length() - 1) & 1)
    return (s >= 0) & (s <= t)


def _intra_scores(q, k, b, keep):
    dk = q.shape[1]
    nsub = CHUNK // SUB
    qs, ks = [], []
    for i in range(nsub):
        lo, hi = i * SUB, (i + 1) * SUB
        bs = b[lo + SUB // 2:lo + SUB // 2 + 1, :]
        qs.append(q[lo:hi] * jnp.exp2(b[lo:hi] - bs))
        ks.append((k[0:hi] * jnp.exp2(bs - b[0:hi])).astype(BF16))
        if hi < CHUNK:
            ks.append(jnp.zeros((CHUNK - hi, dk), BF16))
    prod = _dot_nt(jnp.concatenate(qs, axis=0).astype(BF16), jnp.concatenate(ks, axis=0))
    per = LANES // CHUNK
    a = jnp.concatenate([prod[i * SUB:(i + 1) * SUB, (i // per) * LANES:(i // per + 1) * LANES]
                         for i in range(nsub)], axis=0)
    return jnp.where(keep, a, 0.0)


def _columns(rows):
    r = rows.shape[0]
    if r < LANES:
        rows = jnp.concatenate([rows, jnp.zeros((LANES - r, LANES), F32)], axis=0)
    return rows.T


def _head_rmsnorm(o, gain, dv):
    outs = []
    for j in range(o.shape[1] // dv):
        oj = o[:, j * dv:(j + 1) * dv]
        ms = jnp.mean(oj * oj, axis=-1, keepdims=True)
        outs.append(oj * lax.rsqrt(ms + NORM_EPS))
    on = outs[0] if len(outs) == 1 else jnp.concatenate(outs, axis=1)
    return on * gain


def _layernorm(z, g, b):
    mu = jnp.mean(z, axis=-1, keepdims=True)
    zc = z - mu
    var = jnp.mean(zc * zc, axis=-1, keepdims=True)
    return zc * lax.rsqrt(var + NORM_EPS) * g + b


class _Task(NamedTuple):
    cost: float
    run: Any


def _merge(*lists):
    totals = [sum(t.cost for t in l) or 1.0 for l in lists]
    pos = [0] * len(lists)
    done = [0.0] * len(lists)
    out = []
    for _ in range(sum(len(l) for l in lists)):
        live = [i for i in range(len(lists)) if pos[i] < len(lists[i])]
        j = min(live, key=lambda i: (done[i] + 0.5 * lists[i][pos[i]].cost) / totals[i])
        out.append(lists[j][pos[j]])
        done[j] += lists[j][pos[j]].cost
        pos[j] += 1
    return out


def _lagged(groups, lag=1):
    out = []
    for i, (mms, _) in enumerate(groups):
        out += mms
        if i >= lag:
            out += groups[i - lag][1]
    for _, ews in groups[max(len(groups) - lag, 0):]:
        out += ews
    return out


def _run(tasks):
    for t in tasks:
        t.run()


class _Branch(NamedTuple):
    heads: int
    dk: int
    dv: int
    scale: float
    q: Any
    k: Any
    b: Any
    o: Any
    state: Any


def _slab(buf, hc, dv):
    per = buf.shape[2] // dv
    return hc // per, slice((hc % per) * dv, (hc % per + 1) * dv)


def _store_decay(b_s, n, bt):
    per = COL_TILE // LANES
    for j in range(per):
        b_s[n * per + j] = bt[:, j * LANES:(j + 1) * LANES]


def _store_values(w_s, n, vt, dk, dv, nchunks):
    per = COL_TILE // dv
    for j in range(per):
        for ci in range(nchunks):
            i, cs = _slab(w_s, (n * per + j) * nchunks + ci, dv)
            v = vt[ci * CHUNK:(ci + 1) * CHUNK, j * dv:(j + 1) * dv]
            w_s[i, dk:dk + CHUNK, cs] = v
            w_s[i, dk + CHUNK:dk + 2 * CHUNK, cs] = v


def _recurrence_tasks(br, qa_s, kd_s, w_s, u_s, nchunks):
    nheads, dk, dv = br.heads, br.dk, br.dv
    keep = _score_mask()
    pairs = [(ci, h) for ci in range(nchunks) for h in range(nheads)]
    rows_of = lambda ci: slice(ci * CHUNK, (ci + 1) * CHUNK)
    cols = []

    def operands(ci, h):
        rows, kcols = rows_of(ci), slice(h * dk, (h + 1) * dk)
        q, k, b = br.q[rows, kcols], br.k[rows, kcols], br.b[h, rows, :]
        if br.scale != 1.0:
            q = q * br.scale
        qa_s[h, rows, 0:dk] = (q * jnp.exp2(b)).astype(BF16)
        kd_s[h, rows, :] = (k * jnp.exp2(b[CHUNK - 1:CHUNK, :] - b)).astype(BF16)
        qa_s[h, rows, dk:dk + LANES] = _intra_scores(q, k, b, keep).astype(BF16)

    def outer(ci, h):
        i, cs = _slab(w_s, h * nchunks + ci, dv)
        u_s[i, :, cs] = _dot_tn(kd_s[h, rows_of(ci), :], w_s[i, dk:dk + CHUNK, cs])

    def decay_columns():
        half = CHUNK // 2
        for h in range(nheads):
            cols.append(_columns(jnp.exp2(br.b[h, pl.ds(half - 1, 2 * nchunks, stride=half), :])))

    def chain(ci, h):
        i, cs = _slab(w_s, h * nchunks + ci, dv)
        s = br.state[0, 0, h]
        w_s[i, 0:dk, cs] = s.astype(BF16)
        br.state[0, 0, h] = s * cols[h][:, 2 * ci + 1:2 * ci + 2] + u_s[i, :, cs]

    def output(ci, h):
        i, cs = _slab(w_s, h * nchunks + ci, dv)
        br.o[rows_of(ci), h * dv:(h + 1) * dv] = _dot(qa_s[h, rows_of(ci), :], w_s[i, :, cs])

    bind = lambda cost, fn: [_Task(cost, lambda ci=ci, h=h: fn(ci, h)) for ci, h in pairs]
    return bind(70, operands), (bind(32, outer) + [_Task(16 * nheads, decay_columns)]
                                + bind(8 + dv // 8, chain) + bind(16, output))


def _sample_state_tasks(row0, e_ref, k_ref, q_ref, v_ref, s_ref, snew_ref, o_ref, nheads, dk, dv):
    nseq = s_ref.shape[0]
    first = lax.broadcasted_iota(jnp.int32, (BF16_SUBLANES, dk), 0) == 0
    ks = lambda h: slice(h * dk, (h + 1) * dk)
    vs = lambda h: slice(h * dv, (h + 1) * dv)
    mine = lambda j, width: lax.broadcasted_iota(jnp.int32, (SUBLANES, width), 0) == row0 + j

    def vec(ref, j, cols):
        return jnp.sum(jnp.where(mine(j, cols.stop - cols.start), ref[:, cols], 0.0), axis=0, keepdims=True)

    tile = lambda ref, j, cols: jnp.broadcast_to(vec(ref, j, cols), (BF16_SUBLANES, cols.stop - cols.start))
    pairs = [(h, j) for h in range(nheads) for j in range(nseq)]

    def outer(h, j):
        kj = jnp.where(first, tile(k_ref, j, ks(h)), 0.0).astype(BF16)
        snew_ref[j, h] = _dot_tn(kj, tile(v_ref, j, vs(h)).astype(BF16))

    def update(h, j):
        ecol = jnp.broadcast_to(vec(e_ref, j, ks(h)), (dk, LANES)).T
        ecol = ecol if dv == LANES else jnp.concatenate([ecol] * (dv // LANES), axis=1)
        snew_ref[j, h] = s_ref[j, h] * ecol + snew_ref[j, h]

    def readout(h, j):
        o = _dot(tile(q_ref, j, ks(h)).astype(BF16), snew_ref[j, h].astype(BF16))
        o_ref[:, vs(h)] = jnp.where(mine(j, dv), o[0:SUBLANES, :], o_ref[:, vs(h)])

    bind = lambda cost, fn: [_Task(cost, lambda h=h, j=j: fn(h, j)) for h, j in pairs]
    return bind(40.0, outer), bind(30.0, update), bind(20.0, readout)


def _prompt_kernel(x_ref, seg_ref, skg_ref, sqg_ref, svg_ref, seh_ref, skh_ref, sqh_ref, svh_ref, ssg_ref, ssh_ref,
                   win_ref, wga_ref, wglr_ref, bg_ref, gng_ref, wbg_ref,
                   lbp_ref, hng_ref, wbh_ref, wout_ref, lng_ref, lnb_ref,
                   y_ref, sg_ref, sh_ref, ssg_new_ref, ssh_new_ref, soa_ref, sob_ref,
                   xb_s, tri_s, ga_s, qg_s, kg_s, bgl_s, qh_s, kh_s, bh_s, h3_s, qa_s, kd_s, w_s, u_s,
                   og_s, oh_s, sgr_s, shr_s, mga_s, mgb_s, g_s, m_s, p_s):
    tb = TOK_BLOCK
    nch = tb // CHUNK
    tiles = lambda width: range(width // COL_TILE)
    col = lambda n, base=0: slice(base + n * COL_TILE, base + (n + 1) * COL_TILE)
    gla = _Branch(GLA_H, GLA_DK, GLA_DV, GLA_DK ** -0.5, qg_s, kg_s, bgl_s, og_s, sg_ref)
    hgrn = _Branch(HG_H, HG_DK, HG_DV, 1.0, qh_s, kh_s, bh_s, oh_s, sh_ref)
    mm_cost = 256.0

    @pl.when(pl.program_id(1) == 0)
    def _():
        sg_ref[...] = jnp.zeros_like(sg_ref)
        sh_ref[...] = jnp.zeros_like(sh_ref)
        tri_s[...] = _block_tri(tb)

    nseq = ssg_ref.shape[0]
    lin = pl.program_id(0) * pl.num_programs(1) + pl.program_id(1)
    row0 = (lin % (SUBLANES // nseq)) * nseq

    @pl.when(row0 == 0)
    def _():
        soa_ref[...] = jnp.zeros_like(soa_ref)
        sob_ref[...] = jnp.zeros_like(sob_ref)

    xb_s[...] = x_ref[0].astype(BF16)
    lb = _lower_bound(lbp_ref[...])

    def mm(dst, w_ref, wcols, lhs=xb_s, cast=None, cost=mm_cost):
        def run():
            r = _dot(lhs[...], w_ref[:, wcols])
            dst(r if cast is None else r.astype(cast))
        return _Task(cost, run)

    def store(ref, cols):
        def put(v):
            ref[:, cols] = v
        return put

    def ew(cost, fn, rows=(0, tb), step=ROW_BLOCK):
        def run():
            for r in range(rows[0], rows[1], step):
                fn(slice(r, r + step))
        return _Task(cost, run)

    def cumsum_task(b_s, n):
        def run():
            t = tri_s[...]
            _store_decay(b_s, n, _dot(t, h3_s[0, :, col(n)]) + _dot(t, h3_s[1, :, col(n)]) + _dot(t, h3_s[2, :, col(n)]))
        return _Task(192.0, run)

    def put_split3(rb, cs, g):
        hi, mid, lo = _split3(g)
        h3_s[0, rb, cs] = hi
        h3_s[1, rb, cs] = mid
        h3_s[2, rb, cs] = lo

    def gla_logit(n):
        def run():
            qg_s[:, col(n)] = _dot(ga_s[...], wglr_ref[:, col(n)]) + bg_ref[:, col(n)]
        return _Task(64.0, run)

    def gla_log(n):
        def fn(rb):
            for j in range(COL_TILE // LANES):
                hc = slice(n * COL_TILE + j * LANES, n * COL_TILE + (j + 1) * LANES)
                bgl_s[n * (COL_TILE // LANES) + j, rb, :] = _prefix_sum_rows(
                    _log_sigmoid(qg_s[rb, hc]) * (LOG2_E / GLA_TAU))
        return ew(300.0, fn, step=CHUNK)

    gla_in = [mm(store(ga_s, slice(None)), wga_ref, slice(None), cast=BF16, cost=128.0)]
    gla_in += [mm(store(kg_s, col(n)), win_ref, col(n, C_GK)) for n in tiles(GLA_K)]
    gla_in += [gla_logit(n) for n in tiles(GLA_K)]
    gla_in += [t for n in tiles(GLA_K) for t in (
        mm(lambda v, n=n: _store_values(w_s, n, v, GLA_DK, GLA_DV, nch), win_ref, col(n, C_GV), cast=BF16),
        gla_log(n),
        mm(lambda v, n=n: _store_values(w_s, n + GLA_K // COL_TILE, v, GLA_DK, GLA_DV, nch), win_ref,
           col(n + GLA_K // COL_TILE, C_GV), cast=BF16))]
    gla_in += [mm(store(qg_s, col(n)), win_ref, col(n, C_GQ)) for n in tiles(GLA_K)]
    _run(gla_in)

    def hq_ew(n):
        def fn(rb):
            qh_s[rb, col(n)] = _silu(qh_s[rb, col(n)])
        return ew(130.0, fn)

    def hf_ew(n):
        def fn(rb):
            sf, sfn = _sigmoid_pair(kh_s[rb, col(n)])
            lbn = lb[:, col(n)]
            kh_s[rb, col(n)] = (1.0 - lbn) * sfn
            put_split3(rb, col(n), jnp.log2(lbn + (1.0 - lbn) * sf))
        return ew(450.0, fn)

    def gate_ew(ref, act, n):
        def fn(rb):
            ref[rb, col(n)] = act(ref[rb, col(n)])
        return ew(130.0, fn)

    hg_in = [([mm(store(kh_s, col(n)), win_ref, col(n, C_HF)),
               mm(store(qh_s, col(n)), win_ref, col(n, C_HQ))],
              [hf_ew(n), hq_ew(n), cumsum_task(bh_s, n)]) for n in tiles(HG_K)]
    gates = [([mm(store(sgr_s, col(n)), win_ref, col(n, C_GR))], [gate_ew(sgr_s, _silu, n)]) for n in tiles(GLA_V)]
    gates += [([mm(store(mga_s, col(n)), win_ref, col(n, C_MA))], [gate_ew(mga_s, _sigmoid, n)])
              for n in tiles(D_MODEL)]
    gates_b = [([mm(store(shr_s, col(n)), win_ref, col(n, C_HR))], [gate_ew(shr_s, _silu, n)]) for n in tiles(HG_V)]
    hg_v = [mm(lambda v, n=n: _store_values(w_s, n, v, HG_DK, HG_DV, nch), win_ref, col(n, C_HI), cast=BF16)
            for n in tiles(HG_V)]
    mgb = [([mm(store(mgb_s, col(n)), win_ref, col(n, C_MB))], [gate_ew(mgb_s, _sigmoid, n)]) for n in tiles(D_MODEL)]

    ra_ops, ra_rest = _recurrence_tasks(gla, qa_s, kd_s, w_s, u_s, nch)
    s_outer, s_update, s_readout = [], [], []
    for args in ((seg_ref, skg_ref, sqg_ref, svg_ref, ssg_ref, ssg_new_ref, soa_ref, GLA_H, GLA_DK, GLA_DV),
                 (seh_ref, skh_ref, sqh_ref, svh_ref, ssh_ref, ssh_new_ref, sob_ref, HG_H, HG_DK, HG_DV)):
        outer, update, readout = _sample_state_tasks(row0, *args)
        s_outer += outer
        s_update += update
        s_readout += readout

    _run(_merge(ra_ops + ra_rest, _lagged(hg_in + gates), s_outer))

    def gla_norm(n):
        def fn(rb):
            g_s[rb, col(n)] = (_head_rmsnorm(og_s[rb, col(n)], gng_ref[:, col(n)], GLA_DV) * sgr_s[rb, col(n)]).astype(BF16)
        return ew(200.0, fn)

    def gla_scale(n):
        def fn(rb):
            p_s[rb, col(n)] = p_s[rb, col(n)] * mga_s[rb, col(n)]
        return ew(50.0, fn)

    gla_post = [gla_norm(n) for n in tiles(GLA_V)]
    for n in tiles(D_MODEL):
        gla_post += [mm(store(p_s, col(n)), wbg_ref, col(n), lhs=g_s), gla_scale(n)]
    rb_ops, rb_rest = _recurrence_tasks(hgrn, qa_s, kd_s, w_s, u_s, nch)
    _run(_merge(rb_ops, hg_v + _lagged(gates_b)))
    _run(_merge(rb_rest, gla_post, s_update))

    def hgrn_norm(n):
        def fn(rb):
            m_s[rb, col(n)] = (_head_rmsnorm(oh_s[rb, col(n)], hng_ref[:, col(n)], HG_DV) * shr_s[rb, col(n)]).astype(BF16)
        return ew(220.0, fn)

    def merge_ew(n):
        def fn(rb):
            g_s[rb, col(n)] = (p_s[rb, col(n)] + mgb_s[rb, col(n)] * sgr_s[rb, col(n)]).astype(BF16)
        return ew(60.0, fn)

    def final_ew(rows):
        def fn(rb):
            z = DEEPNORM_ALPHA * x_ref[0, rb, :] + shr_s[rb, :]
            y_ref[0, rb, :] = _layernorm(z, lng_ref[...], lnb_ref[...])
        return ew(180.0, fn, rows=rows, step=LN_ROW_BLOCK)

    _run(_merge([hgrn_norm(n) for n in tiles(HG_V)], _lagged(mgb)))
    _run([mm(store(sgr_s, col(n)), wbh_ref, col(n), lhs=m_s) for n in tiles(D_MODEL)])
    _run([merge_ew(n) for n in tiles(D_MODEL)])
    tail = [mm(store(shr_s, col(n)), wout_ref, col(n), lhs=g_s) for n in tiles(D_MODEL)]
    tail += [final_ew((r, r + ROW_BLOCK)) for r in range(0, tb, ROW_BLOCK)]
    _run(_merge(tail, s_readout))


def _whole(shape):
    return pl.BlockSpec(memory_space=pltpu.VMEM)


def _prompt_call(x, sample_rows, sample_states, win, wga, wglr, bg, gng, wbg, lbp, hng, wbh, wout, lng, lnb):
    bsz, seq, d = x.shape
    tb = TOK_BLOCK
    nch = tb // CHUNK
    per_row = seq // tb
    steps = bsz * per_row
    nsample = sample_states[0].shape[0]
    nseq = nsample // steps
    assert nseq * steps == nsample and SUBLANES % nseq == 0 and all(r.shape[0] == nsample for r in sample_rows)
    weights = (win, wga, wglr, bg, gng, wbg, lbp, hng, wbh, wout, lng, lnb)
    lin = lambda b, t: b * per_row + t
    row_block = lambda b, t: (lin(b, t) // (SUBLANES // nseq), 0)
    row_spec = lambda r: pl.BlockSpec((SUBLANES, r.shape[1]), row_block)
    state_spec = lambda s: pl.BlockSpec((nseq,) + s.shape[1:], lambda b, t: (lin(b, t), 0, 0, 0))
    return pl.pallas_call(
        _prompt_kernel,
        grid=(bsz, per_row),
        in_specs=[pl.BlockSpec((1, tb, d), lambda b, t: (b, t, 0))] + [row_spec(r) for r in sample_rows]
        + [state_spec(s) for s in sample_states] + [_whole(w.shape) for w in weights],
        out_specs=[
            pl.BlockSpec((1, tb, d), lambda b, t: (b, t, 0)),
            pl.BlockSpec((1, 1, GLA_H, GLA_DK, GLA_DV), lambda b, t: (0, b, 0, 0, 0)),
            pl.BlockSpec((1, 1, HG_H, HG_DK, HG_DV), lambda b, t: (0, b, 0, 0, 0)),
            state_spec(sample_states[0]), state_spec(sample_states[1]),
            pl.BlockSpec((SUBLANES, GLA_V), row_block),
            pl.BlockSpec((SUBLANES, HG_V), row_block),
        ],
        out_shape=[
            jax.ShapeDtypeStruct((bsz, seq, d), F32),
            jax.ShapeDtypeStruct((DEPTH, bsz, GLA_H, GLA_DK, GLA_DV), F32),
            jax.ShapeDtypeStruct((DEPTH, bsz, HG_H, HG_DK, HG_DV), F32),
            jax.ShapeDtypeStruct(sample_states[0].shape, F32),
            jax.ShapeDtypeStruct(sample_states[1].shape, F32),
            jax.ShapeDtypeStruct((nsample, GLA_V), F32),
            jax.ShapeDtypeStruct((nsample, HG_V), F32),
        ],
        scratch_shapes=[
            pltpu.VMEM((tb, d), BF16),
            pltpu.VMEM((tb, tb), BF16),
            pltpu.VMEM((tb, LANES), BF16),
            pltpu.VMEM((tb, GLA_K), F32),
            pltpu.VMEM((tb, GLA_K), F32),
            pltpu.VMEM((GLA_H, tb, LANES), F32),
            pltpu.VMEM((tb, HG_K), F32),
            pltpu.VMEM((tb, HG_K), F32),
            pltpu.VMEM((HG_H, tb, LANES), F32),
            pltpu.VMEM((3, tb, HG_K), BF16),
            pltpu.VMEM((HG_H, tb, 2 * LANES), BF16),
            pltpu.VMEM((HG_H, tb, LANES), BF16),
            pltpu.VMEM((GLA_H * nch, 2 * LANES, GLA_DV), BF16),
            pltpu.VMEM((GLA_H * nch, GLA_DK, GLA_DV), F32),
            pltpu.VMEM((tb, GLA_V), F32),
            pltpu.VMEM((tb, HG_V), F32),
            pltpu.VMEM((tb, GLA_V), F32),
            pltpu.VMEM((tb, HG_V), F32),
            pltpu.VMEM((tb, d), F32),
            pltpu.VMEM((tb, d), F32),
            pltpu.VMEM((tb, d), BF16),
            pltpu.VMEM((tb, d), BF16),
            pltpu.VMEM((tb, d), F32),
        ],
        compiler_params=pltpu.CompilerParams(
            dimension_semantics=("arbitrary", "arbitrary"), vmem_limit_bytes=VMEM_LIMIT),
        name="prompt_layer",
    )(x, *sample_rows, *sample_states, *weights)


def _sample_proj_kernel(x_ref, win_ref, wga_ref, wglr_ref, bg_ref, lbp_ref,
                        eg_ref, kg_ref, qg_ref, vg_ref, eh_ref, kh_ref, qh_ref, vh_ref,
                        sgr_ref, shr_ref, mga_ref, mgb_ref):
    xb = x_ref[:, 0, :].astype(BF16)
    proj = lambda base, width: _dot(xb, win_ref[:, base:base + width])
    qg_ref[...] = proj(C_GQ, GLA_K) * (GLA_DK ** -0.5)
    kg_ref[...] = proj(C_GK, GLA_K)
    vg_ref[...] = proj(C_GV, GLA_V)
    sgr_ref[...] = _silu(proj(C_GR, GLA_V))
    ga = _dot(xb, wga_ref[...]).astype(BF16)
    a_logit = _dot(ga, wglr_ref[...]) + bg_ref[...]
    eg_ref[...] = jnp.exp(_log_sigmoid(a_logit) * (1.0 / GLA_TAU))
    lb = _lower_bound(lbp_ref[...])
    qh_ref[...] = _silu(proj(C_HQ, HG_K))
    sf, sfn = _sigmoid_pair(proj(C_HF, HG_K))
    eh_ref[...] = jnp.exp(jnp.log(lb + (1.0 - lb) * sf))
    kh_ref[...] = (1.0 - lb) * sfn
    vh_ref[...] = proj(C_HI, HG_V)
    shr_ref[...] = _silu(proj(C_HR, HG_V))
    mga_ref[...] = _sigmoid(proj(C_MA, D_MODEL))
    mgb_ref[...] = _sigmoid(proj(C_MB, D_MODEL))


def _sample_proj_call(x, win, wga, wglr, bg, lbp):
    n = x.shape[0]
    widths = (GLA_K, GLA_K, GLA_K, GLA_V, HG_K, HG_K, HG_K, HG_V, GLA_V, HG_V, D_MODEL, D_MODEL)
    args = (x, win, wga, wglr, bg, lbp)
    return pl.pallas_call(
        _sample_proj_kernel,
        in_specs=[_whole(a.shape) for a in args],
        out_specs=[_whole((n, w)) for w in widths],
        out_shape=[jax.ShapeDtypeStruct((n, w), F32) for w in widths],
        compiler_params=pltpu.CompilerParams(vmem_limit_bytes=VMEM_LIMIT),
        name="sample_proj",
    )(*args)


def _sample_post_kernel(x_ref, oa_ref, ob_ref, sgr_ref, shr_ref, mga_ref, mgb_ref,
                        gng_ref, wbg_ref, hng_ref, wbh_ref, wout_ref, lng_ref, lnb_ref, y_ref):
    ga = (_head_rmsnorm(oa_ref[...], gng_ref[...], GLA_DV) * sgr_ref[...]).astype(BF16)
    gb = (_head_rmsnorm(ob_ref[...], hng_ref[...], HG_DV) * shr_ref[...]).astype(BF16)
    merged = mga_ref[...] * _dot(ga, wbg_ref[...]) + mgb_ref[...] * _dot(gb, wbh_ref[...])
    z = DEEPNORM_ALPHA * x_ref[:, 0, :] + _dot(merged.astype(BF16), wout_ref[...])
    y_ref[:, 0, :] = _layernorm(z, lng_ref[...], lnb_ref[...])


def _sample_post_call(*args):
    n = args[0].shape[0]
    return pl.pallas_call(
        _sample_post_kernel,
        in_specs=[_whole(a.shape) for a in args],
        out_specs=_whole((n, 1, D_MODEL)),
        out_shape=jax.ShapeDtypeStruct((n, 1, D_MODEL), F32),
        compiler_params=pltpu.CompilerParams(vmem_limit_bytes=VMEM_LIMIT),
        name="sample_post",
    )(*args)


def _weight_prep_kernel(wt_ref, gat_ref, win_ref, wga_ref):
    win_ref[...] = wt_ref[...].T.astype(BF16)

    @pl.when(pl.program_id(0) == 0)
    def _():
        d = gat_ref.shape[1]
        ga = jnp.concatenate([gat_ref[...], jnp.zeros((LANES - GLA_RANK, d), F32)], axis=0)
        wga_ref[...] = ga.T.astype(BF16)


def _weight_prep_call(w_in):
    _, d, width = w_in.shape
    wt = jnp.swapaxes(w_in, 1, 2)[0]
    blk = WEIGHT_PREP_COLS
    assert C_HQ % blk == 0 and (width - GLA_RANK) % blk == 0
    src_row = lambda i: (i * (blk // GLA_RANK) + (i >= C_HQ // blk).astype(jnp.int32)) * GLA_RANK
    return pl.pallas_call(
        _weight_prep_kernel,
        grid=((width - GLA_RANK) // blk,),
        in_specs=[pl.BlockSpec((pl.Element(blk), pl.Element(d)), lambda i: (src_row(i), 0)),
                  pl.BlockSpec((pl.Element(GLA_RANK), pl.Element(d)), lambda i: (C_HQ, 0))],
        out_specs=[pl.BlockSpec((d, blk), lambda i: (0, i)),
                   pl.BlockSpec((d, LANES), lambda i: (0, 0))],
        out_shape=[jax.ShapeDtypeStruct((d, width - GLA_RANK), BF16), jax.ShapeDtypeStruct((d, LANES), BF16)],
        compiler_params=pltpu.CompilerParams(dimension_semantics=("arbitrary",), vmem_limit_bytes=VMEM_LIMIT),
        name="weight_prep",
    )(wt, wt)


def kernel(x_prompt, x_sample, state_gla, state_hgrn, w_in, w_gate_lr, b_gate_lr, gla_norm_g, w_br_gla,
           hgrn_lb_param, hgrn_norm_g, w_br_hgrn, w_out, ln_g, ln_b):
    assert w_in.shape[0] == DEPTH and x_sample.shape[1] == 1
    win, wga = _weight_prep_call(w_in)
    wglr = jnp.pad(w_gate_lr[0], ((0, LANES - GLA_RANK), (0, 0))).astype(BF16)
    bg = b_gate_lr[0].reshape(1, GLA_K)
    gng = gla_norm_g[0].reshape(1, GLA_V)
    hng = hgrn_norm_g[0].reshape(1, HG_V)
    wbg = w_br_gla[0].astype(BF16)
    wbh = w_br_hgrn[0].astype(BF16)
    wout = w_out[0].astype(BF16)
    lng = ln_g[0].reshape(1, D_MODEL)
    lnb = ln_b[0].reshape(1, D_MODEL)
    lbp = hgrn_lb_param

    *rows, sgr, shr, mga, mgb = _sample_proj_call(x_sample, win, wga, wglr, bg, lbp)

    y_prompt, gla_p, hgrn_p, gla_s, hgrn_s, oa, ob = _prompt_call(
        x_prompt, rows, (state_gla[0], state_hgrn[0]), win, wga, wglr, bg, gng, wbg, lbp, hng, wbh, wout, lng, lnb)

    ys = _sample_post_call(x_sample, oa, ob, sgr, shr, mga, mgb, gng, wbg, hng, wbh, wout, lng, lnb)
    return (y_prompt, ys, gla_p, hgrn_p, gla_s[None], hgrn_s[None])
```
